```python
import jax
import jax.numpy as jnp
from jax import lax
import numpy as np

D_MODEL = 1024
BATCH = 32
SEQ = 2048
DEPTH = 4

MLSTM_HEADS = 4
MLSTM_DH = 64
MLSTM_CHUNK = 64
CONV_K = 4
HGRN_HEADS = 4
HGRN_DK = 64
HGRN_DV = 64
HGRN_CHUNK = 32
SWA_Q_HEADS = 8
SWA_KV_HEADS = 2
SWA_DH = 64
WINDOW = 128
ROPE_THETA = 500000.0
ROT_DIM = SWA_DH // 4
D_FF = 4 * D_MODEL
PLE_DIM = 256
EPS = 1e-6

MLSTM_W = MLSTM_HEADS * MLSTM_DH
HGRN_KW = HGRN_HEADS * HGRN_DK
HGRN_VW = HGRN_HEADS * HGRN_DV
SWA_QW = SWA_Q_HEADS * SWA_DH
SWA_KVW = SWA_KV_HEADS * SWA_DH
MIX_W = MLSTM_W + HGRN_VW + SWA_QW
IN_SPLITS = (MLSTM_W, MLSTM_W, MLSTM_W, MLSTM_W, MLSTM_HEADS, MLSTM_HEADS,
             HGRN_KW, HGRN_KW, HGRN_VW, HGRN_VW, SWA_QW, SWA_KVW, SWA_KVW)
IN_W = 4 * MLSTM_W + 2 * MLSTM_HEADS + 2 * HGRN_KW + 2 * HGRN_VW + SWA_QW + 2 * SWA_KVW

kernel_name = "hybrid_mlstm_hgrn2_swa_trunk"


def _rms_norm(x, g):
    xf = x.astype(jnp.float32)
    y = xf * lax.rsqrt(jnp.mean(xf * xf, axis=-1, keepdims=True) + EPS) * g.astype(jnp.float32)
    return y.astype(x.dtype)


def _head_rms_norm(x, g, n_heads):
    B, S, W = x.shape
    d = W // n_heads
    y = _rms_norm(x.reshape(B, S, n_heads, d), g.reshape(n_heads, d))
    return y.reshape(B, S, W)


def _split_cols(y, sizes):
    out = []
    o = 0
    for s in sizes:
        out.append(y[..., o:o + s])
        o += s
    return out


def _heads(t, n_heads):
    B, S, W = t.shape
    return t.reshape(B, S, n_heads, W // n_heads)


def _to_chunks(t, L):
    B, S, H = t.shape[:3]
    t = t.reshape((B, S // L, L, H) + t.shape[3:])
    return jnp.moveaxis(t, (1, 3), (0, 2))


def _from_chunks(t):
    nc, B, H, L, d = t.shape
    return jnp.moveaxis(t, (0, 2), (1, 3)).reshape(B, nc * L, H * d)


def _causal_conv(u, w, b):
    K = w.shape[0]
    y = lax.conv_general_dilated(u, w[:, None, :].astype(u.dtype), window_strides=(1,),
                                 padding=[(K - 1, 0)], dimension_numbers=('NWC', 'WIO', 'NWC'),
                                 feature_group_count=u.shape[-1])
    return y + b.astype(u.dtype)


def _rope_tables(positions):
    inv_freq = ROPE_THETA ** (-jnp.arange(0, ROT_DIM, 2, dtype=jnp.float32) / ROT_DIM)
    ang = positions.astype(jnp.float32)[..., None] * inv_freq
    return jnp.cos(ang)[:, :, None, :], jnp.sin(ang)[:, :, None, :]


def _partial_rope(x, cos, sin):
    xf = x.astype(jnp.float32)
    half = ROT_DIM // 2
    x1, x2, rest = xf[..., :half], xf[..., half:ROT_DIM], xf[..., ROT_DIM:]
    y = jnp.concatenate([x1 * cos - x2 * sin, x2 * cos + x1 * sin, rest], axis=-1)
    return y.astype(x.dtype)


def mlstm_chunkwise(q, k, v, i_pre, f_pre):
    f32 = jnp.float32
    B, S, H, dk = q.shape
    dv = v.shape[-1]
    L = MLSTM_CHUNK
    xs = (_to_chunks(q.astype(f32), L), _to_chunks(k.astype(f32), L), _to_chunks(v.astype(f32), L),
          _to_chunks(i_pre.astype(f32), L), _to_chunks(jax.nn.log_sigmoid(f_pre.astype(f32)), L))
    causal = jnp.tril(jnp.ones((L, L), dtype=bool))

    def step(carry, inp):
        C, n, m = carry
        q_, k_, v_, i_, lf = inp
        b = jnp.cumsum(lf, axis=-1)
        dmat = jnp.where(causal, b[..., :, None] - b[..., None, :] + i_[..., None, :], -jnp.inf)
        inter = b + m[..., None]
        m_j = jnp.maximum(inter, jnp.max(dmat, axis=-1))
        w_intra = jnp.exp(dmat - m_j[..., None])
        w_inter = jnp.exp(inter - m_j)
        s = jnp.einsum('bhld,bhsd->bhls', q_, k_) * w_intra
        num = jnp.einsum('bhls,bhse->bhle', s, v_) + w_inter[..., None] * jnp.einsum('bhld,bhde->bhle', q_, C)
        den = jnp.sum(s, axis=-1) + w_inter * jnp.einsum('bhld,bhd->bhl', q_, n)
        h = num / jnp.maximum(jnp.abs(den), jnp.exp(-m_j))[..., None]
        m_new = m_j[..., -1]
        w_s = jnp.exp(b[..., -1:] - b + i_ - m_new[..., None])
        decay = jnp.exp(b[..., -1] + m - m_new)
        C_new = decay[..., None, None] * C + jnp.einsum('bhs,bhsd,bhse->bhde', w_s, k_, v_)
        n_new = decay[..., None] * n + jnp.einsum('bhs,bhsd->bhd', w_s, k_)
        return (C_new, n_new, m_new), h

    init = (jnp.zeros((B, H, dk, dv), f32), jnp.zeros((B, H, dk), f32), jnp.zeros((B, H), f32))
    _, h = lax.scan(step, init, xs)
    return _from_chunks(h)


def hgrn2_chunkwise(q, f_pre, inp, lb):
    f32 = jnp.float32
    B, S, H, dk = q.shape
    dv = inp.shape[-1]
    L = HGRN_CHUNK
    fp = f_pre.astype(f32)
    lb = lb.astype(f32)
    log_f = jnp.logaddexp(jnp.log(lb), jnp.log1p(-lb) + jax.nn.log_sigmoid(fp))
    key = (1.0 - lb) * jax.nn.sigmoid(-fp)
    qf = jax.nn.silu(q.astype(f32))
    xs = (_to_chunks(qf, L), _to_chunks(key, L), _to_chunks(inp.astype(f32), L), _to_chunks(log_f, L))
    causal = jnp.tril(jnp.ones((L, L), dtype=bool))[:, :, None]

    def step(S_state, chunk):
        q_, k_, v_, lf = chunk
        G = jnp.cumsum(lf, axis=2)
        inter = jnp.einsum('bhld,bhde->bhle', q_ * jnp.exp(G), S_state)
        diff = G[:, :, :, None, :] - G[:, :, None, :, :]
        decay = jnp.exp(jnp.where(causal, diff, -jnp.inf))
        A = jnp.einsum('bhld,bhlsd,bhsd->bhls', q_, decay, k_)
        o = inter + jnp.einsum('bhls,bhse->bhle', A, v_)
        G_last = G[:, :, -1]
        S_new = jnp.exp(G_last)[..., None] * S_state + jnp.einsum(
            'bhsd,bhse->bhde', k_ * jnp.exp(G_last[:, :, None] - G), v_)
        return S_new, o

    _, o = lax.scan(step, jnp.zeros((B, H, dk, dv), f32), xs)
    return _from_chunks(o)


def swa_sink_attention(q, k, v, sinks):
    B, S, HQ, d = q.shape
    HKV = k.shape[2]
    G = HQ // HKV
    W = WINDOW
    nb = S // W
    qb = q.reshape(B, nb, W, HKV, G, d)
    kb = k.reshape(B, nb, W, HKV, d)
    vb = v.reshape(B, nb, W, HKV, d)
    kk = jnp.concatenate([jnp.concatenate([jnp.zeros_like(kb[:, :1]), kb[:, :-1]], axis=1), kb], axis=2)
    vv = jnp.concatenate([jnp.concatenate([jnp.zeros_like(vb[:, :1]), vb[:, :-1]], axis=1), vb], axis=2)
    s = jnp.einsum('bnqhgd,bnkhd->bnhgqk', qb, kk).astype(jnp.float32) * (d ** -0.5)
    qi = jnp.arange(W)[:, None]
    ki = jnp.arange(2 * W)[None, :]
    blk = jnp.arange(nb)[:, None, None]
    mask = (ki > qi) & (ki <= qi + W) & ((blk > 0) | (ki >= W))
    s = jnp.where(mask[None, :, None, None], s, -jnp.inf)
    sink = sinks.astype(jnp.float32).reshape(HKV, G)[None, None, :, :, None, None]
    m = jnp.maximum(jnp.max(s, axis=-1, keepdims=True), sink)
    pexp = jnp.exp(s - m)
    probs = pexp / (jnp.sum(pexp, axis=-1, keepdims=True) + jnp.exp(sink - m))
    o = jnp.einsum('bnhgqk,bnkhd->bnqhgd', probs.astype(v.dtype), vv)
    return o.reshape(B, S, HQ * d)


def setup_inputs(seed: int = 0) -> dict:
    key = jax.random.key(seed)
    ks = jax.random.split(key, 24)
    f32 = jnp.float32
    nrm = lambda k, shape, scale: jax.random.normal(k, shape, f32) * scale
    x = nrm(ks[0], (BATCH, SEQ, D_MODEL), 1.0)
    p = nrm(ks[1], (DEPTH, BATCH, SEQ, PLE_DIM), 1.0)
    start = jax.random.randint(ks[2], (BATCH, 1), 0, 4096, dtype=jnp.int32)
    positions = (start + jnp.arange(SEQ, dtype=jnp.int32)[None, :]).astype(jnp.int32)
    return {
        "x": x,
        "p": p,
        "positions": positions,
        "in_norm_g": 1.0 + nrm(ks[3], (DEPTH, D_MODEL), 0.05),
        "w_in": nrm(ks[4], (DEPTH, D_MODEL, IN_W), D_MODEL ** -0.5),
        "b_in": nrm(ks[5], (DEPTH, IN_W), 0.02),
        "mlstm_f_bias": jnp.linspace(3.0, 6.0, MLSTM_HEADS, dtype=f32)[None, :] + nrm(ks[6], (DEPTH, MLSTM_HEADS), 0.1),
        "mlstm_conv_w": nrm(ks[7], (DEPTH, CONV_K, 2 * MLSTM_W), CONV_K ** -0.5),
        "mlstm_conv_b": nrm(ks[8], (DEPTH, 2 * MLSTM_W), 0.02),
        "mlstm_norm_g": 1.0 + nrm(ks[9], (DEPTH, MLSTM_W), 0.05),
        "hgrn_lb_logits": nrm(ks[10], (DEPTH, HGRN_KW), 0.1),
        "hgrn_norm_g": 1.0 + nrm(ks[11], (DEPTH, HGRN_VW), 0.05),
        "swa_q_norm_g": 1.0 + nrm(ks[12], (DEPTH, SWA_DH), 0.05),
        "swa_k_norm_g": 1.0 + nrm(ks[13], (DEPTH, SWA_DH), 0.05),
        "swa_sinks": nrm(ks[14], (DEPTH, SWA_Q_HEADS), 0.5),
        "w_out": nrm(ks[15], (DEPTH, MIX_W, D_MODEL), MIX_W ** -0.5),
        "mlp_norm_g": 1.0 + nrm(ks[16], (DEPTH, D_MODEL), 0.05),
        "w_up": nrm(ks[17], (DEPTH, D_MODEL, D_FF), D_MODEL ** -0.5),
        "w_down": nrm(ks[18], (DEPTH, D_FF, D_MODEL), D_FF ** -0.5),
        "ple_norm_g": 1.0 + nrm(ks[19], (DEPTH, D_MODEL), 0.05),
        "w_ple_gate": nrm(ks[20], (DEPTH, D_MODEL, D_MODEL), D_MODEL ** -0.5),
        "w_ple_proj": nrm(ks[21], (DEPTH, PLE_DIM, D_MODEL), PLE_DIM ** -0.5),
        "ple_post_norm_g": 1.0 + nrm(ks[22], (DEPTH, D_MODEL), 0.05),
    }


def reference(x, p, positions, in_norm_g, w_in, b_in, mlstm_f_bias, mlstm_conv_w, mlstm_conv_b,
              mlstm_norm_g, hgrn_lb_logits, hgrn_norm_g, swa_q_norm_g, swa_k_norm_g, swa_sinks,
              w_out, mlp_norm_g, w_up, w_down, ple_norm_g, w_ple_gate, w_ple_proj, ple_post_norm_g):
    cos, sin = _rope_tables(positions)
    lb_all = jnp.cumsum(jax.nn.softmax(hgrn_lb_logits.astype(jnp.float32), axis=0), axis=0)
    lb_all = lb_all - lb_all[0:1]
    B, S, _ = x.shape
    for l in range(DEPTH):
        h = _rms_norm(x, in_norm_g[l])
        y = h @ w_in[l] + b_in[l]
        (mq, mk, mv, mo, mi, mf, hq, hf, hi, hg, sq, sk, sv) = _split_cols(y, IN_SPLITS)

        qk = jax.nn.silu(_causal_conv(jnp.concatenate([mq, mk], axis=-1), mlstm_conv_w[l], mlstm_conv_b[l]))
        mq_c, mk_c = qk[..., :MLSTM_W], qk[..., MLSTM_W:]
        m_h = mlstm_chunkwise(_heads(mq_c, MLSTM_HEADS), _heads(mk_c, MLSTM_HEADS) * (MLSTM_DH ** -0.5),
                              _heads(mv, MLSTM_HEADS), mi, mf + mlstm_f_bias[l])
        m_out = (_head_rms_norm(m_h, mlstm_norm_g[l], MLSTM_HEADS)
                 * jax.nn.sigmoid(mo.astype(jnp.float32))).astype(x.dtype)

        h_h = hgrn2_chunkwise(_heads(hq, HGRN_HEADS), _heads(hf, HGRN_HEADS), _heads(hi, HGRN_HEADS),
                              lb_all[l].reshape(HGRN_HEADS, HGRN_DK))
        h_out = (_head_rms_norm(h_h, hgrn_norm_g[l], HGRN_HEADS)
                 * jax.nn.silu(hg.astype(jnp.float32))).astype(x.dtype)

        q_s = _partial_rope(_rms_norm(_heads(sq, SWA_Q_HEADS), swa_q_norm_g[l]), cos, sin)
        k_s = _partial_rope(_rms_norm(_heads(sk, SWA_KV_HEADS), swa_k_norm_g[l]), cos, sin)
        s_out = swa_sink_attention(q_s, k_s, _heads(sv, SWA_KV_HEADS), swa_sinks[l]).astype(x.dtype)

        x = x + jnp.concatenate([m_out, h_out, s_out], axis=-1) @ w_out[l]

        u = _rms_norm(x, mlp_norm_g[l]) @ w_up[l]
        x = x + jnp.square(jax.nn.relu(u)) @ w_down[l]

        gate = jax.nn.sigmoid((_rms_norm(x, ple_norm_g[l]) @ w_ple_gate[l]).astype(jnp.float32))
        e = _rms_norm(p[l] @ w_ple_proj[l], ple_post_norm_g[l]).astype(jnp.float32)
        x = x + (gate * e).astype(x.dtype)
    return x
```

```python
import functools

import numpy as np
import jax
import jax.numpy as jnp
from jax import lax
from jax.experimental import pallas as pl
from jax.experimental.pallas import tpu as pltpu

F32 = jnp.float32
BF16 = jnp.bfloat16

D_MODEL = 1024
N_HEADS = 4
HEAD_DIM = 64
GROUP_W = N_HEADS * HEAD_DIM
CONV_K = 4
SWA_Q_HEADS = 8
SWA_KV_HEADS = 2
SWA_GROUP = SWA_Q_HEADS // SWA_KV_HEADS
WINDOW = 128
ROPE_THETA = 500000.0
ROT_DIM = HEAD_DIM // 4
ROT_HALF = ROT_DIM // 2
D_FF = 4 * D_MODEL
PLE_DIM = 256
EPS = 1e-6

LANES = 128
SUBLANES = 8
VMEM_LIMIT_BYTES = 56 * 1024 * 1024

GATE_COLS = 2 * N_HEADS
M_COLS = 4 * GROUP_W + LANES
H_COLS = 4 * GROUP_W
S_COLS = SWA_Q_HEADS * HEAD_DIM + 2 * SWA_KV_HEADS * HEAD_DIM
IN_COLS = M_COLS + H_COLS + S_COLS
GATE_OFF = 4 * GROUP_W

IN_TM = 512
TAIL_TM = 512
ROPE_TM = 2048
MLSTM_T = 256
HGRN_T = 256
HGRN_L = 64
SWA_T = 512

N_SPLIT = 3


def _params(n_grid_dims):
    return pltpu.CompilerParams(
        dimension_semantics=("arbitrary",) * n_grid_dims,
        vmem_limit_bytes=VMEM_LIMIT_BYTES)


def _dot(a, b):
    return jnp.dot(a, b, preferred_element_type=F32)


def _dot_nt(a, b):
    return lax.dot_general(a, b, (((1,), (1,)), ((), ())), preferred_element_type=F32)


def _dot_tn(a, b):
    return lax.dot_general(a, b, (((0,), (0,)), ((), ())), preferred_element_type=F32)


def _split(x, n=N_SPLIT):
    pieces = []
    r = x
    for _ in range(n):
        p = r.astype(BF16)
        pieces.append(p)
        r = r - p.astype(F32)
    return pieces


def _dot_const_left(m, x):
    acc = None
    for p in _split(x):
        t = _dot(m, p)
        acc = t if acc is None else acc + t
    return acc


def _dot_const_right(x, m):
    acc = None
    for p in _split(x):
        t = _dot(p, m)
        acc = t if acc is None else acc + t
    return acc


def _sigmoid(x):
    return 1.0 / (1.0 + jnp.exp(-x))


def _log_sigmoid(x):
    return jnp.minimum(x, 0.0) - jnp.log1p(jnp.exp(-jnp.abs(x)))


def _rms(x, g):
    return x * lax.rsqrt(jnp.mean(x * x, axis=-1, keepdims=True) + EPS) * g


def _head_rms(x, g, blk_ones):
    ms = _dot_const_right(x * x, blk_ones) * (1.0 / HEAD_DIM)
    return x * lax.rsqrt(ms + EPS) * g


def _lane_head_id(shape):
    return lax.broadcasted_iota(jnp.int32, shape, len(shape) - 1) // HEAD_DIM


def _blk_ones(width):
    hid = np.arange(width) // HEAD_DIM
    return jnp.asarray((hid[:, None] == hid[None, :]).astype(np.float32), dtype=BF16)


def _hgrn_level_sizes():
    sizes = []
    c = 2
    while c <= HGRN_L:
        sizes.append(c)
        c *= 2
    return sizes


def _hgrn_constants():
    L = HGRN_L
    t = np.arange(L)[:, None]
    u = np.arange(L)[None, :]
    mats = [(u <= t), (u > t)]
    masks = [(t == u)]
    for c in _hgrn_level_sizes():
        mid = (t // c) * c + c // 2 - 1
        second = t > mid
        mats.append(np.where(second, (u > mid) & (u <= t), (u > t) & (u <= mid)))
        same_blk = (t // c) == (u // c)
        u_mid = (u // c) * c + c // 2 - 1
        masks.append(same_blk & second & (u <= u_mid))
    mstack = np.concatenate([m.astype(np.float32) for m in mats], axis=0)
    pmask = np.stack([np.tile(m.astype(np.float32), (N_HEADS, 1)) for m in masks])
    return jnp.asarray(mstack, dtype=BF16), jnp.asarray(pmask, dtype=F32)


def _rope_kernel(pos_ref, pat_ref, cos_ref, sin_ref):
    ang = pos_ref[...].astype(F32) * pat_ref[0:1, :]
    cos_ref[...] = jnp.cos(ang)
    sin_ref[...] = jnp.sin(ang) * pat_ref[1:2, :]


def _rope_tables(positions):
    n = positions.size
    inv_freq = ROPE_THETA ** (-jnp.arange(0, ROT_DIM, 2, dtype=F32) / ROT_DIM)
    lane = np.arange(LANES) % HEAD_DIM
    freq = jnp.where(lane < ROT_DIM, inv_freq[lane % ROT_HALF], 0.0)
    sign = np.where(lane < ROT_HALF, -1.0, np.where(lane < ROT_DIM, 1.0, 0.0)).astype(np.float32)
    pat = jnp.zeros((SUBLANES, LANES), F32).at[0].set(freq).at[1].set(sign)
    tm = min(ROPE_TM, n)
    out = jax.ShapeDtypeStruct((n, LANES), F32)
    return pl.pallas_call(
        _rope_kernel,
        grid=(n // tm,),
        in_specs=[pl.BlockSpec((tm, 1), lambda i: (i, 0)),
                  pl.BlockSpec((SUBLANES, LANES), lambda i: (0, 0))],
        out_specs=[pl.BlockSpec((tm, LANES), lambda i: (i, 0))] * 2,
        out_shape=[out, out],
        compiler_params=_params(1),
        name="rope_tables",
    )(positions.reshape(n, 1), pat)


def _lb_kernel(lg_ref, la_ref, l1m_ref, oml_ref):
    z = lg_ref[...]
    e = jnp.exp(z - jnp.max(z, axis=0, keepdims=True))
    sm = e / jnp.sum(e, axis=0, keepdims=True)
    depth = z.shape[0]
    run = sm[0:1, :]
    first = run
    for l in range(depth):
        if l > 0:
            run = run + sm[l:l + 1, :]
        lb = run - first
        la_ref[l:l + 1, :] = jnp.log(lb)
        l1m_ref[l:l + 1, :] = jnp.log1p(-lb)
        oml_ref[l:l + 1, :] = 1.0 - lb


def _hgrn_lower_bounds(logits):
    out = jax.ShapeDtypeStruct(logits.shape, F32)
    return pl.pallas_call(
        _lb_kernel, out_shape=[out, out, out], name="hgrn_lower_bounds",
    )(logits.astype(F32))


def _in_proj_kernel(x_ref, g_ref, w_ref, b_ref, ym_ref, yh_ref, ys_ref):
    h = _rms(x_ref[...], g_ref[...]).astype(BF16)
    lo = 0
    for o_ref in (ym_ref, yh_ref, ys_ref):
        hi = lo + o_ref.shape[-1]
        o_ref[...] = _dot(h, w_ref[:, lo:hi]) + b_ref[:, lo:hi]
        lo = hi


def _in_proj(x, g, w, b, layer):
    n = x.shape[0]
    tm = min(IN_TM, n)
    row = lambda i: (i, 0)
    lay = lambda i: (layer, 0, 0)
    return pl.pallas_call(
        _in_proj_kernel,
        grid=(n // tm,),
        in_specs=[pl.BlockSpec((tm, D_MODEL), row),
                  pl.BlockSpec((None, 1, D_MODEL), lay),
                  pl.BlockSpec((None, D_MODEL, IN_COLS), lay),
                  pl.BlockSpec((None, 1, IN_COLS), lay)],
        out_specs=[pl.BlockSpec((tm, M_COLS), row),
                   pl.BlockSpec((tm, H_COLS), row),
                   pl.BlockSpec((tm, S_COLS), row)],
        out_shape=[jax.ShapeDtypeStruct((n, M_COLS), F32),
                   jax.ShapeDtypeStruct((n, H_COLS), F32),
                   jax.ShapeDtypeStruct((n, S_COLS), F32)],
        compiler_params=_params(1),
        name="in_proj",
    )(x, g, w, b)


def _mlstm_kernel(y_ref, cw_ref, cb_ref, fb_ref, g_ref, ones_ref, tril_ref, exp_ref, o_ref,
                  xbuf, c_st, n_st, m_st):
    T = MLSTM_T
    W = GROUP_W
    step = pl.program_id(1)

    @pl.when(step == 0)
    def _():
        xbuf[0:SUBLANES, :] = jnp.zeros((SUBLANES, 2 * W), F32)
        c_st[...] = jnp.zeros_like(c_st)
        n_st[...] = jnp.zeros_like(n_st)
        m_st[...] = jnp.zeros_like(m_st)

    xbuf[SUBLANES:SUBLANES + T, :] = y_ref[:, 0:2 * W]
    acc = cb_ref[...] + cw_ref[CONV_K - 1:CONV_K, :] * xbuf[SUBLANES:SUBLANES + T, :]
    for j in range(1, CONV_K):
        acc = acc + cw_ref[CONV_K - 1 - j:CONV_K - j, :] * xbuf[SUBLANES - j:SUBLANES - j + T, :]
    xbuf[0:SUBLANES, :] = xbuf[T:T + SUBLANES, :]
    qk = acc * _sigmoid(acc)
    q = qk[:, 0:W]
    k = qk[:, W:2 * W] * (HEAD_DIM ** -0.5)
    v = y_ref[:, 2 * W:3 * W]
    og = y_ref[:, 3 * W:4 * W]

    gates = y_ref[:, GATE_OFF:GATE_OFF + LANES]
    i_pre = gates
    lf = pltpu.roll(_log_sigmoid(gates + fb_ref[...]), LANES - N_HEADS, 1)
    b = _dot_const_left(tril_ref[...], lf)
    a = i_pre - b
    row = lax.broadcasted_iota(jnp.int32, (T, LANES), 0)
    cm = a
    sh = 1
    while sh < T:
        cm = jnp.maximum(cm, jnp.where(row >= sh, pltpu.roll(cm, sh, 0), -jnp.inf))
        sh *= 2
    m_prev = m_st[0:1, :]
    m_j = b + jnp.maximum(m_prev, cm)
    m_new = m_j[T - 1:T, :]
    b_last = b[T - 1:T, :]
    w_inter = jnp.exp(b + m_prev - m_j)
    e_den = jnp.exp(-m_j)
    w_s = jnp.exp(b_last + a - m_new)
    decay = jnp.exp(b_last + m_prev - m_new)
    cexp = b - m_j
    a_t = a.T

    stacked = jnp.concatenate(
        [w_inter, e_den, w_s, jnp.broadcast_to(decay, (SUBLANES, LANES))], axis=0)
    head_lane = lax.broadcasted_iota(jnp.int32, stacked.shape, 1) < N_HEADS
    ex = _dot_const_right(jnp.where(head_lane, stacked, 0.0), exp_ref[...])
    w_inter_x = ex[0:T]
    e_den_x = ex[T:2 * T]
    w_s_x = ex[2 * T:3 * T]
    decay_x = ex[3 * T:3 * T + 1]

    hid = _lane_head_id((T, W))
    causal = (lax.broadcasted_iota(jnp.int32, (T, T), 1)
              <= lax.broadcasted_iota(jnp.int32, (T, T), 0))
    kb = k.astype(BF16)
    vb = v.astype(BF16)
    num = jnp.zeros((T, W), F32)
    den = jnp.zeros((T, W), F32)
    for h in range(N_HEADS):
        sel = hid == h
        qm = jnp.where(sel, q, 0.0).astype(BF16)
        dm = cexp[:, h:h + 1] + a_t[h:h + 1, :]
        w_intra = jnp.exp(jnp.where(causal, dm, -jnp.inf))
        s = (_dot_nt(qm, kb) * w_intra).astype(BF16)
        num = num + _dot(s, jnp.where(sel, v, 0.0).astype(BF16))
        den = den + _dot(s, jnp.where(sel, 1.0, 0.0).astype(BF16))
    qb = q.astype(BF16)
    num = num + w_inter_x * _dot(qb, c_st[...].astype(BF16))
    den = den + w_inter_x * _dot(qb, n_st[...].astype(BF16))
    hh = num / jnp.maximum(jnp.abs(den), e_den_x)

    blk = ones_ref[...]
    o_ref[...] = _head_rms(hh, g_ref[...], blk) * _sigmoid(og)

    kw = (k * w_s_x).astype(BF16)
    blk_f = blk.astype(F32)
    c_st[...] = decay_x * c_st[...] + blk_f * _dot_tn(kw, vb)
    n_st[...] = decay_x * n_st[...] + blk_f * _dot_tn(kw, jnp.ones((T, W), BF16))
    m_st[...] = jnp.broadcast_to(m_new, m_st.shape)


def _mlstm(ym, conv_w, conv_b, f_bias, norm_g, layer, batch, consts):
    n = ym.shape[0]
    seq = n // batch
    T = MLSTM_T
    steps = seq // T
    row = lambda bi, ci: (bi * steps + ci, 0)
    lay = lambda bi, ci: (layer, 0, 0)
    cst = lambda bi, ci: (0, 0)
    blk_ones, tril, expand = consts
    return pl.pallas_call(
        _mlstm_kernel,
        grid=(batch, steps),
        in_specs=[pl.BlockSpec((T, M_COLS), row),
                  pl.BlockSpec((None, CONV_K, 2 * GROUP_W), lay),
                  pl.BlockSpec((None, 1, 2 * GROUP_W), lay),
                  pl.BlockSpec((None, 1, LANES), lay),
                  pl.BlockSpec((None, 1, GROUP_W), lay),
                  pl.BlockSpec((GROUP_W, GROUP_W), cst),
                  pl.BlockSpec((T, T), cst),
                  pl.BlockSpec((LANES, GROUP_W), cst)],
        out_specs=pl.BlockSpec((T, GROUP_W), row),
        out_shape=jax.ShapeDtypeStruct((n, GROUP_W), F32),
        scratch_shapes=[pltpu.VMEM((T + SUBLANES, 2 * GROUP_W), F32),
                        pltpu.VMEM((GROUP_W, GROUP_W), F32),
                        pltpu.VMEM((GROUP_W, GROUP_W), F32),
                        pltpu.VMEM((SUBLANES, LANES), F32)],
        compiler_params=_params(2),
        name="mlstm",
    )(ym, conv_w, conv_b, f_bias, norm_g, blk_ones, tril, expand)


def _hgrn_kernel(y_ref, la_ref, l1m_ref, oml_ref, g_ref, ones_ref, mst_ref, pm_ref, o_ref, s_st):
    L = HGRN_L
    W = GROUP_W
    step = pl.program_id(1)

    @pl.when(step == 0)
    def _():
        s_st[...] = jnp.zeros_like(s_st)

    la = la_ref[...]
    l1m = l1m_ref[...]
    oml = oml_ref[...]
    gain = g_ref[...]
    blk = ones_ref[...]
    blk_f = blk.astype(F32)
    mstack = mst_ref[...]
    n_lvl = pm_ref.shape[0]
    hid = _lane_head_id((L, W))

    def head_stack(x):
        return jnp.concatenate(
            [jnp.where(hid == h, x, 0.0).astype(BF16) for h in range(N_HEADS)], axis=0)

    def chunk(ci, carry):
        r0 = pl.multiple_of(ci * L, L)
        hq = y_ref[pl.ds(r0, L), 0:W]
        hf = y_ref[pl.ds(r0, L), W:2 * W]
        val = y_ref[pl.ds(r0, L), 2 * W:3 * W]
        hg = y_ref[pl.ds(r0, L), 3 * W:4 * W]

        u = jnp.exp(-jnp.abs(hf))
        one_p_u = 1.0 + u
        y = l1m + (jnp.minimum(hf, 0.0) - jnp.log1p(u))
        lf = jnp.maximum(la, y) + jnp.log1p(jnp.exp(-jnp.abs(la - y)))
        key = oml * jnp.where(hf >= 0.0, u, 1.0) / one_p_u
        qf = hq * _sigmoid(hq)

        e_all = jnp.exp(_dot_const_left(mstack, lf))
        e_g = e_all[0:L]
        q_hat = (qf * e_g).astype(BF16)
        k_hat = (key * e_all[L:2 * L]).astype(BF16)
        vb = val.astype(BF16)

        a = pm_ref[0] * _dot_nt(head_stack(qf), key.astype(BF16))
        for i in range(1, n_lvl):
            e_i = e_all[(1 + i) * L:(2 + i) * L]
            xk = (key * e_i).astype(BF16)
            a = a + pm_ref[i] * _dot_nt(head_stack(qf * e_i), xk)
        o_blk = _dot(a.astype(BF16), vb)
        o = _dot_nt(q_hat, s_st[...].astype(BF16))
        for h in range(N_HEADS):
            o = o + jnp.where(hid == h, o_blk[h * L:(h + 1) * L], 0.0)

        o_ref[pl.ds(r0, L), :] = _head_rms(o, gain, blk) * (hg * _sigmoid(hg))
        s_st[...] = s_st[...] * e_g[L - 1:L, :] + blk_f * _dot_tn(vb, k_hat)
        return carry

    lax.fori_loop(0, HGRN_T // L, chunk, 0)


def _hgrn(yh, la, l1m, oml, norm_g, layer, batch, consts):
    n = yh.shape[0]
    seq = n // batch
    T = HGRN_T
    steps = seq // T
    row = lambda bi, ci: (bi * steps + ci, 0)
    lay = lambda bi, ci: (layer, 0, 0)
    blk_ones, mstack, pmask = consts
    return pl.pallas_call(
        _hgrn_kernel,
        grid=(batch, steps),
        in_specs=[pl.BlockSpec((T, H_COLS), row),
                  pl.BlockSpec((None, 1, GROUP_W), lay),
                  pl.BlockSpec((None, 1, GROUP_W), lay),
                  pl.BlockSpec((None, 1, GROUP_W), lay),
                  pl.BlockSpec((None, 1, GROUP_W), lay),
                  pl.BlockSpec((GROUP_W, GROUP_W), lambda bi, ci: (0, 0)),
                  pl.BlockSpec(mstack.shape, lambda bi, ci: (0, 0)),
                  pl.BlockSpec(pmask.shape, lambda bi, ci: (0, 0, 0))],
        out_specs=pl.BlockSpec((T, GROUP_W), row),
        out_shape=jax.ShapeDtypeStruct((n, GROUP_W), F32),
        scratch_shapes=[pltpu.VMEM((GROUP_W, GROUP_W), F32)],
        compiler_params=_params(2),
        name="hgrn2",
    )(yh, la, l1m, oml, norm_g, blk_ones, mstack, pmask)


def _swa_kernel(sink_ref, y_ref, cos_ref, sin_ref, qg_ref, kg_ref, ones_ref, o_ref, k_prev, v_prev,
                *, layer):
    T = SWA_T
    Wn = WINDOW
    QW = SWA_Q_HEADS * HEAD_DIM
    step = pl.program_id(1)

    @pl.when(step == 0)
    def _():
        k_prev[...] = jnp.zeros_like(k_prev)
        v_prev[...] = jnp.zeros_like(v_prev)

    cos_t = cos_ref[...]
    sin_t = sin_ref[...]
    blk = ones_ref[...]
    lane = lax.broadcasted_iota(jnp.int32, (T, LANES), 1)
    low_half = lane < HEAD_DIM
    rot_first = (lane % HEAD_DIM) < ROT_HALF

    def norm_rope(x, gain):
        xn = _head_rms(x, gain, blk)
        partner = jnp.where(rot_first, pltpu.roll(xn, LANES - ROT_HALF, 1), pltpu.roll(xn, ROT_HALF, 1))
        return xn * cos_t + partner * sin_t

    k = norm_rope(y_ref[:, QW:QW + LANES], kg_ref[...])
    v = y_ref[:, QW + LANES:QW + 2 * LANES]
    k_sw = pltpu.roll(k, HEAD_DIM, 1)
    v_sw = pltpu.roll(v, HEAD_DIM, 1)
    k2 = [jnp.concatenate([k_prev[g], jnp.where(low_half == (g == 0), k, k_sw).astype(BF16)], axis=0)
          for g in range(SWA_KV_HEADS)]
    v2 = [jnp.concatenate([v_prev[g], jnp.where(low_half == (g == 0), v, v_sw).astype(BF16)], axis=0)
          for g in range(SWA_KV_HEADS)]
    for g in range(SWA_KV_HEADS):
        k_prev[g] = k2[g][T:T + Wn]
        v_prev[g] = v2[g][T:T + Wn]

    q_grp = [(norm_rope(y_ref[:, c * LANES:(c + 1) * LANES], qg_ref[...]) * (HEAD_DIM ** -0.5))
             for c in range(QW // LANES)]

    qi = lax.broadcasted_iota(jnp.int32, (Wn, 2 * Wn), 0)
    ki = lax.broadcasted_iota(jnp.int32, (Wn, 2 * Wn), 1)
    band = (ki > qi) & (ki <= qi + Wn)
    band_first = band & (ki >= jnp.where(step > 0, 0, Wn))
    lane_w = lax.broadcasted_iota(jnp.int32, (Wn, LANES), 1) < HEAD_DIM

    for j in range(T // Wn):
        r0 = j * Wn
        mask = band_first if j == 0 else band
        for g in range(SWA_KV_HEADS):
            kc = k2[g][r0:r0 + 2 * Wn]
            vc = v2[g][r0:r0 + 2 * Wn]
            parts = []
            for c in (2 * g, 2 * g + 1):
                qc = q_grp[c][r0:r0 + Wn]
                parts.append(jnp.where(lane_w, qc, 0.0).astype(BF16))
                parts.append(jnp.where(lane_w, 0.0, qc).astype(BF16))
            s = _dot_nt(jnp.concatenate(parts, axis=0), kc)
            outs = []
            for r in range(SWA_GROUP):
                sink = sink_ref[layer, SWA_GROUP * g + r]
                sr = jnp.where(mask, s[r * Wn:(r + 1) * Wn], -jnp.inf)
                m = jnp.maximum(jnp.max(sr, axis=-1, keepdims=True), sink)
                pexp = jnp.exp(sr - m)
                den = jnp.sum(pexp, axis=-1, keepdims=True) + jnp.exp(sink - m)
                outs.append(_dot(pexp.astype(BF16), vc) / den)
            c0 = 2 * g * LANES
            o_ref[r0:r0 + Wn, c0:c0 + LANES] = jnp.where(lane_w, outs[0], outs[1])
            o_ref[r0:r0 + Wn, c0 + LANES:c0 + 2 * LANES] = jnp.where(lane_w, outs[2], outs[3])


def _swa(ys, cos_t, sin_t, q_gain, k_gain, sinks, layer, batch, blk_ones):
    n = ys.shape[0]
    seq = n // batch
    T = SWA_T
    steps = seq // T
    QW = SWA_Q_HEADS * HEAD_DIM
    row = lambda bi, ci: (bi * steps + ci, 0)
    lay = lambda bi, ci: (layer, 0, 0)
    return pl.pallas_call(
        functools.partial(_swa_kernel, layer=layer),
        grid=(batch, steps),
        in_specs=[pl.BlockSpec(memory_space=pltpu.SMEM),
                  pl.BlockSpec((T, S_COLS), row),
                  pl.BlockSpec((T, LANES), row),
                  pl.BlockSpec((T, LANES), row),
                  pl.BlockSpec((None, 1, LANES), lay),
                  pl.BlockSpec((None, 1, LANES), lay),
                  pl.BlockSpec((LANES, LANES), lambda bi, ci: (0, 0))],
        out_specs=pl.BlockSpec((T, QW), row),
        out_shape=jax.ShapeDtypeStruct((n, QW), F32),
        scratch_shapes=[pltpu.VMEM((SWA_KV_HEADS, WINDOW, LANES), BF16),
                        pltpu.VMEM((SWA_KV_HEADS, WINDOW, LANES), BF16)],
        compiler_params=_params(2),
        name="swa",
    )(sinks, ys, cos_t, sin_t, q_gain, k_gain, blk_ones)


def _tail_kernel(x_ref, mm_ref, mh_ref, ms_ref, p_ref, wo_ref, g2_ref, wu_ref, wd_ref, g3_ref,
                 wg_ref, wp_ref, g4_ref, o_ref):
    W = GROUP_W
    mix = _dot(mm_ref[...].astype(BF16), wo_ref[0:W, :])
    mix = mix + _dot(mh_ref[...].astype(BF16), wo_ref[W:2 * W, :])
    mix = mix + _dot(ms_ref[...].astype(BF16), wo_ref[2 * W:4 * W, :])
    x = x_ref[...] + mix

    h2 = _rms(x, g2_ref[...]).astype(BF16)
    mlp = None
    for c in range(D_FF // D_MODEL):
        u = _dot(h2, wu_ref[:, c * D_MODEL:(c + 1) * D_MODEL])
        act = jnp.square(jnp.maximum(u, 0.0)).astype(BF16)
        d = _dot(act, wd_ref[c * D_MODEL:(c + 1) * D_MODEL, :])
        mlp = d if mlp is None else mlp + d
    x = x + mlp

    gate = _sigmoid(_dot(_rms(x, g3_ref[...]).astype(BF16), wg_ref[...]))
    emb = _rms(_dot(p_ref[...].astype(BF16), wp_ref[...]), g4_ref[...])
    o_ref[...] = x + gate * emb


def _tail(x, mm, mh, ms, p, w_out, g2, w_up, w_down, g3, w_gate, w_proj, g4, layer):
    n = x.shape[0]
    tm = min(TAIL_TM, n)
    tiles = n // tm
    row = lambda i: (i, 0)
    lay = lambda i: (layer, 0, 0)
    once = pl.Buffered(1)

    def wspec(shape):
        return pl.BlockSpec((None,) + shape, lay, pipeline_mode=once)

    return pl.pallas_call(
        _tail_kernel,
        grid=(tiles,),
        in_specs=[pl.BlockSpec((tm, D_MODEL), row),
                  pl.BlockSpec((tm, GROUP_W), row),
                  pl.BlockSpec((tm, GROUP_W), row),
                  pl.BlockSpec((tm, 2 * GROUP_W), row),
                  pl.BlockSpec((None, tm, PLE_DIM), lambda i: (layer, i, 0)),
                  wspec((D_MODEL, D_MODEL)),
                  wspec((1, D_MODEL)),
                  wspec((D_MODEL, D_FF)),
                  wspec((D_FF, D_MODEL)),
                  wspec((1, D_MODEL)),
                  wspec((D_MODEL, D_MODEL)),
                  wspec((PLE_DIM, D_MODEL)),
                  wspec((1, D_MODEL))],
        out_specs=pl.BlockSpec((tm, D_MODEL), row),
        out_shape=jax.ShapeDtypeStruct((n, D_MODEL), F32),
        compiler_params=_params(1),
        name="tail",
    )(x, mm, mh, ms, p, w_out, g2, w_up, w_down, g3, w_gate, w_proj, g4)


def kernel(x, p, positions, in_norm_g, w_in, b_in, mlstm_f_bias, mlstm_conv_w, mlstm_conv_b,
           mlstm_norm_g, hgrn_lb_logits, hgrn_norm_g, swa_q_norm_g, swa_k_norm_g, swa_sinks,
           w_out, mlp_norm_g, w_up, w_down, ple_norm_g, w_ple_gate, w_ple_proj, ple_post_norm_g):
    batch, seq, d_model = x.shape
    depth = w_in.shape[0]
    n = batch * seq
    assert d_model == D_MODEL and seq % max(MLSTM_T, HGRN_T, SWA_T) == 0 and n % IN_TM == 0

    n_raw = GATE_OFF + GATE_COLS
    pad = LANES - GATE_COLS
    w_in_p = jnp.concatenate(
        [w_in[..., :n_raw], jnp.zeros((depth, D_MODEL, pad), w_in.dtype), w_in[..., n_raw:]],
        axis=-1).astype(BF16)
    b_in_p = jnp.concatenate(
        [b_in[..., :n_raw], jnp.zeros((depth, pad), b_in.dtype), b_in[..., n_raw:]],
        axis=-1).astype(F32).reshape(depth, 1, IN_COLS)
    f_bias = jnp.zeros((depth, 1, LANES), F32).at[:, 0, N_HEADS:GATE_COLS].set(mlstm_f_bias.astype(F32))
    row3 = lambda a: a.astype(F32).reshape(depth, 1, a.shape[-1])
    q_gain = row3(jnp.tile(swa_q_norm_g, (1, LANES // HEAD_DIM)))
    k_gain = row3(jnp.tile(swa_k_norm_g, (1, LANES // HEAD_DIM)))
    w_out_b, w_up_b, w_down_b = w_out.astype(BF16), w_up.astype(BF16), w_down.astype(BF16)
    w_gate_b, w_proj_b = w_ple_gate.astype(BF16), w_ple_proj.astype(BF16)

    blk256 = _blk_ones(GROUP_W)
    blk128 = _blk_ones(LANES)
    tril = jnp.asarray(np.tril(np.ones((MLSTM_T, MLSTM_T), np.float32)), dtype=BF16)
    expand = jnp.asarray(
        (np.arange(LANES)[:, None] == (np.arange(GROUP_W) // HEAD_DIM)[None, :]).astype(np.float32),
        dtype=BF16)
    hgrn_mstack, hgrn_pmask = _hgrn_constants()

    cos_t, sin_t = _rope_tables(positions)
    la, l1m, oml = [row3(a) for a in _hgrn_lower_bounds(hgrn_lb_logits)]
    sinks = swa_sinks.astype(F32)

    xf = x.reshape(n, D_MODEL)
    for l in range(depth):
        ym, yh, ys = _in_proj(xf, row3(in_norm_g), w_in_p, b_in_p, l)
        mm = _mlstm(ym, mlstm_conv_w.astype(F32), row3(mlstm_conv_b), f_bias, row3(mlstm_norm_g),
                    l, batch, (blk256, tril, expand))
        mh = _hgrn(yh, la, l1m, oml, row3(hgrn_norm_g), l, batch, (blk256, hgrn_mstack, hgrn_pmask))
        ms = _swa(ys, cos_t, sin_t, q_gain, k_gain, sinks, l, batch, blk128)
        xf = _tail(xf, mm, mh, ms, p.reshape(depth, n, PLE_DIM), w_out_b, row3(mlp_norm_g), w_up_b,
                   w_down_b, row3(ple_norm_g), w_gate_b, w_proj_b, row3(ple_post_norm_g), l)
    return xf.reshape(batch, seq, D_MODEL)
```

```python
import functools

import numpy as np
import jax
import jax.numpy as jnp
from jax import lax
from jax.experimental import pallas as pl
from jax.experimental.pallas import tpu as pltpu

F32 = jnp.float32
BF16 = jnp.bfloat16

D_MODEL = 1024
N_HEADS = 4
HEAD_DIM = 64
GROUP_W = N_HEADS * HEAD_DIM
CONV_K = 4
SWA_Q_HEADS = 8
SWA_KV_HEADS = 2
SWA_GROUP = SWA_Q_HEADS // SWA_KV_HEADS
WINDOW = 128
ROPE_THETA = 500000.0
ROT_DIM = HEAD_DIM // 4
ROT_HALF = ROT_DIM // 2
D_FF = 4 * D_MODEL
PLE_DIM = 256
EPS = 1e-6

LANES = 128
SUBLANES = 8
VMEM_LIMIT_BYTES = 56 * 1024 * 1024

GATE_COLS = 2 * N_HEADS
M_COLS = 4 * GROUP_W + LANES
H_COLS = 4 * GROUP_W
S_COLS = SWA_Q_HEADS * HEAD_DIM + 2 * SWA_KV_HEADS * HEAD_DIM
IN_COLS = M_COLS + H_COLS + S_COLS
GATE_OFF = 4 * GROUP_W

IN_TM = 512
TAIL_TM = 512
ROPE_TM = 2048
MLSTM_T = 256
HGRN_T = 256
HGRN_L = 64
SWA_T = 512

N_SPLIT = 2
assert HGRN_L == HEAD_DIM


def _params(n_grid_dims):
    return pltpu.CompilerParams(
        dimension_semantics=("arbitrary",) * n_grid_dims,
        vmem_limit_bytes=VMEM_LIMIT_BYTES)


def _dot(a, b):
    return jnp.dot(a, b, preferred_element_type=F32)


def _dot_nt(a, b):
    return lax.dot_general(a, b, (((1,), (1,)), ((), ())), preferred_element_type=F32)


def _dot_tn(a, b):
    return lax.dot_general(a, b, (((0,), (0,)), ((), ())), preferred_element_type=F32)


def _split(x, n=N_SPLIT):
    pieces = []
    r = x
    for _ in range(n):
        p = r.astype(BF16)
        pieces.append(p)
        r = r - p.astype(F32)
    return pieces


def _dot_const_left(m, x):
    acc = None
    for p in _split(x):
        t = _dot(m, p)
        acc = t if acc is None else acc + t
    return acc


def _dot_const_right(x, m):
    acc = None
    for p in _split(x):
        t = _dot(p, m)
        acc = t if acc is None else acc + t
    return acc


def _sigmoid(x):
    return 1.0 / (1.0 + jnp.exp(-x))


def _log_sigmoid(x):
    return jnp.minimum(x, 0.0) - jnp.log1p(jnp.exp(-jnp.abs(x)))


def _rms(x, g):
    return x * lax.rsqrt(jnp.mean(x * x, axis=-1, keepdims=True) + EPS) * g


def _head_rms(x, g, blk_ones):
    ms = _dot_const_right(x * x, blk_ones) * (1.0 / HEAD_DIM)
    return x * lax.rsqrt(ms + EPS) * g


def _lane_head_id(shape):
    return lax.broadcasted_iota(jnp.int32, shape, len(shape) - 1) // HEAD_DIM


def _blk_ones(width):
    hid = np.arange(width) // HEAD_DIM
    return jnp.asarray((hid[:, None] == hid[None, :]).astype(np.float32), dtype=BF16)


def _hgrn_level_sizes():
    sizes = []
    c = 2
    while c <= HGRN_L:
        sizes.append(c)
        c *= 2
    return sizes


def _hgrn_constants():
    L = HGRN_L
    t = np.arange(L)[:, None]
    u = np.arange(L)[None, :]
    mats = [(u <= t), (u > t)]
    masks = [(t == u)]
    for c in _hgrn_level_sizes():
        mid = (t // c) * c + c // 2 - 1
        second = t > mid
        mats.append(np.where(second, (u > mid) & (u <= t), (u > t) & (u <= mid)))
        same_blk = (t // c) == (u // c)
        u_mid = (u // c) * c + c // 2 - 1
        masks.append(same_blk & second & (u <= u_mid))
    mstack = np.concatenate([m.astype(np.float32) for m in mats], axis=0)
    pmask = np.stack([np.tile(m.astype(np.float32), (1, N_HEADS)) for m in masks])
    return jnp.asarray(mstack, dtype=BF16), jnp.asarray(pmask, dtype=F32)


def _rope_kernel(pos_ref, pat_ref, cos_ref, sin_ref):
    ang = pos_ref[...].astype(F32) * pat_ref[0:1, :]
    cos_ref[...] = jnp.cos(ang)
    sin_ref[...] = jnp.sin(ang) * pat_ref[1:2, :]


def _rope_tables(positions):
    n = positions.size
    inv_freq = ROPE_THETA ** (-jnp.arange(0, ROT_DIM, 2, dtype=F32) / ROT_DIM)
    lane = np.arange(LANES) % HEAD_DIM
    freq = jnp.where(lane < ROT_DIM, inv_freq[lane % ROT_HALF], 0.0)
    sign = np.where(lane < ROT_HALF, -1.0, np.where(lane < ROT_DIM, 1.0, 0.0)).astype(np.float32)
    pat = jnp.zeros((SUBLANES, LANES), F32).at[0].set(freq).at[1].set(sign)
    tm = min(ROPE_TM, n)
    out = jax.ShapeDtypeStruct((n, LANES), F32)
    return pl.pallas_call(
        _rope_kernel,
        grid=(n // tm,),
        in_specs=[pl.BlockSpec((tm, 1), lambda i: (i, 0)),
                  pl.BlockSpec((SUBLANES, LANES), lambda i: (0, 0))],
        out_specs=[pl.BlockSpec((tm, LANES), lambda i: (i, 0))] * 2,
        out_shape=[out, out],
        compiler_params=_params(1),
        name="rope_tables",
    )(positions.reshape(n, 1), pat)


def _lb_kernel(lg_ref, la_ref, l1m_ref, oml_ref):
    z = lg_ref[...]
    e = jnp.exp(z - jnp.max(z, axis=0, keepdims=True))
    sm = e / jnp.sum(e, axis=0, keepdims=True)
    depth = z.shape[0]
    run = sm[0:1, :]
    first = run
    for l in range(depth):
        if l > 0:
            run = run + sm[l:l + 1, :]
        lb = run - first
        la_ref[l:l + 1, :] = jnp.log(lb)
        l1m_ref[l:l + 1, :] = jnp.log1p(-lb)
        oml_ref[l:l + 1, :] = 1.0 - lb


def _hgrn_lower_bounds(logits):
    out = jax.ShapeDtypeStruct(logits.shape, F32)
    return pl.pallas_call(
        _lb_kernel, out_shape=[out, out, out], name="hgrn_lower_bounds",
    )(logits.astype(F32))


def _in_proj_kernel(x_ref, g_ref, w_ref, b_ref, ym_ref, yh_ref, ys_ref):
    h = _rms(x_ref[...], g_ref[...]).astype(BF16)
    lo = 0
    for o_ref in (ym_ref, yh_ref, ys_ref):
        hi = lo + o_ref.shape[-1]
        o_ref[...] = _dot(h, w_ref[:, lo:hi]) + b_ref[:, lo:hi]
        lo = hi


def _in_proj(x, g, w, b, layer):
    n = x.shape[0]
    tm = min(IN_TM, n)
    row = lambda i: (i, 0)
    lay = lambda i: (layer, 0, 0)
    return pl.pallas_call(
        _in_proj_kernel,
        grid=(n // tm,),
        in_specs=[pl.BlockSpec((tm, D_MODEL), row),
                  pl.BlockSpec((None, 1, D_MODEL), lay),
                  pl.BlockSpec((None, D_MODEL, IN_COLS), lay),
                  pl.BlockSpec((None, 1, IN_COLS), lay)],
        out_specs=[pl.BlockSpec((tm, M_COLS), row),
                   pl.BlockSpec((tm, H_COLS), row),
                   pl.BlockSpec((tm, S_COLS), row)],
        out_shape=[jax.ShapeDtypeStruct((n, M_COLS), F32),
                   jax.ShapeDtypeStruct((n, H_COLS), F32),
                   jax.ShapeDtypeStruct((n, S_COLS), F32)],
        compiler_params=_params(1),
        name="in_proj",
    )(x, g, w, b)


def _mlstm_kernel(y_ref, cw_ref, cb_ref, fb_ref, g_ref, ones_ref, tril_ref, exp_ref, o_ref,
                  xbuf, c_st, n_st, m_st):
    T = MLSTM_T
    W = GROUP_W
    step = pl.program_id(1)

    @pl.when(step == 0)
    def _():
        xbuf[0:SUBLANES, :] = jnp.zeros((SUBLANES, 2 * W), F32)
        c_st[...] = jnp.zeros_like(c_st)
        n_st[...] = jnp.zeros_like(n_st)
        m_st[...] = jnp.zeros_like(m_st)

    xbuf[SUBLANES:SUBLANES + T, :] = y_ref[:, 0:2 * W]
    acc = cb_ref[...] + cw_ref[CONV_K - 1:CONV_K, :] * xbuf[SUBLANES:SUBLANES + T, :]
    for j in range(1, CONV_K):
        acc = acc + cw_ref[CONV_K - 1 - j:CONV_K - j, :] * xbuf[SUBLANES - j:SUBLANES - j + T, :]
    xbuf[0:SUBLANES, :] = xbuf[T:T + SUBLANES, :]
    qk = acc * _sigmoid(acc)
    q = qk[:, 0:W]
    k = qk[:, W:2 * W] * (HEAD_DIM ** -0.5)
    v = y_ref[:, 2 * W:3 * W]
    og = y_ref[:, 3 * W:4 * W]

    gates = y_ref[:, GATE_OFF:GATE_OFF + LANES]
    i_pre = gates
    lf = pltpu.roll(_log_sigmoid(gates + fb_ref[...]), LANES - N_HEADS, 1)
    b = _dot_const_left(tril_ref[...], lf)
    a = i_pre - b
    row = lax.broadcasted_iota(jnp.int32, (T, LANES), 0)
    cm = a
    sh = 1
    while sh < T:
        cm = jnp.maximum(cm, jnp.where(row >= sh, pltpu.roll(cm, sh, 0), -jnp.inf))
        sh *= 2
    m_prev = m_st[0:1, :]
    m_j = b + jnp.maximum(m_prev, cm)
    m_new = m_j[T - 1:T, :]
    b_last = b[T - 1:T, :]
    w_inter = jnp.exp(b + m_prev - m_j)
    e_den = jnp.exp(-m_j)
    w_s = jnp.exp(b_last + a - m_new)
    decay = jnp.exp(b_last + m_prev - m_new)
    cexp = b - m_j
    a_t = a.T

    stacked = jnp.concatenate(
        [w_inter, e_den, w_s, jnp.broadcast_to(decay, (SUBLANES, LANES))], axis=0)
    head_lane = lax.broadcasted_iota(jnp.int32, stacked.shape, 1) < N_HEADS
    ex = _dot_const_right(jnp.where(head_lane, stacked, 0.0), exp_ref[...])
    w_inter_x = ex[0:T]
    e_den_x = ex[T:2 * T]
    w_s_x = ex[2 * T:3 * T]
    decay_x = ex[3 * T:3 * T + 1]

    hid = _lane_head_id((T, W))
    causal = (lax.broadcasted_iota(jnp.int32, (T, T), 1)
              <= lax.broadcasted_iota(jnp.int32, (T, T), 0))
    kb = k.astype(BF16)
    vb = v.astype(BF16)
    num = jnp.zeros((T, W), F32)
    den = jnp.zeros((T, W), F32)
    for h in range(N_HEADS):
        sel = hid == h
        qm = jnp.where(sel, q, 0.0).astype(BF16)
        dm = cexp[:, h:h + 1] + a_t[h:h + 1, :]
        w_intra = jnp.exp(jnp.where(causal, dm, -jnp.inf))
        s = (_dot_nt(qm, kb) * w_intra).astype(BF16)
        num = num + _dot(s, jnp.where(sel, v, 0.0).astype(BF16))
        den = den + _dot(s, jnp.where(sel, 1.0, 0.0).astype(BF16))
    qb = q.astype(BF16)
    num = num + w_inter_x * _dot(qb, c_st[...].astype(BF16))
    den = den + w_inter_x * _dot(qb, n_st[...].astype(BF16))
    hh = num / jnp.maximum(jnp.abs(den), e_den_x)

    blk = ones_ref[...]
    o_ref[...] = _head_rms(hh, g_ref[...], blk) * _sigmoid(og)

    kw = (k * w_s_x).astype(BF16)
    blk_f = blk.astype(F32)
    c_st[...] = decay_x * c_st[...] + blk_f * _dot_tn(kw, vb)
    n_st[...] = decay_x * n_st[...] + blk_f * _dot_tn(kw, jnp.ones((T, W), BF16))
    m_st[...] = jnp.broadcast_to(m_new, m_st.shape)


def _mlstm(ym, conv_w, conv_b, f_bias, norm_g, layer, batch, consts):
    n = ym.shape[0]
    seq = n // batch
    T = MLSTM_T
    steps = seq // T
    row = lambda bi, ci: (bi * steps + ci, 0)
    lay = lambda bi, ci: (layer, 0, 0)
    cst = lambda bi, ci: (0, 0)
    blk_ones, tril, expand = consts
    return pl.pallas_call(
        _mlstm_kernel,
        grid=(batch, steps),
        in_specs=[pl.BlockSpec((T, M_COLS), row),
                  pl.BlockSpec((None, CONV_K, 2 * GROUP_W), lay),
                  pl.BlockSpec((None, 1, 2 * GROUP_W), lay),
                  pl.BlockSpec((None, 1, LANES), lay),
                  pl.BlockSpec((None, 1, GROUP_W), lay),
                  pl.BlockSpec((GROUP_W, GROUP_W), cst),
                  pl.BlockSpec((T, T), cst),
                  pl.BlockSpec((LANES, GROUP_W), cst)],
        out_specs=pl.BlockSpec((T, GROUP_W), row),
        out_shape=jax.ShapeDtypeStruct((n, GROUP_W), F32),
        scratch_shapes=[pltpu.VMEM((T + SUBLANES, 2 * GROUP_W), F32),
                        pltpu.VMEM((GROUP_W, GROUP_W), F32),
                        pltpu.VMEM((GROUP_W, GROUP_W), F32),
                        pltpu.VMEM((SUBLANES, LANES), F32)],
        compiler_params=_params(2),
        name="mlstm",
    )(ym, conv_w, conv_b, f_bias, norm_g, blk_ones, tril, expand)


def _hgrn_kernel(y_ref, la_ref, l1m_ref, oml_ref, g_ref, ones_ref, mst_ref, pm_ref, o_ref, s_st):
    L = HGRN_L
    W = GROUP_W
    step = pl.program_id(1)

    @pl.when(step == 0)
    def _():
        s_st[...] = jnp.zeros_like(s_st)

    la = la_ref[...]
    l1m = l1m_ref[...]
    oml = oml_ref[...]
    gain = g_ref[...]
    blk = ones_ref[...]
    blk_f = blk.astype(F32)
    mstack = mst_ref[...]
    n_lvl = pm_ref.shape[0]

    def head_stack(xb):
        return jnp.concatenate([xb] * N_HEADS, axis=0) * blk

    for ci in range(HGRN_T // L):
        r0 = ci * L
        hq = y_ref[r0:r0 + L, 0:W]
        hf = y_ref[r0:r0 + L, W:2 * W]
        val = y_ref[r0:r0 + L, 2 * W:3 * W]
        hg = y_ref[r0:r0 + L, 3 * W:4 * W]

        u = jnp.exp(-jnp.abs(hf))
        one_p_u = 1.0 + u
        y = l1m + (jnp.minimum(hf, 0.0) - jnp.log1p(u))
        lf = jnp.maximum(la, y) + jnp.log1p(jnp.exp(-jnp.abs(la - y)))
        key = oml * jnp.where(hf >= 0.0, u, 1.0) / one_p_u
        qf = hq * _sigmoid(hq)

        e_all = jnp.exp(_dot_const_left(mstack, lf))
        e_g = e_all[0:L]
        q_hat = (qf * e_g).astype(BF16)
        k_hat = (key * e_all[L:2 * L]).astype(BF16)
        vb = val.astype(BF16)

        a = pm_ref[0] * _dot_nt(qf.astype(BF16), head_stack(key.astype(BF16)))
        for i in range(1, n_lvl):
            e_i = e_all[(1 + i) * L:(2 + i) * L]
            a = a + pm_ref[i] * _dot_nt((qf * e_i).astype(BF16), head_stack((key * e_i).astype(BF16)))
        o = _dot(a.astype(BF16), head_stack(vb)) + _dot_nt(q_hat, s_st[...].astype(BF16))

        o_ref[r0:r0 + L, :] = _head_rms(o, gain, blk) * (hg * _sigmoid(hg))
        s_st[...] = s_st[...] * e_g[L - 1:L, :] + blk_f * _dot_tn(vb, k_hat)


def _hgrn(yh, la, l1m, oml, norm_g, layer, batch, consts):
    n = yh.shape[0]
    seq = n // batch
    T = HGRN_T
    steps = seq // T
    row = lambda bi, ci: (bi * steps + ci, 0)
    lay = lambda bi, ci: (layer, 0, 0)
    blk_ones, mstack, pmask = consts
    return pl.pallas_call(
        _hgrn_kernel,
        grid=(batch, steps),
        in_specs=[pl.BlockSpec((T, H_COLS), row),
                  pl.BlockSpec((None, 1, GROUP_W), lay),
                  pl.BlockSpec((None, 1, GROUP_W), lay),
                  pl.BlockSpec((None, 1, GROUP_W), lay),
                  pl.BlockSpec((None, 1, GROUP_W), lay),
                  pl.BlockSpec((GROUP_W, GROUP_W), lambda bi, ci: (0, 0)),
                  pl.BlockSpec(mstack.shape, lambda bi, ci: (0, 0)),
                  pl.BlockSpec(pmask.shape, lambda bi, ci: (0, 0, 0))],
        out_specs=pl.BlockSpec((T, GROUP_W), row),
        out_shape=jax.ShapeDtypeStruct((n, GROUP_W), F32),
        scratch_shapes=[pltpu.VMEM((GROUP_W, GROUP_W), F32)],
        compiler_params=_params(2),
        name="hgrn2",
    )(yh, la, l1m, oml, norm_g, blk_ones, mstack, pmask)


def _swa_kernel(sink_ref, y_ref, cos_ref, sin_ref, qg_ref, kg_ref, ones_ref, o_ref, k_prev, v_prev,
                *, layer):
    T = SWA_T
    Wn = WINDOW
    QW = SWA_Q_HEADS * HEAD_DIM
    step = pl.program_id(1)

    @pl.when(step == 0)
    def _():
        k_prev[...] = jnp.zeros_like(k_prev)
        v_prev[...] = jnp.zeros_like(v_prev)

    cos_t = cos_ref[...]
    sin_t = sin_ref[...]
    blk = ones_ref[...]
    lane = lax.broadcasted_iota(jnp.int32, (T, LANES), 1)
    low_half = lane < HEAD_DIM
    rot_first = (lane % HEAD_DIM) < ROT_HALF

    def norm_rope(x, gain):
        xn = _head_rms(x, gain, blk)
        partner = jnp.where(rot_first, pltpu.roll(xn, LANES - ROT_HALF, 1), pltpu.roll(xn, ROT_HALF, 1))
        return xn * cos_t + partner * sin_t

    k = norm_rope(y_ref[:, QW:QW + LANES], kg_ref[...])
    v = y_ref[:, QW + LANES:QW + 2 * LANES]
    k_sw = pltpu.roll(k, HEAD_DIM, 1)
    v_sw = pltpu.roll(v, HEAD_DIM, 1)
    k2 = [jnp.concatenate([k_prev[g], jnp.where(low_half == (g == 0), k, k_sw).astype(BF16)], axis=0)
          for g in range(SWA_KV_HEADS)]
    v2 = [jnp.concatenate([v_prev[g], jnp.where(low_half == (g == 0), v, v_sw).astype(BF16)], axis=0)
          for g in range(SWA_KV_HEADS)]
    for g in range(SWA_KV_HEADS):
        k_prev[g] = k2[g][T:T + Wn]
        v_prev[g] = v2[g][T:T + Wn]

    q_grp = [(norm_rope(y_ref[:, c * LANES:(c + 1) * LANES], qg_ref[...]) * (HEAD_DIM ** -0.5))
             for c in range(QW // LANES)]

    qi = lax.broadcasted_iota(jnp.int32, (Wn, 2 * Wn), 0)
    ki = lax.broadcasted_iota(jnp.int32, (Wn, 2 * Wn), 1)
    band = (ki > qi) & (ki <= qi + Wn)
    band_first = band & (ki >= jnp.where(step > 0, 0, Wn))
    lane_w = lax.broadcasted_iota(jnp.int32, (Wn, LANES), 1) < HEAD_DIM

    for j in range(T // Wn):
        r0 = j * Wn
        mask = band_first if j == 0 else band
        for g in range(SWA_KV_HEADS):
            kc = k2[g][r0:r0 + 2 * Wn]
            vc = v2[g][r0:r0 + 2 * Wn]
            parts = []
            for c in (2 * g, 2 * g + 1):
                qc = q_grp[c][r0:r0 + Wn]
                parts.append(jnp.where(lane_w, qc, 0.0).astype(BF16))
                parts.append(jnp.where(lane_w, 0.0, qc).astype(BF16))
            s = _dot_nt(jnp.concatenate(parts, axis=0), kc)
            outs = []
            for r in range(SWA_GROUP):
                sink = sink_ref[layer, SWA_GROUP * g + r]
                sr = jnp.where(mask, s[r * Wn:(r + 1) * Wn], -jnp.inf)
                m = jnp.maximum(jnp.max(sr, axis=-1, keepdims=True), sink)
                pexp = jnp.exp(sr - m)
                den = jnp.sum(pexp, axis=-1, keepdims=True) + jnp.exp(sink - m)
                outs.append(_dot(pexp.astype(BF16), vc) / den)
            c0 = 2 * g * LANES
            o_ref[r0:r0 + Wn, c0:c0 + LANES] = jnp.where(lane_w, outs[0], outs[1])
            o_ref[r0:r0 + Wn, c0 + LANES:c0 + 2 * LANES] = jnp.where(lane_w, outs[2], outs[3])


def _swa(ys, cos_t, sin_t, q_gain, k_gain, sinks, layer, batch, blk_ones):
    n = ys.shape[0]
    seq = n // batch
    T = SWA_T
    steps = seq // T
    QW = SWA_Q_HEADS * HEAD_DIM
    row = lambda bi, ci: (bi * steps + ci, 0)
    lay = lambda bi, ci: (layer, 0, 0)
    return pl.pallas_call(
        functools.partial(_swa_kernel, layer=layer),
        grid=(batch, steps),
        in_specs=[pl.BlockSpec(memory_space=pltpu.SMEM),
                  pl.BlockSpec((T, S_COLS), row),
                  pl.BlockSpec((T, LANES), row),
                  pl.BlockSpec((T, LANES), row),
                  pl.BlockSpec((None, 1, LANES), lay),
                  pl.BlockSpec((None, 1, LANES), lay),
                  pl.BlockSpec((LANES, LANES), lambda bi, ci: (0, 0))],
        out_specs=pl.BlockSpec((T, QW), row),
        out_shape=jax.ShapeDtypeStruct((n, QW), F32),
        scratch_shapes=[pltpu.VMEM((SWA_KV_HEADS, WINDOW, LANES), BF16),
                        pltpu.VMEM((SWA_KV_HEADS, WINDOW, LANES), BF16)],
        compiler_params=_params(2),
        name="swa",
    )(sinks, ys, cos_t, sin_t, q_gain, k_gain, blk_ones)


def _tail_kernel(x_ref, mm_ref, mh_ref, ms_ref, p_ref, wo_ref, g2_ref, wu_ref, wd_ref, g3_ref,
                 wg_ref, wp_ref, g4_ref, o_ref):
    W = GROUP_W
    mix = _dot(mm_ref[...].astype(BF16), wo_ref[0:W, :])
    mix = mix + _dot(mh_ref[...].astype(BF16), wo_ref[W:2 * W, :])
    mix = mix + _dot(ms_ref[...].astype(BF16), wo_ref[2 * W:4 * W, :])
    x = x_ref[...] + mix

    h2 = _rms(x, g2_ref[...]).astype(BF16)
    mlp = None
    for c in range(D_FF // D_MODEL):
        u = _dot(h2, wu_ref[:, c * D_MODEL:(c + 1) * D_MODEL])
        act = jnp.square(jnp.maximum(u, 0.0)).astype(BF16)
        d = _dot(act, wd_ref[c * D_MODEL:(c + 1) * D_MODEL, :])
        mlp = d if mlp is None else mlp + d
    x = x + mlp

    gate = _sigmoid(_dot(_rms(x, g3_ref[...]).astype(BF16), wg_ref[...]))
    emb = _rms(_dot(p_ref[...].astype(BF16), wp_ref[...]), g4_ref[...])
    o_ref[...] = x + gate * emb


def _tail(x, mm, mh, ms, p, w_out, g2, w_up, w_down, g3, w_gate, w_proj, g4, layer):
    n = x.shape[0]
    tm = min(TAIL_TM, n)
    tiles = n // tm
    row = lambda i: (i, 0)
    lay = lambda i: (layer, 0, 0)
    once = pl.Buffered(1)

    def wspec(shape):
        return pl.BlockSpec((None,) + shape, lay, pipeline_mode=once)

    return pl.pallas_call(
        _tail_kernel,
        grid=(tiles,),
        in_specs=[pl.BlockSpec((tm, D_MODEL), row),
                  pl.BlockSpec((tm, GROUP_W), row),
                  pl.BlockSpec((tm, GROUP_W), row),
                  pl.BlockSpec((tm, 2 * GROUP_W), row),
                  pl.BlockSpec((None, tm, PLE_DIM), lambda i: (layer, i, 0)),
                  wspec((D_MODEL, D_MODEL)),
                  wspec((1, D_MODEL)),
                  wspec((D_MODEL, D_FF)),
                  wspec((D_FF, D_MODEL)),
                  wspec((1, D_MODEL)),
                  wspec((D_MODEL, D_MODEL)),
                  wspec((PLE_DIM, D_MODEL)),
                  wspec((1, D_MODEL))],
        out_specs=pl.BlockSpec((tm, D_MODEL), row),
        out_shape=jax.ShapeDtypeStruct((n, D_MODEL), F32),
        compiler_params=_params(1),
        name="tail",
    )(x, mm, mh, ms, p, w_out, g2, w_up, w_down, g3, w_gate, w_proj, g4)


def kernel(x, p, positions, in_norm_g, w_in, b_in, mlstm_f_bias, mlstm_conv_w, mlstm_conv_b,
           mlstm_norm_g, hgrn_lb_logits, hgrn_norm_g, swa_q_norm_g, swa_k_norm_g, swa_sinks,
           w_out, mlp_norm_g, w_up, w_down, ple_norm_g, w_ple_gate, w_ple_proj, ple_post_norm_g):
    batch, seq, d_model = x.shape
    depth = w_in.shape[0]
    n = batch * seq
    assert d_model == D_MODEL and seq % max(MLSTM_T, HGRN_T, SWA_T) == 0 and n % IN_TM == 0

    n_raw = GATE_OFF + GATE_COLS
    pad = LANES - GATE_COLS
    w_in_p = jnp.concatenate(
        [w_in[..., :n_raw], jnp.zeros((depth, D_MODEL, pad), w_in.dtype), w_in[..., n_raw:]],
        axis=-1).astype(BF16)
    b_in_p = jnp.concatenate(
        [b_in[..., :n_raw], jnp.zeros((depth, pad), b_in.dtype), b_in[..., n_raw:]],
        axis=-1).astype(F32).reshape(depth, 1, IN_COLS)
    f_bias = jnp.zeros((depth, 1, LANES), F32).at[:, 0, N_HEADS:GATE_COLS].set(mlstm_f_bias.astype(F32))
    row3 = lambda a: a.astype(F32).reshape(depth, 1, a.shape[-1])
    q_gain = row3(jnp.tile(swa_q_norm_g, (1, LANES // HEAD_DIM)))
    k_gain = row3(jnp.tile(swa_k_norm_g, (1, LANES // HEAD_DIM)))
    w_out_b, w_up_b, w_down_b = w_out.astype(BF16), w_up.astype(BF16), w_down.astype(BF16)
    w_gate_b, w_proj_b = w_ple_gate.astype(BF16), w_ple_proj.astype(BF16)

    blk256 = _blk_ones(GROUP_W)
    blk128 = _blk_ones(LANES)
    tril = jnp.asarray(np.tril(np.ones((MLSTM_T, MLSTM_T), np.float32)), dtype=BF16)
    expand = jnp.asarray(
        (np.arange(LANES)[:, None] == (np.arange(GROUP_W) // HEAD_DIM)[None, :]).astype(np.float32),
        dtype=BF16)
    hgrn_mstack, hgrn_pmask = _hgrn_constants()

    cos_t, sin_t = _rope_tables(positions)
    la, l1m, oml = [row3(a) for a in _hgrn_lower_bounds(hgrn_lb_logits)]
    sinks = swa_sinks.astype(F32)

    xf = x.reshape(n, D_MODEL)
    for l in range(depth):
        ym, yh, ys = _in_proj(xf, row3(in_norm_g), w_in_p, b_in_p, l)
        mm = _mlstm(ym, mlstm_conv_w.astype(F32), row3(mlstm_conv_b), f_bias, row3(mlstm_norm_g),
                    l, batch, (blk256, tril, expand))
        mh = _hgrn(yh, la, l1m, oml, row3(hgrn_norm_g), l, batch, (blk256, hgrn_mstack, hgrn_pmask))
        ms = _swa(ys, cos_t, sin_t, q_gain, k_gain, sinks, l, batch, blk128)
        xf = _tail(xf, mm, mh, ms, p.reshape(depth, n, PLE_DIM), w_out_b, row3(mlp_norm_g), w_up_b,
                   w_down_b, row3(ple_norm_g), w_gate_b, w_proj_b, row3(ple_post_norm_g), l)
    return xf.reshape(batch, seq, D_MODEL)
```

```python
import functools

import numpy as np
import jax
import jax.numpy as jnp
from jax import lax
from jax.experimental import pallas as pl
from jax.experimental.pallas import tpu as pltpu

F32 = jnp.float32
BF16 = jnp.bfloat16

D_MODEL = 1024
N_HEADS = 4
HEAD_DIM = 64
GROUP_W = N_HEADS * HEAD_DIM
CONV_K = 4
SWA_Q_HEADS = 8
SWA_KV_HEADS = 2
SWA_GROUP = SWA_Q_HEADS // SWA_KV_HEADS
WINDOW = 128
ROPE_THETA = 500000.0
ROT_DIM = HEAD_DIM // 4
ROT_HALF = ROT_DIM // 2
D_FF = 4 * D_MODEL
PLE_DIM = 256
EPS = 1e-6
LOG2_E = 1.4426950408889634

LANES = 128
SUBLANES = 8
VMEM_LIMIT_BYTES = 56 * 1024 * 1024

GATE_COLS = 2 * N_HEADS
M_COLS = 4 * GROUP_W + LANES
H_COLS = 4 * GROUP_W
S_COLS = SWA_Q_HEADS * HEAD_DIM + 2 * SWA_KV_HEADS * HEAD_DIM
IN_COLS = M_COLS + H_COLS + S_COLS
GATE_OFF = 4 * GROUP_W

IN_TM = 512
TAIL_TM = 512
ROPE_TM = 2048
MLSTM_T = 256
HGRN_T = 256
HGRN_L = 64
SWA_T = 512

N_SPLIT = 2
assert HGRN_L == HEAD_DIM


def _params(n_grid_dims, flags=None):
    return pltpu.CompilerParams(
        dimension_semantics=("arbitrary",) * n_grid_dims,
        vmem_limit_bytes=VMEM_LIMIT_BYTES,
        flags=flags)


def _dot(a, b):
    return jnp.dot(a, b, preferred_element_type=F32)


def _dot_nt(a, b):
    return lax.dot_general(a, b, (((1,), (1,)), ((), ())), preferred_element_type=F32)


def _dot_tn(a, b):
    return lax.dot_general(a, b, (((0,), (0,)), ((), ())), preferred_element_type=F32)


def _split(x, n=N_SPLIT):
    pieces = []
    r = x
    for _ in range(n):
        p = r.astype(BF16)
        pieces.append(p)
        r = r - p.astype(F32)
    return pieces


def _dot_const_left(m, x):
    acc = None
    for p in _split(x):
        t = _dot(m, p)
        acc = t if acc is None else acc + t
    return acc


def _dot_const_right(x, m):
    acc = None
    for p in _split(x):
        t = _dot(p, m)
        acc = t if acc is None else acc + t
    return acc


def _sigmoid(x):
    return 1.0 / (1.0 + jnp.exp(-x))


def _log_sigmoid(x):
    return jnp.minimum(x, 0.0) - jnp.log1p(jnp.exp(-jnp.abs(x)))


def _rms(x, g):
    return x * lax.rsqrt(jnp.mean(x * x, axis=-1, keepdims=True) + EPS) * g


def _head_rms(x, g, blk_ones):
    ms = _dot(jnp.square(x).astype(BF16), blk_ones) * (1.0 / HEAD_DIM)
    return x * lax.rsqrt(ms + EPS) * g


def _lane_head_id(shape):
    return lax.broadcasted_iota(jnp.int32, shape, len(shape) - 1) // HEAD_DIM


def _blk_ones(width):
    hid = np.arange(width) // HEAD_DIM
    return jnp.asarray((hid[:, None] == hid[None, :]).astype(np.float32), dtype=BF16)


def _hgrn_level_sizes():
    sizes = []
    c = 2
    while c <= HGRN_L:
        sizes.append(c)
        c *= 2
    return sizes


def _hgrn_constants():
    L = HGRN_L
    t = np.arange(L)[:, None]
    u = np.arange(L)[None, :]
    mats = [(u <= t), (u > t)]
    masks = [(t == u)]
    for c in _hgrn_level_sizes():
        mid = (t // c) * c + c // 2 - 1
        second = t > mid
        mats.append(np.where(second, (u > mid) & (u <= t), (u > t) & (u <= mid)))
        same_blk = (t // c) == (u // c)
        u_mid = (u // c) * c + c // 2 - 1
        masks.append(same_blk & second & (u <= u_mid))
    mstack = np.concatenate([m.astype(np.float32) for m in mats], axis=0)
    pmask = np.stack([np.tile(m.astype(np.float32), (1, N_HEADS)) for m in masks])
    return jnp.asarray(mstack, dtype=BF16), jnp.asarray(pmask, dtype=F32)


def _rope_kernel(pos_ref, pat_ref, cos_ref, sin_ref):
    ang = pos_ref[...].astype(F32) * pat_ref[0:1, :]
    cos_ref[...] = jnp.cos(ang)
    sin_ref[...] = jnp.sin(ang) * pat_ref[1:2, :]


def _rope_tables(positions):
    n = positions.size
    inv_freq = ROPE_THETA ** (-jnp.arange(0, ROT_DIM, 2, dtype=F32) / ROT_DIM)
    lane = np.arange(LANES) % HEAD_DIM
    freq = jnp.where(lane < ROT_DIM, inv_freq[lane % ROT_HALF], 0.0)
    sign = np.where(lane < ROT_HALF, -1.0, np.where(lane < ROT_DIM, 1.0, 0.0)).astype(np.float32)
    pat = jnp.zeros((SUBLANES, LANES), F32).at[0].set(freq).at[1].set(sign)
    tm = min(ROPE_TM, n)
    out = jax.ShapeDtypeStruct((n, LANES), F32)
    return pl.pallas_call(
        _rope_kernel,
        grid=(n // tm,),
        in_specs=[pl.BlockSpec((tm, 1), lambda i: (i, 0)),
                  pl.BlockSpec((SUBLANES, LANES), lambda i: (0, 0))],
        out_specs=[pl.BlockSpec((tm, LANES), lambda i: (i, 0))] * 2,
        out_shape=[out, out],
        compiler_params=_params(1),
        name="rope_tables",
    )(positions.reshape(n, 1), pat)


def _lb_kernel(lg_ref, la_ref, l1m_ref, oml_ref):
    z = lg_ref[...]
    e = jnp.exp(z - jnp.max(z, axis=0, keepdims=True))
    sm = e / jnp.sum(e, axis=0, keepdims=True)
    depth = z.shape[0]
    run = sm[0:1, :]
    first = run
    for l in range(depth):
        if l > 0:
            run = run + sm[l:l + 1, :]
        lb = run - first
        la_ref[l:l + 1, :] = jnp.log(lb)
        l1m_ref[l:l + 1, :] = jnp.log1p(-lb)
        oml_ref[l:l + 1, :] = 1.0 - lb


def _hgrn_lower_bounds(logits):
    out = jax.ShapeDtypeStruct(logits.shape, F32)
    return pl.pallas_call(
        _lb_kernel, out_shape=[out, out, out], name="hgrn_lower_bounds",
    )(logits.astype(F32))


def _in_proj_kernel(x_ref, g_ref, w_ref, b_ref, cw_ref, cb_ref, la_ref, l1m_ref, oml_ref,
                    cos_ref, sin_ref, qg_ref, kg_ref, ones_ref,
                    ml_ref, gt_ref, hb_ref, lf_ref, sw_ref, xbuf, *, tiles_per_seq):
    TM = x_ref.shape[0]
    W = GROUP_W

    @pl.when(pl.program_id(0) % tiles_per_seq == 0)
    def _():
        xbuf[0:SUBLANES, :] = jnp.zeros((SUBLANES, 2 * W), F32)

    h = _rms(x_ref[...], g_ref[...]).astype(BF16)

    def proj(lo, width):
        return _dot(h, w_ref[:, lo:lo + width]) + b_ref[:, lo:lo + width]

    cos_t = cos_ref[...]
    sin_t = sin_ref[...]
    blk = ones_ref[...]
    lane = lax.broadcasted_iota(jnp.int32, (TM, LANES), 1)
    low_half = lane < HEAD_DIM
    rot_first = (lane % HEAD_DIM) < ROT_HALF
    s0 = M_COLS + H_COLS
    QW = SWA_Q_HEADS * HEAD_DIM

    def head_sumsq(y):
        width = y.shape[-1]
        return _dot(jnp.square(y).astype(BF16), blk[0:width, 0:width])

    def norm_rope(v, sumsq, gain):
        xn = v * lax.rsqrt(sumsq * (1.0 / HEAD_DIM) + EPS) * gain
        partner = jnp.where(rot_first, pltpu.roll(xn, LANES - ROT_HALF, 1), pltpu.roll(xn, ROT_HALF, 1))
        return xn * cos_t + partner * sin_t

    def swa_q(c, y, sumsq):
        for half in range(2):
            sl = slice(half * LANES, (half + 1) * LANES)
            sw_ref[:, (c + half) * LANES:(c + half + 1) * LANES] = (
                norm_rope(y[:, sl], sumsq[:, sl], qg_ref[...]) * (HEAD_DIM ** -0.5)).astype(BF16)

    def swa_kv(y, sumsq):
        k = norm_rope(y[:, 0:LANES], sumsq, kg_ref[...])
        v = y[:, LANES:2 * LANES]
        k_sw = pltpu.roll(k, HEAD_DIM, 1)
        v_sw = pltpu.roll(v, HEAD_DIM, 1)
        sw_ref[:, QW:QW + LANES] = jnp.where(low_half, k, k_sw).astype(BF16)
        sw_ref[:, QW + LANES:QW + 2 * LANES] = jnp.where(low_half, k_sw, k).astype(BF16)
        sw_ref[:, QW + 2 * LANES:QW + 3 * LANES] = jnp.where(low_half, v, v_sw).astype(BF16)
        sw_ref[:, QW + 3 * LANES:QW + 4 * LANES] = jnp.where(low_half, v_sw, v).astype(BF16)

    def mlstm_qk(y):
        xbuf[SUBLANES:SUBLANES + TM, :] = y
        acc = cb_ref[...] + cw_ref[CONV_K - 1:CONV_K, :] * xbuf[SUBLANES:SUBLANES + TM, :]
        for j in range(1, CONV_K):
            acc = acc + cw_ref[CONV_K - 1 - j:CONV_K - j, :] * xbuf[SUBLANES - j:SUBLANES - j + TM, :]
        xbuf[0:SUBLANES, :] = xbuf[TM:TM + SUBLANES, :]
        qk = acc * _sigmoid(acc)
        ml_ref[:, 0:W] = qk[:, 0:W].astype(BF16)
        ml_ref[:, W:2 * W] = (qk[:, W:2 * W] * (HEAD_DIM ** -0.5)).astype(BF16)

    def hgrn_qf(y):
        hq = y[:, 0:W]
        hf = y[:, W:2 * W]
        la = la_ref[...]
        u = jnp.exp(-jnp.abs(hf))
        t = l1m_ref[...] + (jnp.minimum(hf, 0.0) - jnp.log1p(u))
        lf_ref[...] = jnp.maximum(la, t) + jnp.log1p(jnp.exp(-jnp.abs(la - t)))
        hb_ref[:, 0:W] = (hq * _sigmoid(hq)).astype(BF16)
        hb_ref[:, W:2 * W] = (oml_ref[...] * jnp.where(hf >= 0.0, u, 1.0) / (1.0 + u)).astype(BF16)

    y_kv = proj(s0 + QW, 2 * LANES)
    y_q0 = proj(s0, 2 * LANES)
    y_q1 = proj(s0 + 2 * LANES, 2 * LANES)
    swa_kv(y_kv, head_sumsq(y_kv[:, 0:LANES]))
    y_h = proj(M_COLS, 2 * W)
    swa_q(0, y_q0, head_sumsq(y_q0))
    y_m = proj(0, 2 * W)
    swa_q(2, y_q1, head_sumsq(y_q1))
    hgrn_qf(y_h)
    mlstm_qk(y_m)
    y = proj(M_COLS + 2 * W, 2 * W)
    hg = y[:, W:2 * W]
    hb_ref[:, 2 * W:3 * W] = y[:, 0:W].astype(BF16)
    hb_ref[:, 3 * W:4 * W] = (hg * _sigmoid(hg)).astype(BF16)
    gt_ref[...] = proj(GATE_OFF, LANES)
    y = proj(2 * W, 2 * W)
    ml_ref[:, 2 * W:3 * W] = y[:, 0:W].astype(BF16)
    ml_ref[:, 3 * W:4 * W] = _sigmoid(y[:, W:2 * W]).astype(BF16)


def _in_proj(x, g, w, b, conv_w, conv_b, la, l1m, oml, cos_t, sin_t, q_gain, k_gain, blk_ones,
             layer, seq):
    n = x.shape[0]
    tm = min(IN_TM, n)
    row = lambda i: (i, 0)
    lay = lambda i: (layer, 0, 0)
    W4 = 4 * GROUP_W
    return pl.pallas_call(
        functools.partial(_in_proj_kernel, tiles_per_seq=seq // tm),
        grid=(n // tm,),
        in_specs=[pl.BlockSpec((tm, D_MODEL), row),
                  pl.BlockSpec((None, 1, D_MODEL), lay),
                  pl.BlockSpec((None, D_MODEL, IN_COLS), lay),
                  pl.BlockSpec((None, 1, IN_COLS), lay),
                  pl.BlockSpec((None, CONV_K, 2 * GROUP_W), lay),
                  pl.BlockSpec((None, 1, 2 * GROUP_W), lay),
                  pl.BlockSpec((None, 1, GROUP_W), lay),
                  pl.BlockSpec((None, 1, GROUP_W), lay),
                  pl.BlockSpec((None, 1, GROUP_W), lay),
                  pl.BlockSpec((tm, LANES), row),
                  pl.BlockSpec((tm, LANES), row),
                  pl.BlockSpec((None, 1, LANES), lay),
                  pl.BlockSpec((None, 1, LANES), lay),
                  pl.BlockSpec((GROUP_W, GROUP_W), lambda i: (0, 0))],
        out_specs=[pl.BlockSpec((tm, W4), row),
                   pl.BlockSpec((tm, LANES), row),
                   pl.BlockSpec((tm, W4), row),
                   pl.BlockSpec((tm, GROUP_W), row),
                   pl.BlockSpec((tm, W4), row)],
        out_shape=[jax.ShapeDtypeStruct((n, W4), BF16),
                   jax.ShapeDtypeStruct((n, LANES), F32),
                   jax.ShapeDtypeStruct((n, W4), BF16),
                   jax.ShapeDtypeStruct((n, GROUP_W), F32),
                   jax.ShapeDtypeStruct((n, W4), BF16)],
        scratch_shapes=[pltpu.VMEM((tm + SUBLANES, 2 * GROUP_W), F32)],
        compiler_params=_params(1),
        name="in_proj",
    )(x, g, w, b, conv_w, conv_b, la, l1m, oml, cos_t, sin_t, q_gain, k_gain, blk_ones)


def _mlstm_kernel(y_ref, gt_ref, fb_ref, g_ref, ones_ref, tril_ref, exp_ref, o_ref,
                  c_st, n_st, m_st):
    T = MLSTM_T
    W = GROUP_W
    step = pl.program_id(1)

    @pl.when(step == 0)
    def _():
        c_st[...] = jnp.zeros_like(c_st)
        n_st[...] = jnp.zeros_like(n_st)
        m_st[...] = jnp.zeros_like(m_st)

    qb = y_ref[:, 0:W]
    kb = y_ref[:, W:2 * W]
    vb = y_ref[:, 2 * W:3 * W]
    out_gate = y_ref[:, 3 * W:4 * W]

    gates = gt_ref[...]
    i_pre = gates
    lf = pltpu.roll(_log_sigmoid(gates + fb_ref[...]), LANES - N_HEADS, 1)
    b = _dot_const_left(tril_ref[...], lf)
    a = i_pre - b
    row = lax.broadcasted_iota(jnp.int32, (T, LANES), 0)
    cm = a
    sh = 1
    while sh < T:
        cm = jnp.maximum(cm, jnp.where(row >= sh, pltpu.roll(cm, sh, 0), -jnp.inf))
        sh *= 2
    m_prev = m_st[0:1, :]
    m_j = b + jnp.maximum(m_prev, cm)
    m_new = m_j[T - 1:T, :]
    b_last = b[T - 1:T, :]
    w_inter = jnp.exp(b + m_prev - m_j)
    e_den = jnp.exp(-m_j)
    w_s = jnp.exp(b_last + a - m_new)
    decay = jnp.exp(b_last + m_prev - m_new)
    cexp = b - m_j
    a_t = a.T

    stacked = jnp.concatenate(
        [w_inter, e_den, w_s, jnp.broadcast_to(decay, (SUBLANES, LANES))], axis=0)
    head_lane = lax.broadcasted_iota(jnp.int32, stacked.shape, 1) < N_HEADS
    ex = _dot_const_right(jnp.where(head_lane, stacked, 0.0), exp_ref[...])
    w_inter_x = ex[0:T]
    e_den_x = ex[T:2 * T]
    w_s_x = ex[2 * T:3 * T]
    decay_x = ex[3 * T:3 * T + 1]

    blk = ones_ref[...]
    causal = (lax.broadcasted_iota(jnp.int32, (T, T), 1)
              <= lax.broadcasted_iota(jnp.int32, (T, T), 0))
    num = None
    den = None
    for h in range(N_HEADS):
        head_row = blk[h * HEAD_DIM:h * HEAD_DIM + 1, :]
        dm = cexp[:, h:h + 1] + a_t[h:h + 1, :]
        w_intra = jnp.exp(jnp.where(causal, dm, -jnp.inf))
        s = (_dot_nt(qb * head_row, kb) * w_intra).astype(BF16)
        dn = _dot(s, vb * head_row)
        dd = _dot(s, jnp.broadcast_to(head_row, (T, W)))
        num = dn if num is None else num + dn
        den = dd if den is None else den + dd
    num = num + w_inter_x * _dot(qb, c_st[...].astype(BF16))
    den = den + w_inter_x * _dot(qb, n_st[...].astype(BF16))
    hh = num / jnp.maximum(jnp.abs(den), e_den_x)

    o_ref[...] = _head_rms(hh, g_ref[...], blk) * out_gate.astype(F32)

    kw = kb * w_s_x.astype(BF16)
    blk_f = blk.astype(F32)
    c_st[...] = decay_x * c_st[...] + blk_f * _dot_tn(kw, vb)
    n_st[...] = decay_x * n_st[...] + blk_f * _dot_tn(kw, jnp.ones((T, W), BF16))
    m_st[...] = jnp.broadcast_to(m_new, m_st.shape)


def _mlstm(ml, gt, f_bias, norm_g, layer, batch, consts):
    n = ml.shape[0]
    seq = n // batch
    T = MLSTM_T
    steps = seq // T
    row = lambda bi, ci: (bi * steps + ci, 0)
    lay = lambda bi, ci: (layer, 0, 0)
    cst = lambda bi, ci: (0, 0)
    blk_ones, tril, expand = consts
    return pl.pallas_call(
        _mlstm_kernel,
        grid=(batch, steps),
        in_specs=[pl.BlockSpec((T, 4 * GROUP_W), row),
                  pl.BlockSpec((T, LANES), row),
                  pl.BlockSpec((None, 1, LANES), lay),
                  pl.BlockSpec((None, 1, GROUP_W), lay),
                  pl.BlockSpec((GROUP_W, GROUP_W), cst),
                  pl.BlockSpec((T, T), cst),
                  pl.BlockSpec((LANES, GROUP_W), cst)],
        out_specs=pl.BlockSpec((T, GROUP_W), row),
        out_shape=jax.ShapeDtypeStruct((n, GROUP_W), F32),
        scratch_shapes=[pltpu.VMEM((GROUP_W, GROUP_W), F32),
                        pltpu.VMEM((GROUP_W, GROUP_W), F32),
                        pltpu.VMEM((SUBLANES, LANES), F32)],
        compiler_params=_params(2),
        name="mlstm",
    )(ml, gt, f_bias, norm_g, blk_ones, tril, expand)


def _hgrn_kernel(y_ref, lf_ref, g_ref, ones_ref, mst_ref, pm_ref, o_ref, s_st):
    L = HGRN_L
    W = GROUP_W
    step = pl.program_id(1)

    @pl.when(step == 0)
    def _():
        s_st[...] = jnp.zeros_like(s_st)

    gain = g_ref[...]
    blk = ones_ref[...]
    blk_f = blk.astype(F32)
    mstack = mst_ref[...]
    n_lvl = pm_ref.shape[0]

    def head_stack(xb):
        return jnp.concatenate([xb] * N_HEADS, axis=0) * blk

    for ci in range(HGRN_T // L):
        r0 = ci * L
        qf = y_ref[r0:r0 + L, 0:W]
        key = y_ref[r0:r0 + L, W:2 * W]
        vb = y_ref[r0:r0 + L, 2 * W:3 * W]
        out_gate = y_ref[r0:r0 + L, 3 * W:4 * W]

        e_f32 = jnp.exp2(_dot_const_left(mstack, lf_ref[r0:r0 + L, :] * LOG2_E))
        e_all = e_f32.astype(BF16)
        q_hat = qf * e_all[0:L]
        k_hat = key * e_all[L:2 * L]

        a = pm_ref[0] * _dot_nt(qf, head_stack(key))
        for i in range(1, n_lvl):
            e_i = e_all[(1 + i) * L:(2 + i) * L]
            a = a + pm_ref[i] * _dot_nt(qf * e_i, head_stack(key * e_i))
        o = _dot(a.astype(BF16), head_stack(vb)) + _dot_nt(q_hat, s_st[...].astype(BF16))

        o_ref[r0:r0 + L, :] = _head_rms(o, gain, blk) * out_gate.astype(F32)
        s_st[...] = s_st[...] * e_f32[L - 1:L, :] + blk_f * _dot_tn(vb, k_hat)


def _hgrn(hb, lf, norm_g, layer, batch, consts):
    n = hb.shape[0]
    seq = n // batch
    T = HGRN_T
    steps = seq // T
    row = lambda bi, ci: (bi * steps + ci, 0)
    lay = lambda bi, ci: (layer, 0, 0)
    blk_ones, mstack, pmask = consts
    return pl.pallas_call(
        _hgrn_kernel,
        grid=(batch, steps),
        in_specs=[pl.BlockSpec((T, H_COLS), row),
                  pl.BlockSpec((T, GROUP_W), row),
                  pl.BlockSpec((None, 1, GROUP_W), lay),
                  pl.BlockSpec((GROUP_W, GROUP_W), lambda bi, ci: (0, 0)),
                  pl.BlockSpec(mstack.shape, lambda bi, ci: (0, 0)),
                  pl.BlockSpec(pmask.shape, lambda bi, ci: (0, 0, 0))],
        out_specs=pl.BlockSpec((T, GROUP_W), row),
        out_shape=jax.ShapeDtypeStruct((n, GROUP_W), F32),
        scratch_shapes=[pltpu.VMEM((GROUP_W, GROUP_W), F32)],
        compiler_params=_params(2),
        name="hgrn2",
    )(hb, lf, norm_g, blk_ones, mstack, pmask)


def _swa_kernel(sink_ref, y_ref, o_ref, k_prev, v_prev, *, layer):
    T = SWA_T
    Wn = WINDOW
    QW = SWA_Q_HEADS * HEAD_DIM
    step = pl.program_id(1)

    @pl.when(step == 0)
    def _():
        k_prev[...] = jnp.zeros_like(k_prev)
        v_prev[...] = jnp.zeros_like(v_prev)

    k2 = [jnp.concatenate([k_prev[g], y_ref[:, QW + g * LANES:QW + (g + 1) * LANES]], axis=0)
          for g in range(SWA_KV_HEADS)]
    v2 = [jnp.concatenate([v_prev[g], y_ref[:, QW + (2 + g) * LANES:QW + (3 + g) * LANES]], axis=0)
          for g in range(SWA_KV_HEADS)]
    for g in range(SWA_KV_HEADS):
        k_prev[g] = k2[g][T:T + Wn]
        v_prev[g] = v2[g][T:T + Wn]

    qi = lax.broadcasted_iota(jnp.int32, (Wn, 2 * Wn), 0)
    ki = lax.broadcasted_iota(jnp.int32, (Wn, 2 * Wn), 1)
    band = (ki > qi) & (ki <= qi + Wn)
    band_first = band & (ki >= jnp.where(step > 0, 0, Wn))
    lane_w = lax.broadcasted_iota(jnp.int32, (Wn, LANES), 1) < HEAD_DIM
    lane_row = lax.broadcasted_iota(jnp.int32, (1, LANES), 1) < HEAD_DIM
    low_row = jnp.where(lane_row, 1.0, 0.0).astype(BF16)
    high_row = jnp.where(lane_row, 0.0, 1.0).astype(BF16)

    for j in range(T // Wn):
        r0 = j * Wn
        mask = band_first if j == 0 else band
        for g in range(SWA_KV_HEADS):
            kc = k2[g][r0:r0 + 2 * Wn]
            vc = v2[g][r0:r0 + 2 * Wn]
            parts = []
            for c in (2 * g, 2 * g + 1):
                qc = y_ref[r0:r0 + Wn, c * LANES:(c + 1) * LANES]
                parts.append(qc * low_row)
                parts.append(qc * high_row)
            s = _dot_nt(jnp.concatenate(parts, axis=0), kc)
            outs = []
            for r in range(SWA_GROUP):
                sink = sink_ref[layer, SWA_GROUP * g + r]
                sr = jnp.where(mask, s[r * Wn:(r + 1) * Wn], -jnp.inf)
                m = jnp.maximum(jnp.max(sr, axis=-1, keepdims=True), sink)
                pexp = jnp.exp(sr - m)
                den = jnp.sum(pexp, axis=-1, keepdims=True) + jnp.exp(sink - m)
                outs.append(_dot(pexp.astype(BF16), vc) / den)
            c0 = 2 * g * LANES
            o_ref[r0:r0 + Wn, c0:c0 + LANES] = jnp.where(lane_w, outs[0], outs[1])
            o_ref[r0:r0 + Wn, c0 + LANES:c0 + 2 * LANES] = jnp.where(lane_w, outs[2], outs[3])


def _swa(sw, sinks, layer, batch):
    n = sw.shape[0]
    seq = n // batch
    T = SWA_T
    steps = seq // T
    QW = SWA_Q_HEADS * HEAD_DIM
    row = lambda bi, ci: (bi * steps + ci, 0)
    return pl.pallas_call(
        functools.partial(_swa_kernel, layer=layer),
        grid=(batch, steps),
        in_specs=[pl.BlockSpec(memory_space=pltpu.SMEM),
                  pl.BlockSpec((T, 4 * GROUP_W), row)],
        out_specs=pl.BlockSpec((T, QW), row),
        out_shape=jax.ShapeDtypeStruct((n, QW), F32),
        scratch_shapes=[pltpu.VMEM((SWA_KV_HEADS, WINDOW, LANES), BF16),
                        pltpu.VMEM((SWA_KV_HEADS, WINDOW, LANES), BF16)],
        compiler_params=_params(2),
        name="swa",
    )(sinks, sw)


def _tail_kernel(x_ref, mm_ref, mh_ref, ms_ref, p_ref, wo_ref, g2_ref, wu_ref, wd_ref, g3_ref,
                 wg_ref, wp_ref, g4_ref, o_ref):
    W = GROUP_W
    mix = _dot(mm_ref[...].astype(BF16), wo_ref[0:W, :])
    mix = mix + _dot(mh_ref[...].astype(BF16), wo_ref[W:2 * W, :])
    mix = mix + _dot(ms_ref[...].astype(BF16), wo_ref[2 * W:4 * W, :])
    x = x_ref[...] + mix

    h2 = _rms(x, g2_ref[...]).astype(BF16)
    mlp = None
    for c in range(D_FF // D_MODEL):
        u = _dot(h2, wu_ref[:, c * D_MODEL:(c + 1) * D_MODEL])
        act = jnp.square(jnp.maximum(u, 0.0)).astype(BF16)
        d = _dot(act, wd_ref[c * D_MODEL:(c + 1) * D_MODEL, :])
        mlp = d if mlp is None else mlp + d
    x = x + mlp

    gate = _sigmoid(_dot(_rms(x, g3_ref[...]).astype(BF16), wg_ref[...]))
    emb = _rms(_dot(p_ref[...].astype(BF16), wp_ref[...]), g4_ref[...])
    o_ref[...] = x + gate * emb


def _tail(x, mm, mh, ms, p, w_out, g2, w_up, w_down, g3, w_gate, w_proj, g4, layer):
    n = x.shape[0]
    tm = min(TAIL_TM, n)
    tiles = n // tm
    row = lambda i: (i, 0)
    lay = lambda i: (layer, 0, 0)
    once = pl.Buffered(1)

    def wspec(shape):
        return pl.BlockSpec((None,) + shape, lay, pipeline_mode=once)

    return pl.pallas_call(
        _tail_kernel,
        grid=(tiles,),
        in_specs=[pl.BlockSpec((tm, D_MODEL), row),
                  pl.BlockSpec((tm, GROUP_W), row),
                  pl.BlockSpec((tm, GROUP_W), row),
                  pl.BlockSpec((tm, 2 * GROUP_W), row),
                  pl.BlockSpec((None, tm, PLE_DIM), lambda i: (layer, i, 0)),
                  wspec((D_MODEL, D_MODEL)),
                  wspec((1, D_MODEL)),
                  wspec((D_MODEL, D_FF)),
                  wspec((D_FF, D_MODEL)),
                  wspec((1, D_MODEL)),
                  wspec((D_MODEL, D_MODEL)),
                  wspec((PLE_DIM, D_MODEL)),
                  wspec((1, D_MODEL))],
        out_specs=pl.BlockSpec((tm, D_MODEL), row),
        out_shape=jax.ShapeDtypeStruct((n, D_MODEL), F32),
        compiler_params=_params(1),
        name="tail",
    )(x, mm, mh, ms, p, w_out, g2, w_up, w_down, g3, w_gate, w_proj, g4)


def kernel(x, p, positions, in_norm_g, w_in, b_in, mlstm_f_bias, mlstm_conv_w, mlstm_conv_b,
           mlstm_norm_g, hgrn_lb_logits, hgrn_norm_g, swa_q_norm_g, swa_k_norm_g, swa_sinks,
           w_out, mlp_norm_g, w_up, w_down, ple_norm_g, w_ple_gate, w_ple_proj, ple_post_norm_g):
    batch, seq, d_model = x.shape
    depth = w_in.shape[0]
    n = batch * seq
    assert d_model == D_MODEL and seq % max(MLSTM_T, HGRN_T, SWA_T) == 0 and n % IN_TM == 0

    n_raw = GATE_OFF + GATE_COLS
    pad = LANES - GATE_COLS
    w_in_p = jnp.concatenate(
        [w_in[..., :n_raw], jnp.zeros((depth, D_MODEL, pad), w_in.dtype), w_in[..., n_raw:]],
        axis=-1).astype(BF16)
    b_in_p = jnp.concatenate(
        [b_in[..., :n_raw], jnp.zeros((depth, pad), b_in.dtype), b_in[..., n_raw:]],
        axis=-1).astype(F32).reshape(depth, 1, IN_COLS)
    f_bias = jnp.zeros((depth, 1, LANES), F32).at[:, 0, N_HEADS:GATE_COLS].set(mlstm_f_bias.astype(F32))
    row3 = lambda a: a.astype(F32).reshape(depth, 1, a.shape[-1])
    q_gain = row3(jnp.tile(swa_q_norm_g, (1, LANES // HEAD_DIM)))
    k_gain = row3(jnp.tile(swa_k_norm_g, (1, LANES // HEAD_DIM)))
    w_out_b, w_up_b, w_down_b = w_out.astype(BF16), w_up.astype(BF16), w_down.astype(BF16)
    w_gate_b, w_proj_b = w_ple_gate.astype(BF16), w_ple_proj.astype(BF16)

    blk256 = _blk_ones(GROUP_W)
    tril = jnp.asarray(np.tril(np.ones((MLSTM_T, MLSTM_T), np.float32)), dtype=BF16)
    expand = jnp.asarray(
        (np.arange(LANES)[:, None] == (np.arange(GROUP_W) // HEAD_DIM)[None, :]).astype(np.float32),
        dtype=BF16)
    hgrn_mstack, hgrn_pmask = _hgrn_constants()

    cos_t, sin_t = _rope_tables(positions)
    la, l1m, oml = [row3(a) for a in _hgrn_lower_bounds(hgrn_lb_logits)]
    sinks = swa_sinks.astype(F32)

    xf = x.reshape(n, D_MODEL)
    for l in range(depth):
        ml, gt, hb, lf, sw = _in_proj(
            xf, row3(in_norm_g), w_in_p, b_in_p, mlstm_conv_w.astype(F32), row3(mlstm_conv_b),
            la, l1m, oml, cos_t, sin_t, q_gain, k_gain, blk256, l, seq)
        mm = _mlstm(ml, gt, f_bias, row3(mlstm_norm_g), l, batch, (blk256, tril, expand))
        mh = _hgrn(hb, lf, row3(hgrn_norm_g), l, batch, (blk256, hgrn_mstack, hgrn_pmask))
        ms = _swa(sw, sinks, l, batch)
        xf = _tail(xf, mm, mh, ms, p.reshape(depth, n, PLE_DIM), w_out_b, row3(mlp_norm_g), w_up_b,
                   w_down_b, row3(ple_norm_g), w_gate_b, w_proj_b, row3(ple_post_norm_g), l)
    return xf.reshape(batch, seq, D_MODEL)
```

```python
import functools

import numpy as np
import jax
import jax.numpy as jnp
from jax import lax
from jax.experimental import pallas as pl
from jax.experimental.pallas import tpu as pltpu

F32 = jnp.float32
BF16 = jnp.bfloat16

D_MODEL = 1024
N_HEADS = 4
HEAD_DIM = 64
GROUP_W = N_HEADS * HEAD_DIM
CONV_K = 4
SWA_Q_HEADS = 8
SWA_KV_HEADS = 2
SWA_GROUP = SWA_Q_HEADS // SWA_KV_HEADS
WINDOW = 128
ROPE_THETA = 500000.0
ROT_DIM = HEAD_DIM // 4
ROT_HALF = ROT_DIM // 2
D_FF = 4 * D_MODEL
PLE_DIM = 256
EPS = 1e-6
LOG2_E = 1.4426950408889634

LANES = 128
SUBLANES = 8
VMEM_LIMIT_BYTES = 56 * 1024 * 1024

GATE_COLS = 2 * N_HEADS
M_COLS = 4 * GROUP_W + LANES
H_COLS = 4 * GROUP_W
S_COLS = SWA_Q_HEADS * HEAD_DIM + 2 * SWA_KV_HEADS * HEAD_DIM
IN_COLS = M_COLS + H_COLS + S_COLS
GATE_OFF = 4 * GROUP_W

IN_TM = 512
TAIL_TM = 512
ROPE_TM = 2048
MLSTM_T = 256
MLSTM_ROWS = 512
HGRN_T = 512
HGRN_L = 64
SWA_T = 512

N_SPLIT = 2
assert HGRN_L == HEAD_DIM


def _params(n_grid_dims, flags=None):
    return pltpu.CompilerParams(
        dimension_semantics=("arbitrary",) * n_grid_dims,
        vmem_limit_bytes=VMEM_LIMIT_BYTES,
        flags=flags)


def _dot(a, b):
    return jnp.dot(a, b, preferred_element_type=F32)


def _dot_nt(a, b):
    return lax.dot_general(a, b, (((1,), (1,)), ((), ())), preferred_element_type=F32)


def _dot_tn(a, b):
    return lax.dot_general(a, b, (((0,), (0,)), ((), ())), preferred_element_type=F32)


def _split(x, n=N_SPLIT):
    pieces = []
    r = x
    for _ in range(n):
        p = r.astype(BF16)
        pieces.append(p)
        r = r - p.astype(F32)
    return pieces


def _dot_const_left(m, x):
    acc = None
    for p in _split(x):
        t = _dot(m, p)
        acc = t if acc is None else acc + t
    return acc


def _dot_const_right(x, m):
    acc = None
    for p in _split(x):
        t = _dot(p, m)
        acc = t if acc is None else acc + t
    return acc


def _sigmoid(x):
    return 1.0 / (1.0 + jnp.exp(-x))


def _log_sigmoid(x):
    return jnp.minimum(x, 0.0) - jnp.log1p(jnp.exp(-jnp.abs(x)))


def _rms(x, g):
    return x * lax.rsqrt(jnp.mean(x * x, axis=-1, keepdims=True) + EPS) * g


def _head_rms(x, g, blk_ones):
    ms = _dot(jnp.square(x).astype(BF16), blk_ones) * (1.0 / HEAD_DIM)
    return x * lax.rsqrt(ms + EPS) * g


def _lane_head_id(shape):
    return lax.broadcasted_iota(jnp.int32, shape, len(shape) - 1) // HEAD_DIM


def _blk_ones(width):
    hid = np.arange(width) // HEAD_DIM
    return jnp.asarray((hid[:, None] == hid[None, :]).astype(np.float32), dtype=BF16)


def _hgrn_level_sizes():
    sizes = []
    c = 2
    while c <= HGRN_L:
        sizes.append(c)
        c *= 2
    return sizes


def _hgrn_constants():
    L = HGRN_L
    t = np.arange(L)[:, None]
    u = np.arange(L)[None, :]
    masks = [(t == u)]
    for c in _hgrn_level_sizes():
        mid = (t // c) * c + c // 2 - 1
        same_blk = (t // c) == (u // c)
        u_mid = (u // c) * c + c // 2 - 1
        masks.append(same_blk & (t > mid) & (u <= u_mid))
    tril = (u <= t).astype(np.float32)
    pmask = np.stack([np.tile(m.astype(np.float32), (1, N_HEADS)) for m in masks])
    return jnp.asarray(tril, dtype=BF16), jnp.asarray(pmask, dtype=F32)


def _rope_kernel(pos_ref, pat_ref, cos_ref, sin_ref):
    ang = pos_ref[...].astype(F32) * pat_ref[0:1, :]
    cos_ref[...] = jnp.cos(ang)
    sin_ref[...] = jnp.sin(ang) * pat_ref[1:2, :]


def _rope_tables(positions):
    n = positions.size
    inv_freq = ROPE_THETA ** (-jnp.arange(0, ROT_DIM, 2, dtype=F32) / ROT_DIM)
    lane = np.arange(LANES) % HEAD_DIM
    freq = jnp.where(lane < ROT_DIM, inv_freq[lane % ROT_HALF], 0.0)
    sign = np.where(lane < ROT_HALF, -1.0, np.where(lane < ROT_DIM, 1.0, 0.0)).astype(np.float32)
    pat = jnp.zeros((SUBLANES, LANES), F32).at[0].set(freq).at[1].set(sign)
    tm = min(ROPE_TM, n)
    out = jax.ShapeDtypeStruct((n, LANES), F32)
    return pl.pallas_call(
        _rope_kernel,
        grid=(n // tm,),
        in_specs=[pl.BlockSpec((tm, 1), lambda i: (i, 0)),
                  pl.BlockSpec((SUBLANES, LANES), lambda i: (0, 0))],
        out_specs=[pl.BlockSpec((tm, LANES), lambda i: (i, 0))] * 2,
        out_shape=[out, out],
        compiler_params=_params(1),
        name="rope_tables",
    )(positions.reshape(n, 1), pat)


def _lb_kernel(lg_ref, la_ref, l1m_ref, oml_ref):
    z = lg_ref[...]
    e = jnp.exp(z - jnp.max(z, axis=0, keepdims=True))
    sm = e / jnp.sum(e, axis=0, keepdims=True)
    depth = z.shape[0]
    run = sm[0:1, :]
    first = run
    for l in range(depth):
        if l > 0:
            run = run + sm[l:l + 1, :]
        lb = run - first
        la_ref[l:l + 1, :] = jnp.log(lb)
        l1m_ref[l:l + 1, :] = jnp.log1p(-lb)
        oml_ref[l:l + 1, :] = 1.0 - lb


def _hgrn_lower_bounds(logits):
    out = jax.ShapeDtypeStruct(logits.shape, F32)
    return pl.pallas_call(
        _lb_kernel, out_shape=[out, out, out], name="hgrn_lower_bounds",
    )(logits.astype(F32))


def _in_proj_kernel(x_ref, g_ref, w_ref, b_ref, cw_ref, cb_ref, la_ref, l1m_ref, oml_ref,
                    cos_ref, sin_ref, qg_ref, kg_ref, ones_ref,
                    ml_ref, gt_ref, hb_ref, lf_ref, sw_ref, xbuf, *, tiles_per_seq):
    TM = x_ref.shape[0]
    W = GROUP_W

    @pl.when(pl.program_id(0) % tiles_per_seq == 0)
    def _():
        xbuf[0:SUBLANES, :] = jnp.zeros((SUBLANES, 2 * W), F32)

    h = _rms(x_ref[...], g_ref[...]).astype(BF16)

    def proj(lo, width):
        return _dot(h, w_ref[:, lo:lo + width]) + b_ref[:, lo:lo + width]

    cos_t = cos_ref[...]
    sin_t = sin_ref[...]
    blk = ones_ref[...]
    lane = lax.broadcasted_iota(jnp.int32, (TM, LANES), 1)
    low_half = lane < HEAD_DIM
    rot_first = (lane % HEAD_DIM) < ROT_HALF
    s0 = M_COLS + H_COLS
    QW = SWA_Q_HEADS * HEAD_DIM

    def head_sumsq(y):
        width = y.shape[-1]
        return _dot(jnp.square(y).astype(BF16), blk[0:width, 0:width])

    def norm_rope(v, sumsq, gain):
        xn = v * lax.rsqrt(sumsq * (1.0 / HEAD_DIM) + EPS) * gain
        partner = jnp.where(rot_first, pltpu.roll(xn, LANES - ROT_HALF, 1), pltpu.roll(xn, ROT_HALF, 1))
        return xn * cos_t + partner * sin_t

    def swa_q(c, y, sumsq):
        for half in range(2):
            sl = slice(half * LANES, (half + 1) * LANES)
            sw_ref[:, (c + half) * LANES:(c + half + 1) * LANES] = (
                norm_rope(y[:, sl], sumsq[:, sl], qg_ref[...]) * (HEAD_DIM ** -0.5)).astype(BF16)

    def swa_kv(y, sumsq):
        k = norm_rope(y[:, 0:LANES], sumsq, kg_ref[...])
        v = y[:, LANES:2 * LANES]
        k_sw = pltpu.roll(k, HEAD_DIM, 1)
        v_sw = pltpu.roll(v, HEAD_DIM, 1)
        sw_ref[:, QW:QW + LANES] = jnp.where(low_half, k, k_sw).astype(BF16)
        sw_ref[:, QW + LANES:QW + 2 * LANES] = jnp.where(low_half, k_sw, k).astype(BF16)
        sw_ref[:, QW + 2 * LANES:QW + 3 * LANES] = jnp.where(low_half, v, v_sw).astype(BF16)
        sw_ref[:, QW + 3 * LANES:QW + 4 * LANES] = jnp.where(low_half, v_sw, v).astype(BF16)

    def mlstm_qk(y):
        xbuf[SUBLANES:SUBLANES + TM, :] = y
        acc = cb_ref[...] + cw_ref[CONV_K - 1:CONV_K, :] * xbuf[SUBLANES:SUBLANES + TM, :]
        for j in range(1, CONV_K):
            acc = acc + cw_ref[CONV_K - 1 - j:CONV_K - j, :] * xbuf[SUBLANES - j:SUBLANES - j + TM, :]
        xbuf[0:SUBLANES, :] = xbuf[TM:TM + SUBLANES, :]
        qk = acc * _sigmoid(acc)
        ml_ref[:, 0:W] = qk[:, 0:W].astype(BF16)
        ml_ref[:, W:2 * W] = (qk[:, W:2 * W] * (HEAD_DIM ** -0.5)).astype(BF16)

    def hgrn_qf(y):
        hq = y[:, 0:W]
        hf = y[:, W:2 * W]
        la = la_ref[...]
        u = jnp.exp(-jnp.abs(hf))
        t = l1m_ref[...] + (jnp.minimum(hf, 0.0) - jnp.log1p(u))
        lf_ref[...] = jnp.maximum(la, t) + jnp.log1p(jnp.exp(-jnp.abs(la - t)))
        hb_ref[:, 0:W] = (hq * _sigmoid(hq)).astype(BF16)
        hb_ref[:, W:2 * W] = (oml_ref[...] * jnp.where(hf >= 0.0, u, 1.0) / (1.0 + u)).astype(BF16)

    y_kv = proj(s0 + QW, 2 * LANES)
    y_q0 = proj(s0, 2 * LANES)
    y_q1 = proj(s0 + 2 * LANES, 2 * LANES)
    swa_kv(y_kv, head_sumsq(y_kv[:, 0:LANES]))
    y_h = proj(M_COLS, 2 * W)
    swa_q(0, y_q0, head_sumsq(y_q0))
    y_m = proj(0, 2 * W)
    swa_q(2, y_q1, head_sumsq(y_q1))
    hgrn_qf(y_h)
    mlstm_qk(y_m)
    y = proj(M_COLS + 2 * W, 2 * W)
    hg = y[:, W:2 * W]
    hb_ref[:, 2 * W:3 * W] = y[:, 0:W].astype(BF16)
    hb_ref[:, 3 * W:4 * W] = (hg * _sigmoid(hg)).astype(BF16)
    gt_ref[...] = proj(GATE_OFF, LANES)
    y = proj(2 * W, 2 * W)
    ml_ref[:, 2 * W:3 * W] = y[:, 0:W].astype(BF16)
    ml_ref[:, 3 * W:4 * W] = _sigmoid(y[:, W:2 * W]).astype(BF16)


def _in_proj(x, g, w, b, conv_w, conv_b, la, l1m, oml, cos_t, sin_t, q_gain, k_gain, blk_ones,
             layer, seq):
    n = x.shape[0]
    tm = min(IN_TM, n)
    row = lambda i: (i, 0)
    lay = lambda i: (layer, 0, 0)
    W4 = 4 * GROUP_W
    return pl.pallas_call(
        functools.partial(_in_proj_kernel, tiles_per_seq=seq // tm),
        grid=(n // tm,),
        in_specs=[pl.BlockSpec((tm, D_MODEL), row),
                  pl.BlockSpec((None, 1, D_MODEL), lay),
                  pl.BlockSpec((None, D_MODEL, IN_COLS), lay),
                  pl.BlockSpec((None, 1, IN_COLS), lay),
                  pl.BlockSpec((None, CONV_K, 2 * GROUP_W), lay),
                  pl.BlockSpec((None, 1, 2 * GROUP_W), lay),
                  pl.BlockSpec((None, 1, GROUP_W), lay),
                  pl.BlockSpec((None, 1, GROUP_W), lay),
                  pl.BlockSpec((None, 1, GROUP_W), lay),
                  pl.BlockSpec((tm, LANES), row),
                  pl.BlockSpec((tm, LANES), row),
                  pl.BlockSpec((None, 1, LANES), lay),
                  pl.BlockSpec((None, 1, LANES), lay),
                  pl.BlockSpec((GROUP_W, GROUP_W), lambda i: (0, 0))],
        out_specs=[pl.BlockSpec((tm, W4), row),
                   pl.BlockSpec((tm, LANES), row),
                   pl.BlockSpec((tm, W4), row),
                   pl.BlockSpec((tm, GROUP_W), row),
                   pl.BlockSpec((tm, W4), row)],
        out_shape=[jax.ShapeDtypeStruct((n, W4), BF16),
                   jax.ShapeDtypeStruct((n, LANES), F32),
                   jax.ShapeDtypeStruct((n, W4), BF16),
                   jax.ShapeDtypeStruct((n, GROUP_W), F32),
                   jax.ShapeDtypeStruct((n, W4), BF16)],
        scratch_shapes=[pltpu.VMEM((tm + SUBLANES, 2 * GROUP_W), F32)],
        compiler_params=_params(1),
        name="in_proj",
    )(x, g, w, b, conv_w, conv_b, la, l1m, oml, cos_t, sin_t, q_gain, k_gain, blk_ones)


def _mlstm_kernel(y_ref, gt_ref, fb_ref, g_ref, ones_ref, tril_ref, exp_ref, o_ref,
                  c_st, n_st, m_st):
    @pl.when(pl.program_id(1) == 0)
    def _():
        c_st[...] = jnp.zeros_like(c_st)
        n_st[...] = jnp.zeros_like(n_st)
        m_st[...] = jnp.zeros_like(m_st)

    for ci in range(MLSTM_ROWS // MLSTM_T):
        _mlstm_chunk(ci * MLSTM_T, y_ref, gt_ref, fb_ref, g_ref, ones_ref, tril_ref, exp_ref, o_ref,
                     c_st, n_st, m_st)


def _mlstm_chunk(r0, y_ref, gt_ref, fb_ref, g_ref, ones_ref, tril_ref, exp_ref, o_ref,
                 c_st, n_st, m_st):
    T = MLSTM_T
    W = GROUP_W
    qb = y_ref[r0:r0 + T, 0:W]
    kb = y_ref[r0:r0 + T, W:2 * W]
    vb = y_ref[r0:r0 + T, 2 * W:3 * W]
    out_gate = y_ref[r0:r0 + T, 3 * W:4 * W]

    gates = gt_ref[r0:r0 + T, :]
    i_pre = gates
    lf = pltpu.roll(_log_sigmoid(gates + fb_ref[...]), LANES - N_HEADS, 1)
    lf_hi = lf.astype(BF16)
    lf_lo = (lf - lf_hi.astype(F32)).astype(BF16)
    bb = _dot(tril_ref[...], jnp.concatenate([lf_hi, lf_lo], axis=1))
    b = bb[:, 0:LANES] + bb[:, LANES:2 * LANES]
    a = i_pre - b
    row = lax.broadcasted_iota(jnp.int32, (T, LANES), 0)
    cm = a
    sh = 1
    while sh < T:
        cm = jnp.maximum(cm, jnp.where(row >= sh, pltpu.roll(cm, sh, 0), -jnp.inf))
        sh *= 2
    m_prev = m_st[0:1, :]
    m_j = b + jnp.maximum(m_prev, cm)
    m_new = m_j[T - 1:T, :]
    b_last = b[T - 1:T, :]
    w_inter = jnp.exp(b + m_prev - m_j)
    e_den = jnp.exp(-m_j)
    w_s = jnp.exp(b_last + a - m_new)
    decay = jnp.exp(b_last + m_prev - m_new)
    cexp = b - m_j
    a_t = a.T

    stacked = jnp.concatenate([w_inter, e_den, w_s], axis=0)
    head_lane = lax.broadcasted_iota(jnp.int32, stacked.shape, 1) < N_HEADS
    ex = _dot(jnp.where(head_lane, stacked, 0.0).astype(BF16), exp_ref[...])
    w_inter_x = ex[0:T]
    e_den_x = ex[T:2 * T]
    w_s_x = ex[2 * T:3 * T]
    decay8 = jnp.where(head_lane[0:SUBLANES], jnp.broadcast_to(decay, (SUBLANES, LANES)), 0.0)
    decay_x = _dot_const_right(decay8, exp_ref[...])[0:1]

    blk = ones_ref[...]
    hid = _lane_head_id((T, W))
    causal = (lax.broadcasted_iota(jnp.int32, (T, T), 1)
              <= lax.broadcasted_iota(jnp.int32, (T, T), 0))
    head_rows = [blk[h * HEAD_DIM:h * HEAD_DIM + 1, :] for h in range(N_HEADS)]
    s_all = _dot_nt(jnp.concatenate([qb * hr for hr in head_rows], axis=0), kb)
    num = None
    den = None
    for h in range(N_HEADS):
        dm = cexp[:, h:h + 1] + a_t[h:h + 1, :]
        s = s_all[h * T:(h + 1) * T] * jnp.exp(jnp.where(causal, dm, -jnp.inf))
        dn = _dot(s.astype(BF16), vb * head_rows[h])
        dd = jnp.where(hid == h, jnp.sum(s, axis=-1, keepdims=True), 0.0)
        num = dn if num is None else num + dn
        den = dd if den is None else den + dd
    n_row = n_st[0:1, :].astype(BF16)
    num = num + w_inter_x * _dot(qb, c_st[...].astype(BF16))
    den = den + w_inter_x * _dot(qb * n_row, blk)
    hh = num / jnp.maximum(jnp.abs(den), e_den_x)

    o_ref[r0:r0 + T, :] = _head_rms(hh, g_ref[...], blk) * out_gate.astype(F32)

    kw = kb * w_s_x.astype(BF16)
    c_st[...] = decay_x * c_st[...] + blk.astype(F32) * _dot_tn(kw, vb)
    n_st[...] = decay_x * n_st[...] + _dot(jnp.ones((SUBLANES, T), BF16), kw)
    m_st[...] = jnp.broadcast_to(m_new, m_st.shape)


def _mlstm(ml, gt, f_bias, norm_g, layer, batch, consts):
    n = ml.shape[0]
    seq = n // batch
    T = MLSTM_ROWS
    steps = seq // T
    row = lambda bi, ci: (bi * steps + ci, 0)
    lay = lambda bi, ci: (layer, 0, 0)
    cst = lambda bi, ci: (0, 0)
    blk_ones, tril, expand = consts
    return pl.pallas_call(
        _mlstm_kernel,
        grid=(batch, steps),
        in_specs=[pl.BlockSpec((T, 4 * GROUP_W), row),
                  pl.BlockSpec((T, LANES), row),
                  pl.BlockSpec((None, 1, LANES), lay),
                  pl.BlockSpec((None, 1, GROUP_W), lay),
                  pl.BlockSpec((GROUP_W, GROUP_W), cst),
                  pl.BlockSpec((MLSTM_T, MLSTM_T), cst),
                  pl.BlockSpec((LANES, GROUP_W), cst)],
        out_specs=pl.BlockSpec((T, GROUP_W), row),
        out_shape=jax.ShapeDtypeStruct((n, GROUP_W), F32),
        scratch_shapes=[pltpu.VMEM((GROUP_W, GROUP_W), F32),
                        pltpu.VMEM((SUBLANES, GROUP_W), F32),
                        pltpu.VMEM((SUBLANES, LANES), F32)],
        compiler_params=_params(2),
        name="mlstm",
    )(ml, gt, f_bias, norm_g, blk_ones, tril, expand)


def _hgrn_kernel(y_ref, lf_ref, g_ref, ones_ref, mst_ref, pm_ref, o_ref, s_st):
    L = HGRN_L
    W = GROUP_W
    step = pl.program_id(1)

    @pl.when(step == 0)
    def _():
        s_st[...] = jnp.zeros_like(s_st)

    gain = g_ref[...]
    blk = ones_ref[...]
    blk_f = blk.astype(F32)
    tril = mst_ref[...]
    row = lax.broadcasted_iota(jnp.int32, (L, W), 0)

    def level_gap(g, lf2, c):
        if c == 2:
            return jnp.where(row % 2 == 1, lf2, 0.0)
        if c >= SUBLANES:
            mids = [b * c + c // 2 - 1 for b in range(L // c)]
            return g - jnp.concatenate(
                [jnp.broadcast_to(g[m:m + 1, :], (c, W)) for m in mids], axis=0)
        lo = jnp.concatenate([jnp.broadcast_to(g[m:m + 1, :], (SUBLANES, W))
                              for m in range(1, L, SUBLANES)], axis=0)
        hi = jnp.concatenate([jnp.broadcast_to(g[m:m + 1, :], (SUBLANES, W))
                              for m in range(5, L, SUBLANES)], axis=0)
        return g - jnp.where(row % SUBLANES < 4, lo, hi)

    def head_stack(xb):
        return jnp.concatenate([xb] * N_HEADS, axis=0) * blk

    for ci in range(HGRN_T // L):
        r0 = ci * L
        qf = y_ref[r0:r0 + L, 0:W]
        key = y_ref[r0:r0 + L, W:2 * W]
        vb = y_ref[r0:r0 + L, 2 * W:3 * W]
        out_gate = y_ref[r0:r0 + L, 3 * W:4 * W]

        lf2 = lf_ref[r0:r0 + L, :] * LOG2_E
        g = _dot_const_left(tril, lf2)
        e_last = jnp.exp2(g[L - 1:L, :])
        q_hat = qf * jnp.exp2(g).astype(BF16)
        k_hat = key * jnp.exp2(g[L - 1:L, :] - g).astype(BF16)

        a = pm_ref[0] * _dot_nt(qf, head_stack(key))
        for i, c in enumerate(_hgrn_level_sizes()):
            e_i = jnp.exp2(-jnp.abs(level_gap(g, lf2, c))).astype(BF16)
            a = a + pm_ref[1 + i] * _dot_nt(qf * e_i, head_stack(key * e_i))
        o = _dot(a.astype(BF16), head_stack(vb)) + _dot_nt(q_hat, s_st[...].astype(BF16))

        o_ref[r0:r0 + L, :] = _head_rms(o, gain, blk) * out_gate.astype(F32)
        s_st[...] = s_st[...] * e_last + blk_f * _dot_tn(vb, k_hat)


def _hgrn(hb, lf, norm_g, layer, batch, consts):
    n = hb.shape[0]
    seq = n // batch
    T = HGRN_T
    steps = seq // T
    row = lambda bi, ci: (bi * steps + ci, 0)
    lay = lambda bi, ci: (layer, 0, 0)
    blk_ones, mstack, pmask = consts
    return pl.pallas_call(
        _hgrn_kernel,
        grid=(batch, steps),
        in_specs=[pl.BlockSpec((T, H_COLS), row),
                  pl.BlockSpec((T, GROUP_W), row),
                  pl.BlockSpec((None, 1, GROUP_W), lay),
                  pl.BlockSpec((GROUP_W, GROUP_W), lambda bi, ci: (0, 0)),
                  pl.BlockSpec(mstack.shape, lambda bi, ci: (0, 0)),
                  pl.BlockSpec(pmask.shape, lambda bi, ci: (0, 0, 0))],
        out_specs=pl.BlockSpec((T, GROUP_W), row),
        out_shape=jax.ShapeDtypeStruct((n, GROUP_W), F32),
        scratch_shapes=[pltpu.VMEM((GROUP_W, GROUP_W), F32)],
        compiler_params=_params(2),
        name="hgrn2",
    )(hb, lf, norm_g, blk_ones, mstack, pmask)


def _swa_kernel(sink_ref, y_ref, o_ref, k_prev, v_prev, *, layer):
    T = SWA_T
    Wn = WINDOW
    QW = SWA_Q_HEADS * HEAD_DIM
    step = pl.program_id(1)

    @pl.when(step == 0)
    def _():
        k_prev[...] = jnp.zeros_like(k_prev)
        v_prev[...] = jnp.zeros_like(v_prev)

    k2 = [jnp.concatenate([k_prev[g], y_ref[:, QW + g * LANES:QW + (g + 1) * LANES]], axis=0)
          for g in range(SWA_KV_HEADS)]
    v2 = [jnp.concatenate([v_prev[g], y_ref[:, QW + (2 + g) * LANES:QW + (3 + g) * LANES]], axis=0)
          for g in range(SWA_KV_HEADS)]
    for g in range(SWA_KV_HEADS):
        k_prev[g] = k2[g][T:T + Wn]
        v_prev[g] = v2[g][T:T + Wn]

    qi = lax.broadcasted_iota(jnp.int32, (Wn, 2 * Wn), 0)
    ki = lax.broadcasted_iota(jnp.int32, (Wn, 2 * Wn), 1)
    band = (ki > qi) & (ki <= qi + Wn)
    band_first = band & (ki >= jnp.where(step > 0, 0, Wn))
    lane_w = lax.broadcasted_iota(jnp.int32, (Wn, LANES), 1) < HEAD_DIM
    lane_row = lax.broadcasted_iota(jnp.int32, (1, LANES), 1) < HEAD_DIM
    low_row = jnp.where(lane_row, 1.0, 0.0).astype(BF16)
    high_row = jnp.where(lane_row, 0.0, 1.0).astype(BF16)

    for j in range(T // Wn):
        r0 = j * Wn
        mask = band_first if j == 0 else band
        for g in range(SWA_KV_HEADS):
            kc = k2[g][r0:r0 + 2 * Wn]
            vc = v2[g][r0:r0 + 2 * Wn]
            parts = []
            for c in (2 * g, 2 * g + 1):
                qc = y_ref[r0:r0 + Wn, c * LANES:(c + 1) * LANES]
                parts.append(qc * low_row)
                parts.append(qc * high_row)
            s = _dot_nt(jnp.concatenate(parts, axis=0), kc)
            outs = []
            for r in range(SWA_GROUP):
                sink = sink_ref[layer, SWA_GROUP * g + r]
                sr = jnp.where(mask, s[r * Wn:(r + 1) * Wn], -jnp.inf)
                m = jnp.maximum(jnp.max(sr, axis=-1, keepdims=True), sink)
                pexp = jnp.exp(sr - m)
                den = jnp.sum(pexp, axis=-1, keepdims=True) + jnp.exp(sink - m)
                outs.append(_dot(pexp.astype(BF16), vc) / den)
            c0 = 2 * g * LANES
            o_ref[r0:r0 + Wn, c0:c0 + LANES] = jnp.where(lane_w, outs[0], outs[1])
            o_ref[r0:r0 + Wn, c0 + LANES:c0 + 2 * LANES] = jnp.where(lane_w, outs[2], outs[3])


def _swa(sw, sinks, layer, batch):
    n = sw.shape[0]
    seq = n // batch
    T = SWA_T
    steps = seq // T
    QW = SWA_Q_HEADS * HEAD_DIM
    row = lambda bi, ci: (bi * steps + ci, 0)
    return pl.pallas_call(
        functools.partial(_swa_kernel, layer=layer),
        grid=(batch, steps),
        in_specs=[pl.BlockSpec(memory_space=pltpu.SMEM),
                  pl.BlockSpec((T, 4 * GROUP_W), row)],
        out_specs=pl.BlockSpec((T, QW), row),
        out_shape=jax.ShapeDtypeStruct((n, QW), F32),
        scratch_shapes=[pltpu.VMEM((SWA_KV_HEADS, WINDOW, LANES), BF16),
                        pltpu.VMEM((SWA_KV_HEADS, WINDOW, LANES), BF16)],
        compiler_params=_params(2),
        name="swa",
    )(sinks, sw)


def _tail_kernel(x_ref, mm_ref, mh_ref, ms_ref, p_ref, wo_ref, g2_ref, wu_ref, wd_ref, g3_ref,
                 wg_ref, wp_ref, g4_ref, o_ref):
    W = GROUP_W
    mix = _dot(mm_ref[...].astype(BF16), wo_ref[0:W, :])
    mix = mix + _dot(mh_ref[...].astype(BF16), wo_ref[W:2 * W, :])
    mix = mix + _dot(ms_ref[...].astype(BF16), wo_ref[2 * W:4 * W, :])
    x = x_ref[...] + mix

    h2 = _rms(x, g2_ref[...]).astype(BF16)
    mlp = None
    for c in range(D_FF // D_MODEL):
        u = _dot(h2, wu_ref[:, c * D_MODEL:(c + 1) * D_MODEL])
        act = jnp.square(jnp.maximum(u, 0.0)).astype(BF16)
        d = _dot(act, wd_ref[c * D_MODEL:(c + 1) * D_MODEL, :])
        mlp = d if mlp is None else mlp + d
    x = x + mlp

    gate = _sigmoid(_dot(_rms(x, g3_ref[...]).astype(BF16), wg_ref[...]))
    emb = _rms(_dot(p_ref[...].astype(BF16), wp_ref[...]), g4_ref[...])
    o_ref[...] = x + gate * emb


def _tail(x, mm, mh, ms, p, w_out, g2, w_up, w_down, g3, w_gate, w_proj, g4, layer):
    n = x.shape[0]
    tm = min(TAIL_TM, n)
    tiles = n // tm
    row = lambda i: (i, 0)
    lay = lambda i: (layer, 0, 0)
    once = pl.Buffered(1)

    def wspec(shape):
        return pl.BlockSpec((None,) + shape, lay, pipeline_mode=once)

    return pl.pallas_call(
        _tail_kernel,
        grid=(tiles,),
        in_specs=[pl.BlockSpec((tm, D_MODEL), row),
                  pl.BlockSpec((tm, GROUP_W), row),
                  pl.BlockSpec((tm, GROUP_W), row),
                  pl.BlockSpec((tm, 2 * GROUP_W), row),
                  pl.BlockSpec((None, tm, PLE_DIM), lambda i: (layer, i, 0)),
                  wspec((D_MODEL, D_MODEL)),
                  wspec((1, D_MODEL)),
                  wspec((D_MODEL, D_FF)),
                  wspec((D_FF, D_MODEL)),
                  wspec((1, D_MODEL)),
                  wspec((D_MODEL, D_MODEL)),
                  wspec((PLE_DIM, D_MODEL)),
                  wspec((1, D_MODEL))],
        out_specs=pl.BlockSpec((tm, D_MODEL), row),
        out_shape=jax.ShapeDtypeStruct((n, D_MODEL), F32),
        compiler_params=_params(1),
        name="tail",
    )(x, mm, mh, ms, p, w_out, g2, w_up, w_down, g3, w_gate, w_proj, g4)


def kernel(x, p, positions, in_norm_g, w_in, b_in, mlstm_f_bias, mlstm_conv_w, mlstm_conv_b,
           mlstm_norm_g, hgrn_lb_logits, hgrn_norm_g, swa_q_norm_g, swa_k_norm_g, swa_sinks,
           w_out, mlp_norm_g, w_up, w_down, ple_norm_g, w_ple_gate, w_ple_proj, ple_post_norm_g):
    batch, seq, d_model = x.shape
    depth = w_in.shape[0]
    n = batch * seq
    assert d_model == D_MODEL and seq % max(MLSTM_ROWS, HGRN_T, SWA_T, IN_TM) == 0

    n_raw = GATE_OFF + GATE_COLS
    pad = LANES - GATE_COLS
    w_in_p = jnp.concatenate(
        [w_in[..., :n_raw], jnp.zeros((depth, D_MODEL, pad), w_in.dtype), w_in[..., n_raw:]],
        axis=-1).astype(BF16)
    b_in_p = jnp.concatenate(
        [b_in[..., :n_raw], jnp.zeros((depth, pad), b_in.dtype), b_in[..., n_raw:]],
        axis=-1).astype(F32).reshape(depth, 1, IN_COLS)
    f_bias = jnp.zeros((depth, 1, LANES), F32).at[:, 0, N_HEADS:GATE_COLS].set(mlstm_f_bias.astype(F32))
    row3 = lambda a: a.astype(F32).reshape(depth, 1, a.shape[-1])
    q_gain = row3(jnp.tile(swa_q_norm_g, (1, LANES // HEAD_DIM)))
    k_gain = row3(jnp.tile(swa_k_norm_g, (1, LANES // HEAD_DIM)))
    w_out_b, w_up_b, w_down_b = w_out.astype(BF16), w_up.astype(BF16), w_down.astype(BF16)
    w_gate_b, w_proj_b = w_ple_gate.astype(BF16), w_ple_proj.astype(BF16)

    blk256 = _blk_ones(GROUP_W)
    tril = jnp.asarray(np.tril(np.ones((MLSTM_T, MLSTM_T), np.float32)), dtype=BF16)
    expand = jnp.asarray(
        (np.arange(LANES)[:, None] == (np.arange(GROUP_W) // HEAD_DIM)[None, :]).astype(np.float32),
        dtype=BF16)
    hgrn_mstack, hgrn_pmask = _hgrn_constants()

    cos_t, sin_t = _rope_tables(positions)
    la, l1m, oml = [row3(a) for a in _hgrn_lower_bounds(hgrn_lb_logits)]
    sinks = swa_sinks.astype(F32)

    xf = x.reshape(n, D_MODEL)
    for l in range(depth):
        ml, gt, hb, lf, sw = _in_proj(
            xf, row3(in_norm_g), w_in_p, b_in_p, mlstm_conv_w.astype(F32), row3(mlstm_conv_b),
            la, l1m, oml, cos_t, sin_t, q_gain, k_gain, blk256, l, seq)
        mm = _mlstm(ml, gt, f_bias, row3(mlstm_norm_g), l, batch, (blk256, tril, expand))
        mh = _hgrn(hb, lf, row3(hgrn_norm_g), l, batch, (blk256, hgrn_mstack, hgrn_pmask))
        ms = _swa(sw, sinks, l, batch)
        xf = _tail(xf, mm, mh, ms, p.reshape(depth, n, PLE_DIM), w_out_b, row3(mlp_norm_g), w_up_b,
                   w_down_b, row3(ple_norm_g), w_gate_b, w_proj_b, row3(ple_post_norm_g), l)
    return xf.reshape(batch, seq, D_MODEL)
```

```python
import functools

import numpy as np
import jax
import jax.numpy as jnp
from jax import lax
from jax.experimental import pallas as pl
from jax.experimental.pallas import tpu as pltpu

F32 = jnp.float32
BF16 = jnp.bfloat16

D_MODEL = 1024
N_HEADS = 4
HEAD_DIM = 64
GROUP_W = N_HEADS * HEAD_DIM
CONV_K = 4
SWA_Q_HEADS = 8
SWA_KV_HEADS = 2
SWA_GROUP = SWA_Q_HEADS // SWA_KV_HEADS
WINDOW = 128
ROPE_THETA = 500000.0
ROT_DIM = HEAD_DIM // 4
ROT_HALF = ROT_DIM // 2
D_FF = 4 * D_MODEL
PLE_DIM = 256
EPS = 1e-6
LOG2_E = 1.4426950408889634

LANES = 128
SUBLANES = 8
VMEM_LIMIT_BYTES = 56 * 1024 * 1024

GATE_COLS = 2 * N_HEADS
M_COLS = 4 * GROUP_W + LANES
H_COLS = 4 * GROUP_W
S_COLS = SWA_Q_HEADS * HEAD_DIM + 2 * SWA_KV_HEADS * HEAD_DIM
IN_COLS = M_COLS + H_COLS + S_COLS
GATE_OFF = 4 * GROUP_W

IN_TM = 512
TAIL_TM = 512
ROPE_TM = 2048
MIX_ROWS = 512
MLSTM_T = 256
HGRN_L = 64
SWA_T = MIX_ROWS

N_SPLIT = 2
assert HGRN_L == HEAD_DIM


def _params(n_grid_dims, flags=None):
    return pltpu.CompilerParams(
        dimension_semantics=("arbitrary",) * n_grid_dims,
        vmem_limit_bytes=VMEM_LIMIT_BYTES,
        flags=flags)


def _dot(a, b):
    return jnp.dot(a, b, preferred_element_type=F32)


def _dot_nt(a, b):
    return lax.dot_general(a, b, (((1,), (1,)), ((), ())), preferred_element_type=F32)


def _dot_tn(a, b):
    return lax.dot_general(a, b, (((0,), (0,)), ((), ())), preferred_element_type=F32)


def _split(x, n=N_SPLIT):
    pieces = []
    r = x
    for _ in range(n):
        p = r.astype(BF16)
        pieces.append(p)
        r = r - p.astype(F32)
    return pieces


def _dot_const_left(m, x):
    acc = None
    for p in _split(x):
        t = _dot(m, p)
        acc = t if acc is None else acc + t
    return acc


def _dot_const_right(x, m):
    acc = None
    for p in _split(x):
        t = _dot(p, m)
        acc = t if acc is None else acc + t
    return acc


def _sigmoid(x):
    return 1.0 / (1.0 + jnp.exp(-x))


def _log_sigmoid(x):
    return jnp.minimum(x, 0.0) - jnp.log1p(jnp.exp(-jnp.abs(x)))


def _rms(x, g):
    return x * lax.rsqrt(jnp.mean(x * x, axis=-1, keepdims=True) + EPS) * g


def _head_rms(x, g, blk_ones):
    ms = _dot(jnp.square(x).astype(BF16), blk_ones) * (1.0 / HEAD_DIM)
    return x * lax.rsqrt(ms + EPS) * g


def _lane_head_id(shape):
    return lax.broadcasted_iota(jnp.int32, shape, len(shape) - 1) // HEAD_DIM


def _blk_ones(width):
    hid = np.arange(width) // HEAD_DIM
    return jnp.asarray((hid[:, None] == hid[None, :]).astype(np.float32), dtype=BF16)


def _hgrn_level_sizes():
    sizes = []
    c = 2
    while c <= HGRN_L:
        sizes.append(c)
        c *= 2
    return sizes


def _hgrn_constants():
    L = HGRN_L
    t = np.arange(L)[:, None]
    u = np.arange(L)[None, :]
    masks = [(t == u)]
    for c in _hgrn_level_sizes():
        mid = (t // c) * c + c // 2 - 1
        same_blk = (t // c) == (u // c)
        u_mid = (u // c) * c + c // 2 - 1
        masks.append(same_blk & (t > mid) & (u <= u_mid))
    tril = (u <= t).astype(np.float32)
    pmask = np.stack([np.tile(m.astype(np.float32), (1, N_HEADS)) for m in masks])
    return jnp.asarray(tril, dtype=BF16), jnp.asarray(pmask, dtype=F32)


def _rope_kernel(pos_ref, pat_ref, cos_ref, sin_ref):
    ang = pos_ref[...].astype(F32) * pat_ref[0:1, :]
    cos_ref[...] = jnp.cos(ang)
    sin_ref[...] = jnp.sin(ang) * pat_ref[1:2, :]


def _rope_tables(positions):
    n = positions.size
    inv_freq = ROPE_THETA ** (-jnp.arange(0, ROT_DIM, 2, dtype=F32) / ROT_DIM)
    lane = np.arange(LANES) % HEAD_DIM
    freq = jnp.where(lane < ROT_DIM, inv_freq[lane % ROT_HALF], 0.0)
    sign = np.where(lane < ROT_HALF, -1.0, np.where(lane < ROT_DIM, 1.0, 0.0)).astype(np.float32)
    pat = jnp.zeros((SUBLANES, LANES), F32).at[0].set(freq).at[1].set(sign)
    tm = min(ROPE_TM, n)
    out = jax.ShapeDtypeStruct((n, LANES), F32)
    return pl.pallas_call(
        _rope_kernel,
        grid=(n // tm,),
        in_specs=[pl.BlockSpec((tm, 1), lambda i: (i, 0)),
                  pl.BlockSpec((SUBLANES, LANES), lambda i: (0, 0))],
        out_specs=[pl.BlockSpec((tm, LANES), lambda i: (i, 0))] * 2,
        out_shape=[out, out],
        compiler_params=_params(1),
        name="rope_tables",
    )(positions.reshape(n, 1), pat)


def _lb_kernel(lg_ref, la_ref, l1m_ref, oml_ref):
    z = lg_ref[...]
    e = jnp.exp(z - jnp.max(z, axis=0, keepdims=True))
    sm = e / jnp.sum(e, axis=0, keepdims=True)
    depth = z.shape[0]
    run = sm[0:1, :]
    first = run
    for l in range(depth):
        if l > 0:
            run = run + sm[l:l + 1, :]
        lb = run - first
        la_ref[l:l + 1, :] = jnp.log(lb)
        l1m_ref[l:l + 1, :] = jnp.log1p(-lb)
        oml_ref[l:l + 1, :] = 1.0 - lb


def _hgrn_lower_bounds(logits):
    out = jax.ShapeDtypeStruct(logits.shape, F32)
    return pl.pallas_call(
        _lb_kernel, out_shape=[out, out, out], name="hgrn_lower_bounds",
    )(logits.astype(F32))


def _in_proj_kernel(x_ref, g_ref, w_ref, b_ref, cw_ref, cb_ref, la_ref, l1m_ref, oml_ref,
                    cos_ref, sin_ref, qg_ref, kg_ref, ones_ref,
                    ml_ref, gt_ref, hb_ref, lf_ref, sw_ref, xbuf, *, tiles_per_seq):
    TM = x_ref.shape[0]
    W = GROUP_W

    @pl.when(pl.program_id(0) % tiles_per_seq == 0)
    def _():
        xbuf[0:SUBLANES, :] = jnp.zeros((SUBLANES, 2 * W), F32)

    h = _rms(x_ref[...], g_ref[...]).astype(BF16)

    def proj(lo, width):
        return _dot(h, w_ref[:, lo:lo + width]) + b_ref[:, lo:lo + width]

    cos_t = cos_ref[...]
    sin_t = sin_ref[...]
    blk = ones_ref[...]
    lane = lax.broadcasted_iota(jnp.int32, (TM, LANES), 1)
    low_half = lane < HEAD_DIM
    rot_first = (lane % HEAD_DIM) < ROT_HALF
    s0 = M_COLS + H_COLS
    QW = SWA_Q_HEADS * HEAD_DIM

    def head_sumsq(y):
        width = y.shape[-1]
        return _dot(jnp.square(y).astype(BF16), blk[0:width, 0:width])

    def norm_rope(v, sumsq, gain):
        xn = v * lax.rsqrt(sumsq * (1.0 / HEAD_DIM) + EPS) * gain
        partner = jnp.where(rot_first, pltpu.roll(xn, LANES - ROT_HALF, 1), pltpu.roll(xn, ROT_HALF, 1))
        return xn * cos_t + partner * sin_t

    def swa_q(c, y, sumsq):
        for half in range(2):
            sl = slice(half * LANES, (half + 1) * LANES)
            sw_ref[:, (c + half) * LANES:(c + half + 1) * LANES] = (
                norm_rope(y[:, sl], sumsq[:, sl], qg_ref[...]) * (HEAD_DIM ** -0.5)).astype(BF16)

    def swa_kv(y, sumsq):
        k = norm_rope(y[:, 0:LANES], sumsq, kg_ref[...])
        v = y[:, LANES:2 * LANES]
        k_sw = pltpu.roll(k, HEAD_DIM, 1)
        v_sw = pltpu.roll(v, HEAD_DIM, 1)
        sw_ref[:, QW:QW + LANES] = jnp.where(low_half, k, k_sw).astype(BF16)
        sw_ref[:, QW + LANES:QW + 2 * LANES] = jnp.where(low_half, k_sw, k).astype(BF16)
        sw_ref[:, QW + 2 * LANES:QW + 3 * LANES] = jnp.where(low_half, v, v_sw).astype(BF16)
        sw_ref[:, QW + 3 * LANES:QW + 4 * LANES] = jnp.where(low_half, v_sw, v).astype(BF16)

    def mlstm_qk(y):
        xbuf[SUBLANES:SUBLANES + TM, :] = y
        acc = cb_ref[...] + cw_ref[CONV_K - 1:CONV_K, :] * xbuf[SUBLANES:SUBLANES + TM, :]
        for j in range(1, CONV_K):
            acc = acc + cw_ref[CONV_K - 1 - j:CONV_K - j, :] * xbuf[SUBLANES - j:SUBLANES - j + TM, :]
        xbuf[0:SUBLANES, :] = xbuf[TM:TM + SUBLANES, :]
        qk = acc * _sigmoid(acc)
        ml_ref[:, 0:W] = qk[:, 0:W].astype(BF16)
        ml_ref[:, W:2 * W] = (qk[:, W:2 * W] * (HEAD_DIM ** -0.5)).astype(BF16)

    def hgrn_qf(y):
        hq = y[:, 0:W]
        hf = y[:, W:2 * W]
        la = la_ref[...]
        u = jnp.exp(-jnp.abs(hf))
        t = l1m_ref[...] + (jnp.minimum(hf, 0.0) - jnp.log1p(u))
        lf_ref[...] = jnp.maximum(la, t) + jnp.log1p(jnp.exp(-jnp.abs(la - t)))
        hb_ref[:, 0:W] = (hq * _sigmoid(hq)).astype(BF16)
        hb_ref[:, W:2 * W] = (oml_ref[...] * jnp.where(hf >= 0.0, u, 1.0) / (1.0 + u)).astype(BF16)

    y_kv = proj(s0 + QW, 2 * LANES)
    y_q0 = proj(s0, 2 * LANES)
    y_q1 = proj(s0 + 2 * LANES, 2 * LANES)
    swa_kv(y_kv, head_sumsq(y_kv[:, 0:LANES]))
    y_h = proj(M_COLS, 2 * W)
    swa_q(0, y_q0, head_sumsq(y_q0))
    y_m = proj(0, 2 * W)
    swa_q(2, y_q1, head_sumsq(y_q1))
    hgrn_qf(y_h)
    mlstm_qk(y_m)
    y = proj(M_COLS + 2 * W, 2 * W)
    hg = y[:, W:2 * W]
    hb_ref[:, 2 * W:3 * W] = y[:, 0:W].astype(BF16)
    hb_ref[:, 3 * W:4 * W] = (hg * _sigmoid(hg)).astype(BF16)
    gt_ref[...] = proj(GATE_OFF, LANES)
    y = proj(2 * W, 2 * W)
    ml_ref[:, 2 * W:3 * W] = y[:, 0:W].astype(BF16)
    ml_ref[:, 3 * W:4 * W] = _sigmoid(y[:, W:2 * W]).astype(BF16)


def _in_proj(x, g, w, b, conv_w, conv_b, la, l1m, oml, cos_t, sin_t, q_gain, k_gain, blk_ones,
             layer, seq):
    n = x.shape[0]
    tm = min(IN_TM, n)
    row = lambda i: (i, 0)
    lay = lambda i: (layer, 0, 0)
    W4 = 4 * GROUP_W
    return pl.pallas_call(
        functools.partial(_in_proj_kernel, tiles_per_seq=seq // tm),
        grid=(n // tm,),
        in_specs=[pl.BlockSpec((tm, D_MODEL), row),
                  pl.BlockSpec((None, 1, D_MODEL), lay),
                  pl.BlockSpec((None, D_MODEL, IN_COLS), lay),
                  pl.BlockSpec((None, 1, IN_COLS), lay),
                  pl.BlockSpec((None, CONV_K, 2 * GROUP_W), lay),
                  pl.BlockSpec((None, 1, 2 * GROUP_W), lay),
                  pl.BlockSpec((None, 1, GROUP_W), lay),
                  pl.BlockSpec((None, 1, GROUP_W), lay),
                  pl.BlockSpec((None, 1, GROUP_W), lay),
                  pl.BlockSpec((tm, LANES), row),
                  pl.BlockSpec((tm, LANES), row),
                  pl.BlockSpec((None, 1, LANES), lay),
                  pl.BlockSpec((None, 1, LANES), lay),
                  pl.BlockSpec((GROUP_W, GROUP_W), lambda i: (0, 0))],
        out_specs=[pl.BlockSpec((tm, W4), row),
                   pl.BlockSpec((tm, LANES), row),
                   pl.BlockSpec((tm, W4), row),
                   pl.BlockSpec((tm, GROUP_W), row),
                   pl.BlockSpec((tm, W4), row)],
        out_shape=[jax.ShapeDtypeStruct((n, W4), BF16),
                   jax.ShapeDtypeStruct((n, LANES), F32),
                   jax.ShapeDtypeStruct((n, W4), BF16),
                   jax.ShapeDtypeStruct((n, GROUP_W), F32),
                   jax.ShapeDtypeStruct((n, W4), BF16)],
        scratch_shapes=[pltpu.VMEM((tm + SUBLANES, 2 * GROUP_W), F32)],
        compiler_params=_params(1),
        name="in_proj",
    )(x, g, w, b, conv_w, conv_b, la, l1m, oml, cos_t, sin_t, q_gain, k_gain, blk_ones)


def _mlstm_chunk(r0, y_ref, gt_ref, fb_ref, g_ref, ones_ref, tril_ref, exp_ref, o_ref,
                 c_st, n_st, m_st):
    T = MLSTM_T
    W = GROUP_W
    qb = y_ref[r0:r0 + T, 0:W]
    kb = y_ref[r0:r0 + T, W:2 * W]
    vb = y_ref[r0:r0 + T, 2 * W:3 * W]
    out_gate = y_ref[r0:r0 + T, 3 * W:4 * W]

    gates = gt_ref[r0:r0 + T, :]
    i_pre = gates
    lf = pltpu.roll(_log_sigmoid(gates + fb_ref[...]), LANES - N_HEADS, 1)
    lf_hi = lf.astype(BF16)
    lf_lo = (lf - lf_hi.astype(F32)).astype(BF16)
    bb = _dot(tril_ref[...], jnp.concatenate([lf_hi, lf_lo], axis=1))
    b = bb[:, 0:LANES] + bb[:, LANES:2 * LANES]
    a = i_pre - b
    row = lax.broadcasted_iota(jnp.int32, (T, LANES), 0)
    cm = a
    sh = 1
    while sh < T:
        cm = jnp.maximum(cm, jnp.where(row >= sh, pltpu.roll(cm, sh, 0), -jnp.inf))
        sh *= 2
    m_prev = m_st[0:1, :]
    m_j = b + jnp.maximum(m_prev, cm)
    m_new = m_j[T - 1:T, :]
    b_last = b[T - 1:T, :]
    w_inter = jnp.exp(b + m_prev - m_j)
    e_den = jnp.exp(-m_j)
    w_s = jnp.exp(b_last + a - m_new)
    decay = jnp.exp(b_last + m_prev - m_new)
    cexp = b - m_j
    a_t = a.T

    stacked = jnp.concatenate([w_inter, e_den, w_s], axis=0)
    head_lane = lax.broadcasted_iota(jnp.int32, stacked.shape, 1) < N_HEADS
    ex = _dot(jnp.where(head_lane, stacked, 0.0).astype(BF16), exp_ref[...])
    w_inter_x = ex[0:T]
    e_den_x = ex[T:2 * T]
    w_s_x = ex[2 * T:3 * T]
    decay8 = jnp.where(head_lane[0:SUBLANES], jnp.broadcast_to(decay, (SUBLANES, LANES)), 0.0)
    decay_x = _dot_const_right(decay8, exp_ref[...])[0:1]

    blk = ones_ref[...]
    hid = _lane_head_id((T, W))
    causal = (lax.broadcasted_iota(jnp.int32, (T, T), 1)
              <= lax.broadcasted_iota(jnp.int32, (T, T), 0))
    head_rows = [blk[h * HEAD_DIM:h * HEAD_DIM + 1, :] for h in range(N_HEADS)]
    s_all = _dot_nt(jnp.concatenate([qb * hr for hr in head_rows], axis=0), kb)
    num = None
    den = None
    for h in range(N_HEADS):
        dm = cexp[:, h:h + 1] + a_t[h:h + 1, :]
        s = s_all[h * T:(h + 1) * T] * jnp.exp(jnp.where(causal, dm, -jnp.inf))
        dn = _dot(s.astype(BF16), vb * head_rows[h])
        dd = jnp.where(hid == h, jnp.sum(s, axis=-1, keepdims=True), 0.0)
        num = dn if num is None else num + dn
        den = dd if den is None else den + dd
    n_row = n_st[0:1, :].astype(BF16)
    num = num + w_inter_x * _dot(qb, c_st[...].astype(BF16))
    den = den + w_inter_x * _dot(qb * n_row, blk)
    hh = num / jnp.maximum(jnp.abs(den), e_den_x)

    o_ref[r0:r0 + T, :] = _head_rms(hh, g_ref[...], blk) * out_gate.astype(F32)

    kw = kb * w_s_x.astype(BF16)
    c_st[...] = decay_x * c_st[...] + blk.astype(F32) * _dot_tn(kw, vb)
    n_st[...] = decay_x * n_st[...] + _dot(jnp.ones((SUBLANES, T), BF16), kw)
    m_st[...] = jnp.broadcast_to(m_new, m_st.shape)


def _hgrn_chunk(r0, y_ref, lf_ref, g_ref, ones_ref, mst_ref, pm_ref, o_ref, s_st):
    L = HGRN_L
    W = GROUP_W
    gain = g_ref[...]
    blk = ones_ref[...]
    blk_f = blk.astype(F32)
    tril = mst_ref[...]
    row = lax.broadcasted_iota(jnp.int32, (L, W), 0)

    def level_gap(g, lf2, c):
        if c == 2:
            return jnp.where(row % 2 == 1, lf2, 0.0)
        if c >= SUBLANES:
            mids = [b * c + c // 2 - 1 for b in range(L // c)]
            return g - jnp.concatenate(
                [jnp.broadcast_to(g[m:m + 1, :], (c, W)) for m in mids], axis=0)
        lo = jnp.concatenate([jnp.broadcast_to(g[m:m + 1, :], (SUBLANES, W))
                              for m in range(1, L, SUBLANES)], axis=0)
        hi = jnp.concatenate([jnp.broadcast_to(g[m:m + 1, :], (SUBLANES, W))
                              for m in range(5, L, SUBLANES)], axis=0)
        return g - jnp.where(row % SUBLANES < 4, lo, hi)

    def head_stack(xb):
        return jnp.concatenate([xb] * N_HEADS, axis=0) * blk

    if True:
        qf = y_ref[r0:r0 + L, 0:W]
        key = y_ref[r0:r0 + L, W:2 * W]
        vb = y_ref[r0:r0 + L, 2 * W:3 * W]
        out_gate = y_ref[r0:r0 + L, 3 * W:4 * W]

        lf2 = lf_ref[r0:r0 + L, :] * LOG2_E
        g = _dot_const_left(tril, lf2)
        e_last = jnp.exp2(g[L - 1:L, :])
        q_hat = qf * jnp.exp2(g).astype(BF16)
        k_hat = key * jnp.exp2(g[L - 1:L, :] - g).astype(BF16)

        a = pm_ref[0] * _dot_nt(qf, head_stack(key))
        for i, c in enumerate(_hgrn_level_sizes()):
            e_i = jnp.exp2(-jnp.abs(level_gap(g, lf2, c))).astype(BF16)
            a = a + pm_ref[1 + i] * _dot_nt(qf * e_i, head_stack(key * e_i))
        o = _dot(a.astype(BF16), head_stack(vb)) + _dot_nt(q_hat, s_st[...].astype(BF16))

        o_ref[r0:r0 + L, :] = _head_rms(o, gain, blk) * out_gate.astype(F32)
        s_st[...] = s_st[...] * e_last + blk_f * _dot_tn(vb, k_hat)


def _swa_blocks(sink_ref, y_ref, o_ref, k_prev, v_prev, layer):
    T = SWA_T
    Wn = WINDOW
    QW = SWA_Q_HEADS * HEAD_DIM
    step = pl.program_id(1)

    k2 = [jnp.concatenate([k_prev[g], y_ref[:, QW + g * LANES:QW + (g + 1) * LANES]], axis=0)
          for g in range(SWA_KV_HEADS)]
    v2 = [jnp.concatenate([v_prev[g], y_ref[:, QW + (2 + g) * LANES:QW + (3 + g) * LANES]], axis=0)
          for g in range(SWA_KV_HEADS)]
    for g in range(SWA_KV_HEADS):
        k_prev[g] = k2[g][T:T + Wn]
        v_prev[g] = v2[g][T:T + Wn]

    qi = lax.broadcasted_iota(jnp.int32, (Wn, 2 * Wn), 0)
    ki = lax.broadcasted_iota(jnp.int32, (Wn, 2 * Wn), 1)
    band = (ki > qi) & (ki <= qi + Wn)
    band_first = band & (ki >= jnp.where(step > 0, 0, Wn))
    lane_w = lax.broadcasted_iota(jnp.int32, (Wn, LANES), 1) < HEAD_DIM
    lane_row = lax.broadcasted_iota(jnp.int32, (1, LANES), 1) < HEAD_DIM
    low_row = jnp.where(lane_row, 1.0, 0.0).astype(BF16)
    high_row = jnp.where(lane_row, 0.0, 1.0).astype(BF16)

    def scores(j, g):
        r0 = j * Wn
        kc = k2[g][r0:r0 + 2 * Wn]
        parts = []
        for c in (2 * g, 2 * g + 1):
            qc = y_ref[r0:r0 + Wn, c * LANES:(c + 1) * LANES]
            parts.append(qc * low_row)
            parts.append(qc * high_row)
        return _dot_nt(jnp.concatenate(parts, axis=0), kc)

    def finish(j, g, s):
        r0 = j * Wn
        mask = band_first if j == 0 else band
        if True:
            vc = v2[g][r0:r0 + 2 * Wn]
            outs = []
            for r in range(SWA_GROUP):
                sink = sink_ref[layer, SWA_GROUP * g + r]
                sr = jnp.where(mask, s[r * Wn:(r + 1) * Wn], -jnp.inf)
                m = jnp.maximum(jnp.max(sr, axis=-1, keepdims=True), sink)
                pexp = jnp.exp(sr - m)
                den = jnp.sum(pexp, axis=-1, keepdims=True) + jnp.exp(sink - m)
                outs.append(_dot(pexp.astype(BF16), vc) / den)
            c0 = 2 * g * LANES
            o_ref[r0:r0 + Wn, c0:c0 + LANES] = jnp.where(lane_w, outs[0], outs[1])
            o_ref[r0:r0 + Wn, c0 + LANES:c0 + 2 * LANES] = jnp.where(lane_w, outs[2], outs[3])

    return [(functools.partial(scores, j, g), functools.partial(finish, j, g))
            for j in range(T // Wn) for g in range(SWA_KV_HEADS)]


def _mixers_kernel(sink_ref, ml_ref, gt_ref, fb_ref, gm_ref, hb_ref, lf_ref, gh_ref, sw_ref,
                   ones_ref, tril_ref, exp_ref, tril64_ref, pm_ref,
                   om_ref, oh_ref, os_ref,
                   c_st, n_st, m_st, s_st, k_prev, v_prev, *, layer):
    @pl.when(pl.program_id(1) == 0)
    def _():
        for ref in (c_st, n_st, m_st, s_st, k_prev, v_prev):
            ref[...] = jnp.zeros_like(ref)

    swa_blocks = _swa_blocks(sink_ref, sw_ref, os_ref, k_prev, v_prev, layer)
    n_sub = MIX_ROWS // HGRN_L
    assert len(swa_blocks) == n_sub and n_sub % (MIX_ROWS // MLSTM_T) == 0
    per_mlstm = n_sub // (MIX_ROWS // MLSTM_T)
    for sub in range(n_sub):
        swa_scores, swa_finish = swa_blocks[sub]
        s = swa_scores()
        _hgrn_chunk(sub * HGRN_L, hb_ref, lf_ref, gh_ref, ones_ref, tril64_ref, pm_ref, oh_ref, s_st)
        swa_finish(s)
        if sub % per_mlstm == 0:
            _mlstm_chunk((sub // per_mlstm) * MLSTM_T, ml_ref, gt_ref, fb_ref, gm_ref, ones_ref,
                         tril_ref, exp_ref, om_ref, c_st, n_st, m_st)


def _mixers(ml, gt, hb, lf, sw, f_bias, mlstm_g, hgrn_g, sinks, layer, batch, consts):
    n = ml.shape[0]
    seq = n // batch
    T = MIX_ROWS
    steps = seq // T
    W4 = 4 * GROUP_W
    QW = SWA_Q_HEADS * HEAD_DIM
    row = lambda bi, ci: (bi * steps + ci, 0)
    lay = lambda bi, ci: (layer, 0, 0)
    cst = lambda bi, ci: (0, 0)
    blk_ones, tril, expand, tril64, pmask = consts
    return pl.pallas_call(
        functools.partial(_mixers_kernel, layer=layer),
        grid=(batch, steps),
        in_specs=[pl.BlockSpec(memory_space=pltpu.SMEM),
                  pl.BlockSpec((T, W4), row),
                  pl.BlockSpec((T, LANES), row),
                  pl.BlockSpec((None, 1, LANES), lay),
                  pl.BlockSpec((None, 1, GROUP_W), lay),
                  pl.BlockSpec((T, W4), row),
                  pl.BlockSpec((T, GROUP_W), row),
                  pl.BlockSpec((None, 1, GROUP_W), lay),
                  pl.BlockSpec((T, W4), row),
                  pl.BlockSpec((GROUP_W, GROUP_W), cst),
                  pl.BlockSpec((MLSTM_T, MLSTM_T), cst),
                  pl.BlockSpec((LANES, GROUP_W), cst),
                  pl.BlockSpec((HGRN_L, HGRN_L), cst),
                  pl.BlockSpec(pmask.shape, lambda bi, ci: (0, 0, 0))],
        out_specs=[pl.BlockSpec((T, GROUP_W), row),
                   pl.BlockSpec((T, GROUP_W), row),
                   pl.BlockSpec((T, QW), row)],
        out_shape=[jax.ShapeDtypeStruct((n, GROUP_W), F32),
                   jax.ShapeDtypeStruct((n, GROUP_W), F32),
                   jax.ShapeDtypeStruct((n, QW), F32)],
        scratch_shapes=[pltpu.VMEM((GROUP_W, GROUP_W), F32),
                        pltpu.VMEM((SUBLANES, GROUP_W), F32),
                        pltpu.VMEM((SUBLANES, LANES), F32),
                        pltpu.VMEM((GROUP_W, GROUP_W), F32),
                        pltpu.VMEM((SWA_KV_HEADS, WINDOW, LANES), BF16),
                        pltpu.VMEM((SWA_KV_HEADS, WINDOW, LANES), BF16)],
        compiler_params=_params(2),
        name="mixers",
    )(sinks, ml, gt, f_bias, mlstm_g, hb, lf, hgrn_g, sw, blk_ones, tril, expand, tril64, pmask)


def _tail_kernel(x_ref, mm_ref, mh_ref, ms_ref, p_ref, wo_ref, g2_ref, wu_ref, wd_ref, g3_ref,
                 wg_ref, wp_ref, g4_ref, o_ref):
    W = GROUP_W
    mix = _dot(mm_ref[...].astype(BF16), wo_ref[0:W, :])
    mix = mix + _dot(mh_ref[...].astype(BF16), wo_ref[W:2 * W, :])
    mix = mix + _dot(ms_ref[...].astype(BF16), wo_ref[2 * W:4 * W, :])
    x = x_ref[...] + mix

    h2 = _rms(x, g2_ref[...]).astype(BF16)
    mlp = None
    for c in range(D_FF // D_MODEL):
        u = _dot(h2, wu_ref[:, c * D_MODEL:(c + 1) * D_MODEL])
        act = jnp.square(jnp.maximum(u, 0.0)).astype(BF16)
        d = _dot(act, wd_ref[c * D_MODEL:(c + 1) * D_MODEL, :])
        mlp = d if mlp is None else mlp + d
    x = x + mlp

    gate = _sigmoid(_dot(_rms(x, g3_ref[...]).astype(BF16), wg_ref[...]))
    emb = _rms(_dot(p_ref[...].astype(BF16), wp_ref[...]), g4_ref[...])
    o_ref[...] = x + gate * emb


def _tail(x, mm, mh, ms, p, w_out, g2, w_up, w_down, g3, w_gate, w_proj, g4, layer):
    n = x.shape[0]
    tm = min(TAIL_TM, n)
    tiles = n // tm
    row = lambda i: (i, 0)
    lay = lambda i: (layer, 0, 0)
    once = pl.Buffered(1)

    def wspec(shape):
        return pl.BlockSpec((None,) + shape, lay, pipeline_mode=once)

    return pl.pallas_call(
        _tail_kernel,
        grid=(tiles,),
        in_specs=[pl.BlockSpec((tm, D_MODEL), row),
                  pl.BlockSpec((tm, GROUP_W), row),
                  pl.BlockSpec((tm, GROUP_W), row),
                  pl.BlockSpec((tm, 2 * GROUP_W), row),
                  pl.BlockSpec((None, tm, PLE_DIM), lambda i: (layer, i, 0)),
                  wspec((D_MODEL, D_MODEL)),
                  wspec((1, D_MODEL)),
                  wspec((D_MODEL, D_FF)),
                  wspec((D_FF, D_MODEL)),
                  wspec((1, D_MODEL)),
                  wspec((D_MODEL, D_MODEL)),
                  wspec((PLE_DIM, D_MODEL)),
                  wspec((1, D_MODEL))],
        out_specs=pl.BlockSpec((tm, D_MODEL), row),
        out_shape=jax.ShapeDtypeStruct((n, D_MODEL), F32),
        compiler_params=_params(1),
        name="tail",
    )(x, mm, mh, ms, p, w_out, g2, w_up, w_down, g3, w_gate, w_proj, g4)


def kernel(x, p, positions, in_norm_g, w_in, b_in, mlstm_f_bias, mlstm_conv_w, mlstm_conv_b,
           mlstm_norm_g, hgrn_lb_logits, hgrn_norm_g, swa_q_norm_g, swa_k_norm_g, swa_sinks,
           w_out, mlp_norm_g, w_up, w_down, ple_norm_g, w_ple_gate, w_ple_proj, ple_post_norm_g):
    batch, seq, d_model = x.shape
    depth = w_in.shape[0]
    n = batch * seq
    assert d_model == D_MODEL and seq % max(MIX_ROWS, IN_TM) == 0

    n_raw = GATE_OFF + GATE_COLS
    pad = LANES - GATE_COLS
    w_in_p = jnp.concatenate(
        [w_in[..., :n_raw], jnp.zeros((depth, D_MODEL, pad), w_in.dtype), w_in[..., n_raw:]],
        axis=-1).astype(BF16)
    b_in_p = jnp.concatenate(
        [b_in[..., :n_raw], jnp.zeros((depth, pad), b_in.dtype), b_in[..., n_raw:]],
        axis=-1).astype(F32).reshape(depth, 1, IN_COLS)
    f_bias = jnp.zeros((depth, 1, LANES), F32).at[:, 0, N_HEADS:GATE_COLS].set(mlstm_f_bias.astype(F32))
    row3 = lambda a: a.astype(F32).reshape(depth, 1, a.shape[-1])
    q_gain = row3(jnp.tile(swa_q_norm_g, (1, LANES // HEAD_DIM)))
    k_gain = row3(jnp.tile(swa_k_norm_g, (1, LANES // HEAD_DIM)))
    w_out_b, w_up_b, w_down_b = w_out.astype(BF16), w_up.astype(BF16), w_down.astype(BF16)
    w_gate_b, w_proj_b = w_ple_gate.astype(BF16), w_ple_proj.astype(BF16)

    blk256 = _blk_ones(GROUP_W)
    tril = jnp.asarray(np.tril(np.ones((MLSTM_T, MLSTM_T), np.float32)), dtype=BF16)
    expand = jnp.asarray(
        (np.arange(LANES)[:, None] == (np.arange(GROUP_W) // HEAD_DIM)[None, :]).astype(np.float32),
        dtype=BF16)
    hgrn_tril, hgrn_pmask = _hgrn_constants()

    cos_t, sin_t = _rope_tables(positions)
    la, l1m, oml = [row3(a) for a in _hgrn_lower_bounds(hgrn_lb_logits)]
    sinks = swa_sinks.astype(F32)

    xf = x.reshape(n, D_MODEL)
    for l in range(depth):
        ml, gt, hb, lf, sw = _in_proj(
            xf, row3(in_norm_g), w_in_p, b_in_p, mlstm_conv_w.astype(F32), row3(mlstm_conv_b),
            la, l1m, oml, cos_t, sin_t, q_gain, k_gain, blk256, l, seq)
        mm, mh, ms = _mixers(ml, gt, hb, lf, sw, f_bias, row3(mlstm_norm_g), row3(hgrn_norm_g), sinks,
                             l, batch, (blk256, tril, expand, hgrn_tril, hgrn_pmask))
        xf = _tail(xf, mm, mh, ms, p.reshape(depth, n, PLE_DIM), w_out_b, row3(mlp_norm_g), w_up_b,
                   w_down_b, row3(ple_norm_g), w_gate_b, w_proj_b, row3(ple_post_norm_g), l)
    return xf.reshape(batch, seq, D_MODEL)
```

```python
import functools

import numpy as np
import jax
import jax.numpy as jnp
from jax import lax
from jax.experimental import pallas as pl
from jax.experimental.pallas import tpu as pltpu

F32 = jnp.float32
BF16 = jnp.bfloat16

D_MODEL = 1024
N_HEADS = 4
HEAD_DIM = 64
GROUP_W = N_HEADS * HEAD_DIM
CONV_K = 4
SWA_Q_HEADS = 8
SWA_KV_HEADS = 2
SWA_GROUP = SWA_Q_HEADS // SWA_KV_HEADS
WINDOW = 128
ROPE_THETA = 500000.0
ROT_DIM = HEAD_DIM // 4
ROT_HALF = ROT_DIM // 2
D_FF = 4 * D_MODEL
PLE_DIM = 256
EPS = 1e-6
LOG2_E = 1.4426950408889634

LANES = 128
SUBLANES = 8
VMEM_LIMIT_BYTES = 56 * 1024 * 1024

GATE_COLS = 2 * N_HEADS
M_COLS = 4 * GROUP_W + LANES
H_COLS = 4 * GROUP_W
S_COLS = SWA_Q_HEADS * HEAD_DIM + 2 * SWA_KV_HEADS * HEAD_DIM
IN_COLS = M_COLS + H_COLS + S_COLS
GATE_OFF = 4 * GROUP_W

IN_TM = 512
TAIL_TM = 512
ROPE_TM = 2048
MIX_ROWS = 512
MLSTM_T = 256
HGRN_L = 64
SWA_T = MIX_ROWS

N_SPLIT = 2
assert HGRN_L == HEAD_DIM


def _params(n_grid_dims, flags=None):
    return pltpu.CompilerParams(
        dimension_semantics=("arbitrary",) * n_grid_dims,
        vmem_limit_bytes=VMEM_LIMIT_BYTES,
        flags=flags)


def _dot(a, b):
    return jnp.dot(a, b, preferred_element_type=F32)


def _dot_nt(a, b):
    return lax.dot_general(a, b, (((1,), (1,)), ((), ())), preferred_element_type=F32)


def _dot_tn(a, b):
    return lax.dot_general(a, b, (((0,), (0,)), ((), ())), preferred_element_type=F32)


def _split(x, n=N_SPLIT):
    pieces = []
    r = x
    for _ in range(n):
        p = r.astype(BF16)
        pieces.append(p)
        r = r - p.astype(F32)
    return pieces


def _dot_const_left(m, x):
    acc = None
    for p in _split(x):
        t = _dot(m, p)
        acc = t if acc is None else acc + t
    return acc


def _dot_const_right(x, m):
    acc = None
    for p in _split(x):
        t = _dot(p, m)
        acc = t if acc is None else acc + t
    return acc


def _sigmoid(x):
    return 1.0 / (1.0 + jnp.exp(-x))


def _log_sigmoid(x):
    return jnp.minimum(x, 0.0) - jnp.log1p(jnp.exp(-jnp.abs(x)))


def _rms(x, g):
    return x * lax.rsqrt(jnp.mean(x * x, axis=-1, keepdims=True) + EPS) * g


def _head_rms(x, g, blk_ones):
    ms = _dot(jnp.square(x).astype(BF16), blk_ones) * (1.0 / HEAD_DIM)
    return x * lax.rsqrt(ms + EPS) * g


def _lane_head_id(shape):
    return lax.broadcasted_iota(jnp.int32, shape, len(shape) - 1) // HEAD_DIM


def _blk_ones(width):
    hid = np.arange(width) // HEAD_DIM
    return jnp.asarray((hid[:, None] == hid[None, :]).astype(np.float32), dtype=BF16)


def _hgrn_level_sizes():
    sizes = []
    c = 2
    while c <= HGRN_L:
        sizes.append(c)
        c *= 2
    return sizes


def _hgrn_constants():
    L = HGRN_L
    t = np.arange(L)[:, None]
    u = np.arange(L)[None, :]
    masks = [(t == u)]
    for c in _hgrn_level_sizes():
        mid = (t // c) * c + c // 2 - 1
        same_blk = (t // c) == (u // c)
        u_mid = (u // c) * c + c // 2 - 1
        masks.append(same_blk & (t > mid) & (u <= u_mid))
    tril = (u <= t).astype(np.float32)
    pmask = np.stack([np.tile(m.astype(np.float32), (1, N_HEADS)) for m in masks])
    return jnp.asarray(tril, dtype=BF16), jnp.asarray(pmask, dtype=F32)


def _rope_kernel(pos_ref, pat_ref, cos_ref, sin_ref):
    ang = pos_ref[...].astype(F32) * pat_ref[0:1, :]
    cos_ref[...] = jnp.cos(ang)
    sin_ref[...] = jnp.sin(ang) * pat_ref[1:2, :]


def _rope_tables(positions):
    n = positions.size
    inv_freq = ROPE_THETA ** (-jnp.arange(0, ROT_DIM, 2, dtype=F32) / ROT_DIM)
    lane = np.arange(LANES) % HEAD_DIM
    freq = jnp.where(lane < ROT_DIM, inv_freq[lane % ROT_HALF], 0.0)
    sign = np.where(lane < ROT_HALF, -1.0, np.where(lane < ROT_DIM, 1.0, 0.0)).astype(np.float32)
    pat = jnp.zeros((SUBLANES, LANES), F32).at[0].set(freq).at[1].set(sign)
    tm = min(ROPE_TM, n)
    out = jax.ShapeDtypeStruct((n, LANES), F32)
    return pl.pallas_call(
        _rope_kernel,
        grid=(n // tm,),
        in_specs=[pl.BlockSpec((tm, 1), lambda i: (i, 0)),
                  pl.BlockSpec((SUBLANES, LANES), lambda i: (0, 0))],
        out_specs=[pl.BlockSpec((tm, LANES), lambda i: (i, 0))] * 2,
        out_shape=[out, out],
        compiler_params=_params(1),
        name="rope_tables",
    )(positions.reshape(n, 1), pat)


def _lb_kernel(lg_ref, la_ref, l1m_ref, oml_ref):
    z = lg_ref[...]
    e = jnp.exp(z - jnp.max(z, axis=0, keepdims=True))
    sm = e / jnp.sum(e, axis=0, keepdims=True)
    depth = z.shape[0]
    run = sm[0:1, :]
    first = run
    for l in range(depth):
        if l > 0:
            run = run + sm[l:l + 1, :]
        lb = run - first
        la_ref[l:l + 1, :] = jnp.log(lb)
        l1m_ref[l:l + 1, :] = jnp.log1p(-lb)
        oml_ref[l:l + 1, :] = 1.0 - lb


def _hgrn_lower_bounds(logits):
    out = jax.ShapeDtypeStruct(logits.shape, F32)
    return pl.pallas_call(
        _lb_kernel, out_shape=[out, out, out], name="hgrn_lower_bounds",
    )(logits.astype(F32))


def _in_proj_kernel(x_ref, g_ref, w_ref, b_ref, cw_ref, cb_ref, la_ref, l1m_ref, oml_ref,
                    cos_ref, sin_ref, qg_ref, kg_ref, ones_ref,
                    ml_ref, gt_ref, hb_ref, lf_ref, sw_ref, xbuf, *, tiles_per_seq):
    TM = x_ref.shape[0]
    W = GROUP_W

    @pl.when(pl.program_id(0) % tiles_per_seq == 0)
    def _():
        xbuf[0:SUBLANES, :] = jnp.zeros((SUBLANES, 2 * W), F32)

    h = _rms(x_ref[...], g_ref[...]).astype(BF16)

    def proj(lo, width):
        return _dot(h, w_ref[:, lo:lo + width]) + b_ref[:, lo:lo + width]

    cos_t = cos_ref[...]
    sin_t = sin_ref[...]
    blk = ones_ref[...]
    lane = lax.broadcasted_iota(jnp.int32, (TM, LANES), 1)
    low_half = lane < HEAD_DIM
    rot_first = (lane % HEAD_DIM) < ROT_HALF
    s0 = M_COLS + H_COLS
    QW = SWA_Q_HEADS * HEAD_DIM

    def head_sumsq(y):
        width = y.shape[-1]
        return _dot(jnp.square(y).astype(BF16), blk[0:width, 0:width])

    def norm_rope(v, sumsq, gain):
        xn = v * lax.rsqrt(sumsq * (1.0 / HEAD_DIM) + EPS) * gain
        partner = jnp.where(rot_first, pltpu.roll(xn, LANES - ROT_HALF, 1), pltpu.roll(xn, ROT_HALF, 1))
        return xn * cos_t + partner * sin_t

    def swa_q(c, y, sumsq):
        for half in range(2):
            sl = slice(half * LANES, (half + 1) * LANES)
            sw_ref[:, (c + half) * LANES:(c + half + 1) * LANES] = (
                norm_rope(y[:, sl], sumsq[:, sl], qg_ref[...]) * (HEAD_DIM ** -0.5 * LOG2_E)).astype(BF16)

    def swa_kv(y, sumsq):
        k = norm_rope(y[:, 0:LANES], sumsq, kg_ref[...])
        v = y[:, LANES:2 * LANES]
        k_sw = pltpu.roll(k, HEAD_DIM, 1)
        v_sw = pltpu.roll(v, HEAD_DIM, 1)
        sw_ref[:, QW:QW + LANES] = jnp.where(low_half, k, k_sw).astype(BF16)
        sw_ref[:, QW + LANES:QW + 2 * LANES] = jnp.where(low_half, k_sw, k).astype(BF16)
        sw_ref[:, QW + 2 * LANES:QW + 3 * LANES] = jnp.where(low_half, v, v_sw).astype(BF16)
        sw_ref[:, QW + 3 * LANES:QW + 4 * LANES] = jnp.where(low_half, v_sw, v).astype(BF16)

    def mlstm_qk(y):
        xbuf[SUBLANES:SUBLANES + TM, :] = y
        acc = cb_ref[...] + cw_ref[CONV_K - 1:CONV_K, :] * xbuf[SUBLANES:SUBLANES + TM, :]
        for j in range(1, CONV_K):
            acc = acc + cw_ref[CONV_K - 1 - j:CONV_K - j, :] * xbuf[SUBLANES - j:SUBLANES - j + TM, :]
        xbuf[0:SUBLANES, :] = xbuf[TM:TM + SUBLANES, :]
        qk = acc * _sigmoid(acc)
        ml_ref[:, 0:W] = qk[:, 0:W].astype(BF16)
        ml_ref[:, W:2 * W] = (qk[:, W:2 * W] * (HEAD_DIM ** -0.5)).astype(BF16)

    def hgrn_qf(y):
        hq = y[:, 0:W]
        hf = y[:, W:2 * W]
        la = la_ref[...]
        u = jnp.exp(-jnp.abs(hf))
        t = l1m_ref[...] + (jnp.minimum(hf, 0.0) - jnp.log1p(u))
        lf_ref[...] = jnp.maximum(la, t) + jnp.log1p(jnp.exp(-jnp.abs(la - t)))
        hb_ref[:, 0:W] = (hq * _sigmoid(hq)).astype(BF16)
        hb_ref[:, W:2 * W] = (oml_ref[...] * jnp.where(hf >= 0.0, u, 1.0) / (1.0 + u)).astype(BF16)

    y_h = proj(M_COLS, 2 * W)
    y_m = proj(0, 2 * W)
    hgrn_qf(y_h)
    y_kv = proj(s0 + QW, 2 * LANES)
    mlstm_qk(y_m)
    y_q0 = proj(s0, 2 * LANES)
    swa_kv(y_kv, head_sumsq(y_kv[:, 0:LANES]))
    y_q1 = proj(s0 + 2 * LANES, 2 * LANES)
    swa_q(0, y_q0, head_sumsq(y_q0))
    y_h2 = proj(M_COLS + 2 * W, 2 * W)
    swa_q(2, y_q1, head_sumsq(y_q1))
    y_m2 = proj(2 * W, 2 * W)
    hg = y_h2[:, W:2 * W]
    hb_ref[:, 2 * W:3 * W] = y_h2[:, 0:W].astype(BF16)
    hb_ref[:, 3 * W:4 * W] = (hg * _sigmoid(hg)).astype(BF16)
    gt_ref[...] = proj(GATE_OFF, LANES)
    ml_ref[:, 2 * W:3 * W] = y_m2[:, 0:W].astype(BF16)
    ml_ref[:, 3 * W:4 * W] = _sigmoid(y_m2[:, W:2 * W]).astype(BF16)


def _in_proj(x, g, w, b, conv_w, conv_b, la, l1m, oml, cos_t, sin_t, q_gain, k_gain, blk_ones,
             layer, seq):
    n = x.shape[0]
    tm = min(IN_TM, n)
    row = lambda i: (i, 0)
    lay = lambda i: (layer, 0, 0)
    W4 = 4 * GROUP_W
    return pl.pallas_call(
        functools.partial(_in_proj_kernel, tiles_per_seq=seq // tm),
        grid=(n // tm,),
        in_specs=[pl.BlockSpec((tm, D_MODEL), row),
                  pl.BlockSpec((None, 1, D_MODEL), lay),
                  pl.BlockSpec((None, D_MODEL, IN_COLS), lay),
                  pl.BlockSpec((None, 1, IN_COLS), lay),
                  pl.BlockSpec((None, CONV_K, 2 * GROUP_W), lay),
                  pl.BlockSpec((None, 1, 2 * GROUP_W), lay),
                  pl.BlockSpec((None, 1, GROUP_W), lay),
                  pl.BlockSpec((None, 1, GROUP_W), lay),
                  pl.BlockSpec((None, 1, GROUP_W), lay),
                  pl.BlockSpec((tm, LANES), row),
                  pl.BlockSpec((tm, LANES), row),
                  pl.BlockSpec((None, 1, LANES), lay),
                  pl.BlockSpec((None, 1, LANES), lay),
                  pl.BlockSpec((GROUP_W, GROUP_W), lambda i: (0, 0))],
        out_specs=[pl.BlockSpec((tm, W4), row),
                   pl.BlockSpec((tm, LANES), row),
                   pl.BlockSpec((tm, W4), row),
                   pl.BlockSpec((tm, GROUP_W), row),
                   pl.BlockSpec((tm, W4), row)],
        out_shape=[jax.ShapeDtypeStruct((n, W4), BF16),
                   jax.ShapeDtypeStruct((n, LANES), F32),
                   jax.ShapeDtypeStruct((n, W4), BF16),
                   jax.ShapeDtypeStruct((n, GROUP_W), F32),
                   jax.ShapeDtypeStruct((n, W4), BF16)],
        scratch_shapes=[pltpu.VMEM((tm + SUBLANES, 2 * GROUP_W), F32)],
        compiler_params=_params(1),
        name="in_proj",
    )(x, g, w, b, conv_w, conv_b, la, l1m, oml, cos_t, sin_t, q_gain, k_gain, blk_ones)


def _mlstm_chunk(r0, y_ref, gt_ref, fb_ref, g_ref, ones_ref, tril_ref, exp_ref, o_ref,
                 c_st, n_st, m_st):
    T = MLSTM_T
    W = GROUP_W
    qb = y_ref[r0:r0 + T, 0:W]
    kb = y_ref[r0:r0 + T, W:2 * W]
    vb = y_ref[r0:r0 + T, 2 * W:3 * W]
    out_gate = y_ref[r0:r0 + T, 3 * W:4 * W]

    gates = gt_ref[r0:r0 + T, :]
    i_pre = gates
    lf = pltpu.roll(_log_sigmoid(gates + fb_ref[...]), LANES - N_HEADS, 1)
    lf_hi = lf.astype(BF16)
    lf_lo = (lf - lf_hi.astype(F32)).astype(BF16)
    bb = _dot(tril_ref[...], jnp.concatenate([lf_hi, lf_lo], axis=1))
    b = bb[:, 0:LANES] + bb[:, LANES:2 * LANES]
    a = i_pre - b
    row = lax.broadcasted_iota(jnp.int32, (T, LANES), 0)
    cm = a
    sh = 1
    while sh < T:
        cm = jnp.maximum(cm, jnp.where(row >= sh, pltpu.roll(cm, sh, 0), -jnp.inf))
        sh *= 2
    m_prev = m_st[0:1, :]
    m_j = b + jnp.maximum(m_prev, cm)
    m_new = m_j[T - 1:T, :]
    b_last = b[T - 1:T, :]
    w_inter = jnp.exp(b + m_prev - m_j)
    e_den = jnp.exp(-m_j)
    w_s = jnp.exp(b_last + a - m_new)
    decay = jnp.exp(b_last + m_prev - m_new)
    cexp = (b - m_j) * LOG2_E
    a_t = (a * LOG2_E).T

    stacked = jnp.concatenate([w_inter, e_den, w_s], axis=0)
    head_lane = lax.broadcasted_iota(jnp.int32, stacked.shape, 1) < N_HEADS
    ex = _dot(jnp.where(head_lane, stacked, 0.0).astype(BF16), exp_ref[...])
    w_inter_x = ex[0:T]
    e_den_x = ex[T:2 * T]
    w_s_x = ex[2 * T:3 * T]
    decay8 = jnp.where(head_lane[0:SUBLANES], jnp.broadcast_to(decay, (SUBLANES, LANES)), 0.0)
    decay_x = _dot_const_right(decay8, exp_ref[...])[0:1]

    blk = ones_ref[...]
    hid = _lane_head_id((T, W))
    causal = (lax.broadcasted_iota(jnp.int32, (T, T), 1)
              <= lax.broadcasted_iota(jnp.int32, (T, T), 0))
    head_rows = [blk[h * HEAD_DIM:h * HEAD_DIM + 1, :] for h in range(N_HEADS)]
    s_all = _dot_nt(jnp.concatenate([qb * hr for hr in head_rows], axis=0), kb)
    num = None
    den = None
    for h in range(N_HEADS):
        dm = cexp[:, h:h + 1] + a_t[h:h + 1, :]
        s = s_all[h * T:(h + 1) * T] * jnp.exp2(jnp.where(causal, dm, -jnp.inf))
        dn = _dot(s.astype(BF16), vb * head_rows[h])
        dd = jnp.where(hid == h, jnp.sum(s, axis=-1, keepdims=True), 0.0)
        num = dn if num is None else num + dn
        den = dd if den is None else den + dd
    n_row = n_st[0:1, :].astype(BF16)
    num = num + w_inter_x * _dot(qb, c_st[...].astype(BF16))
    den = den + w_inter_x * _dot(qb * n_row, blk)
    hh = num / jnp.maximum(jnp.abs(den), e_den_x)

    o_ref[r0:r0 + T, :] = _head_rms(hh, g_ref[...], blk) * out_gate.astype(F32)

    kw = kb * w_s_x.astype(BF16)
    c_st[...] = decay_x * c_st[...] + blk.astype(F32) * _dot_tn(kw, vb)
    n_st[...] = decay_x * n_st[...] + _dot(jnp.ones((SUBLANES, T), BF16), kw)
    m_st[...] = jnp.broadcast_to(m_new, m_st.shape)


def _hgrn_chunk(r0, y_ref, lf_ref, g_ref, ones_ref, mst_ref, pm_ref, o_ref, s_st):
    L = HGRN_L
    W = GROUP_W
    gain = g_ref[...]
    blk = ones_ref[...]
    blk_f = blk.astype(F32)
    tril = mst_ref[...]
    row = lax.broadcasted_iota(jnp.int32, (L, W), 0)

    def level_gap(g, lf2, c):
        if c == 2:
            return jnp.where(row % 2 == 1, lf2, 0.0)
        if c >= SUBLANES:
            mids = [b * c + c // 2 - 1 for b in range(L // c)]
            return g - jnp.concatenate(
                [jnp.broadcast_to(g[m:m + 1, :], (c, W)) for m in mids], axis=0)
        lo = jnp.concatenate([jnp.broadcast_to(g[m:m + 1, :], (SUBLANES, W))
                              for m in range(1, L, SUBLANES)], axis=0)
        hi = jnp.concatenate([jnp.broadcast_to(g[m:m + 1, :], (SUBLANES, W))
                              for m in range(5, L, SUBLANES)], axis=0)
        return g - jnp.where(row % SUBLANES < 4, lo, hi)

    def head_stack(xb):
        return jnp.concatenate([xb] * N_HEADS, axis=0) * blk

    if True:
        qf = y_ref[r0:r0 + L, 0:W]
        key = y_ref[r0:r0 + L, W:2 * W]
        vb = y_ref[r0:r0 + L, 2 * W:3 * W]
        out_gate = y_ref[r0:r0 + L, 3 * W:4 * W]

        lf2 = lf_ref[r0:r0 + L, :] * LOG2_E
        g = _dot_const_left(tril, lf2)
        e_last = jnp.exp2(g[L - 1:L, :])
        q_hat = qf * jnp.exp2(g).astype(BF16)
        k_hat = key * jnp.exp2(g[L - 1:L, :] - g).astype(BF16)

        a = pm_ref[0] * _dot_nt(qf, head_stack(key))
        for i, c in enumerate(_hgrn_level_sizes()):
            e_i = jnp.exp2(-jnp.abs(level_gap(g, lf2, c))).astype(BF16)
            a = a + pm_ref[1 + i] * _dot_nt(qf * e_i, head_stack(key * e_i))
        o = _dot(a.astype(BF16), head_stack(vb)) + _dot_nt(q_hat, s_st[...].astype(BF16))

        o_ref[r0:r0 + L, :] = _head_rms(o, gain, blk) * out_gate.astype(F32)
        s_st[...] = s_st[...] * e_last + blk_f * _dot_tn(vb, k_hat)


def _swa_blocks(sink_ref, y_ref, o_ref, k_prev, v_prev, layer):
    T = SWA_T
    Wn = WINDOW
    QW = SWA_Q_HEADS * HEAD_DIM
    step = pl.program_id(1)

    k2 = [jnp.concatenate([k_prev[g], y_ref[:, QW + g * LANES:QW + (g + 1) * LANES]], axis=0)
          for g in range(SWA_KV_HEADS)]
    v2 = [jnp.concatenate([v_prev[g], y_ref[:, QW + (2 + g) * LANES:QW + (3 + g) * LANES]], axis=0)
          for g in range(SWA_KV_HEADS)]
    for g in range(SWA_KV_HEADS):
        k_prev[g] = k2[g][T:T + Wn]
        v_prev[g] = v2[g][T:T + Wn]

    qi = lax.broadcasted_iota(jnp.int32, (Wn, 2 * Wn), 0)
    ki = lax.broadcasted_iota(jnp.int32, (Wn, 2 * Wn), 1)
    band = (ki > qi) & (ki <= qi + Wn)
    band_first = band & (ki >= jnp.where(step > 0, 0, Wn))
    lane_w = lax.broadcasted_iota(jnp.int32, (Wn, LANES), 1) < HEAD_DIM
    lane_row = lax.broadcasted_iota(jnp.int32, (1, LANES), 1) < HEAD_DIM
    low_row = jnp.where(lane_row, 1.0, 0.0).astype(BF16)
    high_row = jnp.where(lane_row, 0.0, 1.0).astype(BF16)

    def scores(j, g):
        r0 = j * Wn
        kc = k2[g][r0:r0 + 2 * Wn]
        parts = []
        for c in (2 * g, 2 * g + 1):
            qc = y_ref[r0:r0 + Wn, c * LANES:(c + 1) * LANES]
            parts.append(qc * low_row)
            parts.append(qc * high_row)
        return _dot_nt(jnp.concatenate(parts, axis=0), kc)

    def finish(j, g, s):
        r0 = j * Wn
        mask = band_first if j == 0 else band
        if True:
            vc = v2[g][r0:r0 + 2 * Wn]
            outs = []
            for r in range(SWA_GROUP):
                sink = sink_ref[layer, SWA_GROUP * g + r] * LOG2_E
                sr = jnp.where(mask, s[r * Wn:(r + 1) * Wn], -jnp.inf)
                m = jnp.maximum(jnp.max(sr, axis=-1, keepdims=True), sink)
                pexp = jnp.exp2(sr - m)
                den = jnp.sum(pexp, axis=-1, keepdims=True) + jnp.exp2(sink - m)
                outs.append(_dot(pexp.astype(BF16), vc) / den)
            c0 = 2 * g * LANES
            o_ref[r0:r0 + Wn, c0:c0 + LANES] = jnp.where(lane_w, outs[0], outs[1])
            o_ref[r0:r0 + Wn, c0 + LANES:c0 + 2 * LANES] = jnp.where(lane_w, outs[2], outs[3])

    return [(functools.partial(scores, j, g), functools.partial(finish, j, g))
            for j in range(T // Wn) for g in range(SWA_KV_HEADS)]


def _mixers_kernel(sink_ref, ml_ref, gt_ref, fb_ref, gm_ref, hb_ref, lf_ref, gh_ref, sw_ref,
                   ones_ref, tril_ref, exp_ref, tril64_ref, pm_ref,
                   om_ref, oh_ref, os_ref,
                   c_st, n_st, m_st, s_st, k_prev, v_prev, *, layer):
    @pl.when(pl.program_id(1) == 0)
    def _():
        for ref in (c_st, n_st, m_st, s_st, k_prev, v_prev):
            ref[...] = jnp.zeros_like(ref)

    swa_blocks = _swa_blocks(sink_ref, sw_ref, os_ref, k_prev, v_prev, layer)
    n_sub = MIX_ROWS // HGRN_L
    assert len(swa_blocks) == n_sub and n_sub % (MIX_ROWS // MLSTM_T) == 0
    per_mlstm = n_sub // (MIX_ROWS // MLSTM_T)
    for sub in range(n_sub):
        swa_scores, swa_finish = swa_blocks[sub]
        s = swa_scores()
        _hgrn_chunk(sub * HGRN_L, hb_ref, lf_ref, gh_ref, ones_ref, tril64_ref, pm_ref, oh_ref, s_st)
        swa_finish(s)
        if sub % per_mlstm == 0:
            _mlstm_chunk((sub // per_mlstm) * MLSTM_T, ml_ref, gt_ref, fb_ref, gm_ref, ones_ref,
                         tril_ref, exp_ref, om_ref, c_st, n_st, m_st)


def _mixers(ml, gt, hb, lf, sw, f_bias, mlstm_g, hgrn_g, sinks, layer, batch, consts):
    n = ml.shape[0]
    seq = n // batch
    T = MIX_ROWS
    steps = seq // T
    W4 = 4 * GROUP_W
    QW = SWA_Q_HEADS * HEAD_DIM
    row = lambda bi, ci: (bi * steps + ci, 0)
    lay = lambda bi, ci: (layer, 0, 0)
    cst = lambda bi, ci: (0, 0)
    blk_ones, tril, expand, tril64, pmask = consts
    return pl.pallas_call(
        functools.partial(_mixers_kernel, layer=layer),
        grid=(batch, steps),
        in_specs=[pl.BlockSpec(memory_space=pltpu.SMEM),
                  pl.BlockSpec((T, W4), row),
                  pl.BlockSpec((T, LANES), row),
                  pl.BlockSpec((None, 1, LANES), lay),
                  pl.BlockSpec((None, 1, GROUP_W), lay),
                  pl.BlockSpec((T, W4), row),
                  pl.BlockSpec((T, GROUP_W), row),
                  pl.BlockSpec((None, 1, GROUP_W), lay),
                  pl.BlockSpec((T, W4), row),
                  pl.BlockSpec((GROUP_W, GROUP_W), cst),
                  pl.BlockSpec((MLSTM_T, MLSTM_T), cst),
                  pl.BlockSpec((LANES, GROUP_W), cst),
                  pl.BlockSpec((HGRN_L, HGRN_L), cst),
                  pl.BlockSpec(pmask.shape, lambda bi, ci: (0, 0, 0))],
        out_specs=[pl.BlockSpec((T, GROUP_W), row),
                   pl.BlockSpec((T, GROUP_W), row),
                   pl.BlockSpec((T, QW), row)],
        out_shape=[jax.ShapeDtypeStruct((n, GROUP_W), F32),
                   jax.ShapeDtypeStruct((n, GROUP_W), F32),
                   jax.ShapeDtypeStruct((n, QW), F32)],
        scratch_shapes=[pltpu.VMEM((GROUP_W, GROUP_W), F32),
                        pltpu.VMEM((SUBLANES, GROUP_W), F32),
                        pltpu.VMEM((SUBLANES, LANES), F32),
                        pltpu.VMEM((GROUP_W, GROUP_W), F32),
                        pltpu.VMEM((SWA_KV_HEADS, WINDOW, LANES), BF16),
                        pltpu.VMEM((SWA_KV_HEADS, WINDOW, LANES), BF16)],
        compiler_params=_params(2),
        name="mixers",
    )(sinks, ml, gt, f_bias, mlstm_g, hb, lf, hgrn_g, sw, blk_ones, tril, expand, tril64, pmask)


def _tail_kernel(x_ref, mm_ref, mh_ref, ms_ref, p_ref, wo_ref, g2_ref, wu_ref, wd_ref, g3_ref,
                 wg_ref, wp_ref, g4_ref, o_ref):
    W = GROUP_W
    mix = _dot(mm_ref[...].astype(BF16), wo_ref[0:W, :])
    mix = mix + _dot(mh_ref[...].astype(BF16), wo_ref[W:2 * W, :])
    mix = mix + _dot(ms_ref[...].astype(BF16), wo_ref[2 * W:4 * W, :])
    x = x_ref[...] + mix

    h2 = _rms(x, g2_ref[...]).astype(BF16)
    mlp = None
    for c in range(D_FF // D_MODEL):
        u = _dot(h2, wu_ref[:, c * D_MODEL:(c + 1) * D_MODEL])
        act = jnp.square(jnp.maximum(u, 0.0)).astype(BF16)
        d = _dot(act, wd_ref[c * D_MODEL:(c + 1) * D_MODEL, :])
        mlp = d if mlp is None else mlp + d
    x = x + mlp

    gate = _sigmoid(_dot(_rms(x, g3_ref[...]).astype(BF16), wg_ref[...]))
    emb = _rms(_dot(p_ref[...].astype(BF16), wp_ref[...]), g4_ref[...])
    o_ref[...] = x + gate * emb


def _tail(x, mm, mh, ms, p, w_out, g2, w_up, w_down, g3, w_gate, w_proj, g4, layer):
    n = x.shape[0]
    tm = min(TAIL_TM, n)
    tiles = n // tm
    row = lambda i: (i, 0)
    lay = lambda i: (layer, 0, 0)
    once = pl.Buffered(1)

    def wspec(shape):
        return pl.BlockSpec((None,) + shape, lay, pipeline_mode=once)

    return pl.pallas_call(
        _tail_kernel,
        grid=(tiles,),
        in_specs=[pl.BlockSpec((tm, D_MODEL), row),
                  pl.BlockSpec((tm, GROUP_W), row),
                  pl.BlockSpec((tm, GROUP_W), row),
                  pl.BlockSpec((tm, 2 * GROUP_W), row),
                  pl.BlockSpec((None, tm, PLE_DIM), lambda i: (layer, i, 0)),
                  wspec((D_MODEL, D_MODEL)),
                  wspec((1, D_MODEL)),
                  wspec((D_MODEL, D_FF)),
                  wspec((D_FF, D_MODEL)),
                  wspec((1, D_MODEL)),
                  wspec((D_MODEL, D_MODEL)),
                  wspec((PLE_DIM, D_MODEL)),
                  wspec((1, D_MODEL))],
        out_specs=pl.BlockSpec((tm, D_MODEL), row),
        out_shape=jax.ShapeDtypeStruct((n, D_MODEL), F32),
        compiler_params=_params(1),
        name="tail",
    )(x, mm, mh, ms, p, w_out, g2, w_up, w_down, g3, w_gate, w_proj, g4)


def kernel(x, p, positions, in_norm_g, w_in, b_in, mlstm_f_bias, mlstm_conv_w, mlstm_conv_b,
           mlstm_norm_g, hgrn_lb_logits, hgrn_norm_g, swa_q_norm_g, swa_k_norm_g, swa_sinks,
           w_out, mlp_norm_g, w_up, w_down, ple_norm_g, w_ple_gate, w_ple_proj, ple_post_norm_g):
    batch, seq, d_model = x.shape
    depth = w_in.shape[0]
    n = batch * seq
    assert d_model == D_MODEL and seq % max(MIX_ROWS, IN_TM) == 0

    n_raw = GATE_OFF + GATE_COLS
    pad = LANES - GATE_COLS
    w_in_p = jnp.concatenate(
        [w_in[..., :n_raw], jnp.zeros((depth, D_MODEL, pad), w_in.dtype), w_in[..., n_raw:]],
        axis=-1).astype(BF16)
    b_in_p = jnp.concatenate(
        [b_in[..., :n_raw], jnp.zeros((depth, pad), b_in.dtype), b_in[..., n_raw:]],
        axis=-1).astype(F32).reshape(depth, 1, IN_COLS)
    f_bias = jnp.zeros((depth, 1, LANES), F32).at[:, 0, N_HEADS:GATE_COLS].set(mlstm_f_bias.astype(F32))
    row3 = lambda a: a.astype(F32).reshape(depth, 1, a.shape[-1])
    q_gain = row3(jnp.tile(swa_q_norm_g, (1, LANES // HEAD_DIM)))
    k_gain = row3(jnp.tile(swa_k_norm_g, (1, LANES // HEAD_DIM)))
    w_out_b, w_up_b, w_down_b = w_out.astype(BF16), w_up.astype(BF16), w_down.astype(BF16)
    w_gate_b, w_proj_b = w_ple_gate.astype(BF16), w_ple_proj.astype(BF16)

    blk256 = _blk_ones(GROUP_W)
    tril = jnp.asarray(np.tril(np.ones((MLSTM_T, MLSTM_T), np.float32)), dtype=BF16)
    expand = jnp.asarray(
        (np.arange(LANES)[:, None] == (np.arange(GROUP_W) // HEAD_DIM)[None, :]).astype(np.float32),
        dtype=BF16)
    hgrn_tril, hgrn_pmask = _hgrn_constants()

    cos_t, sin_t = _rope_tables(positions)
    la, l1m, oml = [row3(a) for a in _hgrn_lower_bounds(hgrn_lb_logits)]
    sinks = swa_sinks.astype(F32)

    xf = x.reshape(n, D_MODEL)
    for l in range(depth):
        ml, gt, hb, lf, sw = _in_proj(
            xf, row3(in_norm_g), w_in_p, b_in_p, mlstm_conv_w.astype(F32), row3(mlstm_conv_b),
            la, l1m, oml, cos_t, sin_t, q_gain, k_gain, blk256, l, seq)
        mm, mh, ms = _mixers(ml, gt, hb, lf, sw, f_bias, row3(mlstm_norm_g), row3(hgrn_norm_g), sinks,
                             l, batch, (blk256, tril, expand, hgrn_tril, hgrn_pmask))
        xf = _tail(xf, mm, mh, ms, p.reshape(depth, n, PLE_DIM), w_out_b, row3(mlp_norm_g), w_up_b,
                   w_down_b, row3(ple_norm_g), w_gate_b, w_proj_b, row3(ple_post_norm_g), l)
    return xf.reshape(batch, seq, D_MODEL)
```

```python
import functools

import numpy as np
import jax
import jax.numpy as jnp
from jax import lax
from jax.experimental import pallas as pl
from jax.experimental.pallas import tpu as pltpu

F32 = jnp.float32
BF16 = jnp.bfloat16

D_MODEL = 1024
N_HEADS = 4
HEAD_DIM = 64
GROUP_W = N_HEADS * HEAD_DIM
CONV_K = 4
SWA_Q_HEADS = 8
SWA_KV_HEADS = 2
SWA_GROUP = SWA_Q_HEADS // SWA_KV_HEADS
WINDOW = 128
ROPE_THETA = 500000.0
ROT_DIM = HEAD_DIM // 4
ROT_HALF = ROT_DIM // 2
D_FF = 4 * D_MODEL
PLE_DIM = 256
EPS = 1e-6
LOG2_E = 1.4426950408889634

LANES = 128
SUBLANES = 8
VMEM_LIMIT_BYTES = 56 * 1024 * 1024

GATE_COLS = 2 * N_HEADS
M_COLS = 4 * GROUP_W + LANES
H_COLS = 4 * GROUP_W
S_COLS = SWA_Q_HEADS * HEAD_DIM + 2 * SWA_KV_HEADS * HEAD_DIM
IN_COLS = M_COLS + H_COLS + S_COLS
GATE_OFF = 4 * GROUP_W

IN_TM = 512
TAIL_TM = 512
ROPE_TM = 2048
MIX_ROWS = 512
MIX_BATCH = 1
MLSTM_T = 256
HGRN_L = 64
SWA_T = MIX_ROWS

N_SPLIT = 2
assert HGRN_L == HEAD_DIM


def _params(n_grid_dims, flags=None):
    return pltpu.CompilerParams(
        dimension_semantics=("arbitrary",) * n_grid_dims,
        vmem_limit_bytes=VMEM_LIMIT_BYTES,
        flags=flags)


def _dot(a, b):
    return jnp.dot(a, b, preferred_element_type=F32)


def _dot_nt(a, b):
    return lax.dot_general(a, b, (((1,), (1,)), ((), ())), preferred_element_type=F32)


def _dot_tn(a, b):
    return lax.dot_general(a, b, (((0,), (0,)), ((), ())), preferred_element_type=F32)


def _split(x, n=N_SPLIT):
    pieces = []
    r = x
    for _ in range(n):
        p = r.astype(BF16)
        pieces.append(p)
        r = r - p.astype(F32)
    return pieces


def _dot_const_left(m, x):
    acc = None
    for p in _split(x):
        t = _dot(m, p)
        acc = t if acc is None else acc + t
    return acc


def _dot_const_right(x, m):
    acc = None
    for p in _split(x):
        t = _dot(p, m)
        acc = t if acc is None else acc + t
    return acc


def _sigmoid(x):
    return 1.0 / (1.0 + jnp.exp(-x))


def _log_sigmoid(x):
    return jnp.minimum(x, 0.0) - jnp.log1p(jnp.exp(-jnp.abs(x)))


def _rms(x, g):
    return x * lax.rsqrt(jnp.mean(x * x, axis=-1, keepdims=True) + EPS) * g


def _head_rms(x, g, blk_ones):
    ms = _dot(jnp.square(x).astype(BF16), blk_ones) * (1.0 / HEAD_DIM)
    return x * lax.rsqrt(ms + EPS) * g


def _lane_head_id(shape):
    return lax.broadcasted_iota(jnp.int32, shape, len(shape) - 1) // HEAD_DIM


def _blk_ones(width):
    hid = np.arange(width) // HEAD_DIM
    return jnp.asarray((hid[:, None] == hid[None, :]).astype(np.float32), dtype=BF16)


def _hgrn_level_sizes():
    sizes = []
    c = 2
    while c <= HGRN_L:
        sizes.append(c)
        c *= 2
    return sizes


def _hgrn_constants():
    L = HGRN_L
    t = np.arange(L)[:, None]
    u = np.arange(L)[None, :]
    masks = [(t == u)]
    for c in _hgrn_level_sizes():
        mid = (t // c) * c + c // 2 - 1
        same_blk = (t // c) == (u // c)
        u_mid = (u // c) * c + c // 2 - 1
        masks.append(same_blk & (t > mid) & (u <= u_mid))
    tril = (u <= t).astype(np.float32)
    pmask = np.stack([np.tile(m.astype(np.float32), (1, N_HEADS)) for m in masks])
    return jnp.asarray(tril, dtype=BF16), jnp.asarray(pmask, dtype=F32)


def _rope_kernel(pos_ref, pat_ref, cos_ref, sin_ref):
    ang = pos_ref[...].astype(F32) * pat_ref[0:1, :]
    cos_ref[...] = jnp.cos(ang)
    sin_ref[...] = jnp.sin(ang) * pat_ref[1:2, :]


def _rope_tables(positions):
    n = positions.size
    inv_freq = ROPE_THETA ** (-jnp.arange(0, ROT_DIM, 2, dtype=F32) / ROT_DIM)
    lane = np.arange(LANES) % HEAD_DIM
    freq = jnp.where(lane < ROT_DIM, inv_freq[lane % ROT_HALF], 0.0)
    sign = np.where(lane < ROT_HALF, -1.0, np.where(lane < ROT_DIM, 1.0, 0.0)).astype(np.float32)
    pat = jnp.zeros((SUBLANES, LANES), F32).at[0].set(freq).at[1].set(sign)
    tm = min(ROPE_TM, n)
    out = jax.ShapeDtypeStruct((n, LANES), F32)
    return pl.pallas_call(
        _rope_kernel,
        grid=(n // tm,),
        in_specs=[pl.BlockSpec((tm, 1), lambda i: (i, 0)),
                  pl.BlockSpec((SUBLANES, LANES), lambda i: (0, 0))],
        out_specs=[pl.BlockSpec((tm, LANES), lambda i: (i, 0))] * 2,
        out_shape=[out, out],
        compiler_params=_params(1),
        name="rope_tables",
    )(positions.reshape(n, 1), pat)


def _lb_kernel(lg_ref, la_ref, l1m_ref, oml_ref):
    z = lg_ref[...]
    e = jnp.exp(z - jnp.max(z, axis=0, keepdims=True))
    sm = e / jnp.sum(e, axis=0, keepdims=True)
    depth = z.shape[0]
    run = sm[0:1, :]
    first = run
    for l in range(depth):
        if l > 0:
            run = run + sm[l:l + 1, :]
        lb = run - first
        la_ref[l:l + 1, :] = jnp.log(lb)
        l1m_ref[l:l + 1, :] = jnp.log1p(-lb)
        oml_ref[l:l + 1, :] = 1.0 - lb


def _hgrn_lower_bounds(logits):
    out = jax.ShapeDtypeStruct(logits.shape, F32)
    return pl.pallas_call(
        _lb_kernel, out_shape=[out, out, out], name="hgrn_lower_bounds",
    )(logits.astype(F32))


def _in_proj_kernel(x_ref, g_ref, w_ref, b_ref, cw_ref, cb_ref, la_ref, l1m_ref, oml_ref,
                    cos_ref, sin_ref, qg_ref, kg_ref, ones_ref,
                    ml_ref, gt_ref, hb_ref, lf_ref, sw_ref, xbuf, *, tiles_per_seq):
    TM = x_ref.shape[0]
    W = GROUP_W

    @pl.when(pl.program_id(0) % tiles_per_seq == 0)
    def _():
        xbuf[0:SUBLANES, :] = jnp.zeros((SUBLANES, 2 * W), F32)

    h = _rms(x_ref[...], g_ref[...]).astype(BF16)

    def proj(lo, width):
        return _dot(h, w_ref[:, lo:lo + width]) + b_ref[:, lo:lo + width]

    cos_t = cos_ref[...]
    sin_t = sin_ref[...]
    blk = ones_ref[...]
    lane = lax.broadcasted_iota(jnp.int32, (TM, LANES), 1)
    low_half = lane < HEAD_DIM
    rot_first = (lane % HEAD_DIM) < ROT_HALF
    s0 = M_COLS + H_COLS
    QW = SWA_Q_HEADS * HEAD_DIM

    def head_sumsq(y):
        width = y.shape[-1]
        return _dot(jnp.square(y).astype(BF16), blk[0:width, 0:width])

    def norm_rope(v, sumsq, gain):
        xn = v * lax.rsqrt(sumsq * (1.0 / HEAD_DIM) + EPS) * gain
        partner = jnp.where(rot_first, pltpu.roll(xn, LANES - ROT_HALF, 1), pltpu.roll(xn, ROT_HALF, 1))
        return xn * cos_t + partner * sin_t

    def swa_q(c, y, sumsq):
        for half in range(2):
            sl = slice(half * LANES, (half + 1) * LANES)
            sw_ref[:, (c + half) * LANES:(c + half + 1) * LANES] = (
                norm_rope(y[:, sl], sumsq[:, sl], qg_ref[...]) * (HEAD_DIM ** -0.5 * LOG2_E)).astype(BF16)

    def swa_kv(y, sumsq):
        k = norm_rope(y[:, 0:LANES], sumsq, kg_ref[...])
        v = y[:, LANES:2 * LANES]
        k_sw = pltpu.roll(k, HEAD_DIM, 1)
        v_sw = pltpu.roll(v, HEAD_DIM, 1)
        sw_ref[:, QW:QW + LANES] = jnp.where(low_half, k, k_sw).astype(BF16)
        sw_ref[:, QW + LANES:QW + 2 * LANES] = jnp.where(low_half, k_sw, k).astype(BF16)
        sw_ref[:, QW + 2 * LANES:QW + 3 * LANES] = jnp.where(low_half, v, v_sw).astype(BF16)
        sw_ref[:, QW + 3 * LANES:QW + 4 * LANES] = jnp.where(low_half, v_sw, v).astype(BF16)

    def mlstm_qk(y):
        xbuf[SUBLANES:SUBLANES + TM, :] = y
        acc = cb_ref[...] + cw_ref[CONV_K - 1:CONV_K, :] * xbuf[SUBLANES:SUBLANES + TM, :]
        for j in range(1, CONV_K):
            acc = acc + cw_ref[CONV_K - 1 - j:CONV_K - j, :] * xbuf[SUBLANES - j:SUBLANES - j + TM, :]
        xbuf[0:SUBLANES, :] = xbuf[TM:TM + SUBLANES, :]
        qk = acc * _sigmoid(acc)
        ml_ref[:, 0:W] = qk[:, 0:W].astype(BF16)
        ml_ref[:, W:2 * W] = (qk[:, W:2 * W] * (HEAD_DIM ** -0.5)).astype(BF16)

    def hgrn_qf(y):
        hq = y[:, 0:W]
        hf = y[:, W:2 * W]
        la = la_ref[...]
        u = jnp.exp(-jnp.abs(hf))
        t = l1m_ref[...] + (jnp.minimum(hf, 0.0) - jnp.log1p(u))
        lf_ref[...] = jnp.maximum(la, t) + jnp.log1p(jnp.exp(-jnp.abs(la - t)))
        hb_ref[:, 0:W] = (hq * _sigmoid(hq)).astype(BF16)
        hb_ref[:, W:2 * W] = (oml_ref[...] * jnp.where(hf >= 0.0, u, 1.0) / (1.0 + u)).astype(BF16)

    y_h = proj(M_COLS, 2 * W)
    y_m = proj(0, 2 * W)
    hgrn_qf(y_h)
    y_kv = proj(s0 + QW, 2 * LANES)
    mlstm_qk(y_m)
    y_q0 = proj(s0, 2 * LANES)
    swa_kv(y_kv, head_sumsq(y_kv[:, 0:LANES]))
    y_q1 = proj(s0 + 2 * LANES, 2 * LANES)
    swa_q(0, y_q0, head_sumsq(y_q0))
    y_h2 = proj(M_COLS + 2 * W, 2 * W)
    swa_q(2, y_q1, head_sumsq(y_q1))
    y_m2 = proj(2 * W, 2 * W)
    hg = y_h2[:, W:2 * W]
    hb_ref[:, 2 * W:3 * W] = y_h2[:, 0:W].astype(BF16)
    hb_ref[:, 3 * W:4 * W] = (hg * _sigmoid(hg)).astype(BF16)
    gt_ref[...] = proj(GATE_OFF, LANES)
    ml_ref[:, 2 * W:3 * W] = y_m2[:, 0:W].astype(BF16)
    ml_ref[:, 3 * W:4 * W] = _sigmoid(y_m2[:, W:2 * W]).astype(BF16)


def _in_proj(x, g, w, b, conv_w, conv_b, la, l1m, oml, cos_t, sin_t, q_gain, k_gain, blk_ones,
             layer, seq):
    n = x.shape[0]
    tm = min(IN_TM, n)
    row = lambda i: (i, 0)
    lay = lambda i: (layer, 0, 0)
    W4 = 4 * GROUP_W
    return pl.pallas_call(
        functools.partial(_in_proj_kernel, tiles_per_seq=seq // tm),
        grid=(n // tm,),
        in_specs=[pl.BlockSpec((tm, D_MODEL), row),
                  pl.BlockSpec((None, 1, D_MODEL), lay),
                  pl.BlockSpec((None, D_MODEL, IN_COLS), lay),
                  pl.BlockSpec((None, 1, IN_COLS), lay),
                  pl.BlockSpec((None, CONV_K, 2 * GROUP_W), lay),
                  pl.BlockSpec((None, 1, 2 * GROUP_W), lay),
                  pl.BlockSpec((None, 1, GROUP_W), lay),
                  pl.BlockSpec((None, 1, GROUP_W), lay),
                  pl.BlockSpec((None, 1, GROUP_W), lay),
                  pl.BlockSpec((tm, LANES), row),
                  pl.BlockSpec((tm, LANES), row),
                  pl.BlockSpec((None, 1, LANES), lay),
                  pl.BlockSpec((None, 1, LANES), lay),
                  pl.BlockSpec((GROUP_W, GROUP_W), lambda i: (0, 0))],
        out_specs=[pl.BlockSpec((tm, W4), row),
                   pl.BlockSpec((tm, LANES), row),
                   pl.BlockSpec((tm, W4), row),
                   pl.BlockSpec((tm, GROUP_W), row),
                   pl.BlockSpec((tm, W4), row)],
        out_shape=[jax.ShapeDtypeStruct((n, W4), BF16),
                   jax.ShapeDtypeStruct((n, LANES), F32),
                   jax.ShapeDtypeStruct((n, W4), BF16),
                   jax.ShapeDtypeStruct((n, GROUP_W), F32),
                   jax.ShapeDtypeStruct((n, W4), BF16)],
        scratch_shapes=[pltpu.VMEM((tm + SUBLANES, 2 * GROUP_W), F32)],
        compiler_params=_params(1),
        name="in_proj",
    )(x, g, w, b, conv_w, conv_b, la, l1m, oml, cos_t, sin_t, q_gain, k_gain, blk_ones)


def _mlstm_gates(r0, gt_ref, fb_ref, tril_ref, m_prev):
    T = MLSTM_T
    g8 = gt_ref[r0:r0 + T, :].T[0:SUBLANES, :]
    head_row8 = lax.broadcasted_iota(jnp.int32, (SUBLANES, T), 0) < N_HEADS
    lane8 = lax.broadcasted_iota(jnp.int32, (SUBLANES, T), 1)
    fb8 = jnp.concatenate([fb_ref[...]] * (T // LANES), axis=1)
    lf = pltpu.roll(_log_sigmoid(g8 + fb8), N_HEADS, 0)
    lf_hi = lf.astype(BF16)
    lf_lo = (lf - lf_hi.astype(F32)).astype(BF16)
    bb = _dot_nt(jnp.concatenate([lf_hi, lf_lo], axis=0), tril_ref[...])
    b = bb[0:SUBLANES] + bb[SUBLANES:2 * SUBLANES]
    a = g8 - b
    cm = a
    sh = 1
    while sh < T:
        cm = jnp.maximum(cm, jnp.where(lane8 >= sh, pltpu.roll(cm, sh, 1), -jnp.inf))
        sh *= 2
    m_j = b + jnp.maximum(m_prev, cm)
    m_new = m_j[:, T - 1:T]
    b_last = b[:, T - 1:T]
    w_inter = jnp.exp(b + m_prev - m_j)
    e_den = jnp.exp(-m_j)
    w_s = jnp.exp(b_last + a - m_new)
    decay = jnp.exp(b_last + m_prev - m_new)
    cexp = (b - m_j) * LOG2_E
    a_t = a * LOG2_E

    live = [jnp.where(head_row8, x, 0.0) for x in (cexp, w_inter, e_den, w_s)]
    cols = jnp.concatenate(live + [jnp.zeros((LANES - 4 * SUBLANES, T), F32)], axis=0).T
    return cols, a_t, jnp.where(head_row8[:, 0:1], decay, 0.0), m_new


def _mlstm_chunk(r0, gates, y_ref, g_ref, ones_ref, exp_ref, o_ref, c_st, n_st):
    T = MLSTM_T
    W = GROUP_W
    cols, a_t, decay, _ = gates
    qb = y_ref[r0:r0 + T, 0:W]
    kb = y_ref[r0:r0 + T, W:2 * W]
    vb = y_ref[r0:r0 + T, 2 * W:3 * W]
    out_gate = y_ref[r0:r0 + T, 3 * W:4 * W]

    ex = _dot(cols.astype(BF16), exp_ref[...])
    w_inter_x = ex[:, 0:W]
    e_den_x = ex[:, W:2 * W]
    w_s_x = ex[:, 2 * W:3 * W]
    blk = ones_ref[...]
    head_lanes8 = jnp.concatenate(
        [blk[h * HEAD_DIM:h * HEAD_DIM + 1, :] for h in range(N_HEADS)]
        + [jnp.zeros((SUBLANES - N_HEADS, W), BF16)], axis=0).astype(F32)
    decay_x = jnp.sum(decay * head_lanes8, axis=0, keepdims=True)

    hid = _lane_head_id((T, W))
    causal = (lax.broadcasted_iota(jnp.int32, (T, T), 1)
              <= lax.broadcasted_iota(jnp.int32, (T, T), 0))
    head_rows = [blk[h * HEAD_DIM:h * HEAD_DIM + 1, :] for h in range(N_HEADS)]
    s_all = _dot_nt(jnp.concatenate([qb * hr for hr in head_rows], axis=0), kb)
    num = None
    den = None
    for h in range(N_HEADS):
        dm = cols[:, h:h + 1] + a_t[h:h + 1, :]
        s = s_all[h * T:(h + 1) * T] * jnp.exp2(jnp.where(causal, dm, -jnp.inf))
        dn = _dot(s.astype(BF16), vb * head_rows[h])
        dd = jnp.where(hid == h, jnp.sum(s, axis=-1, keepdims=True), 0.0)
        num = dn if num is None else num + dn
        den = dd if den is None else den + dd
    n_row = n_st[0:1, :].astype(BF16)
    num = num + w_inter_x * _dot(qb, c_st[...].astype(BF16))
    den = den + w_inter_x * _dot(qb * n_row, blk)
    hh = num / jnp.maximum(jnp.abs(den), e_den_x)

    o_ref[r0:r0 + T, :] = _head_rms(hh, g_ref[...], blk) * out_gate.astype(F32)

    kw = kb * w_s_x.astype(BF16)
    c_st[...] = decay_x * c_st[...] + blk.astype(F32) * _dot_tn(kw, vb)
    n_st[...] = decay_x * n_st[...] + _dot(jnp.ones((SUBLANES, T), BF16), kw)


def _hgrn_chunk(r0, y_ref, lf_ref, g_ref, ones_ref, mst_ref, pm_ref, o_ref, s_st):
    L = HGRN_L
    W = GROUP_W
    gain = g_ref[...]
    blk = ones_ref[...]
    blk_f = blk.astype(F32)
    tril = mst_ref[...]
    row = lax.broadcasted_iota(jnp.int32, (L, W), 0)

    def level_gap(g, lf2, c):
        if c == 2:
            return jnp.where(row % 2 == 1, lf2, 0.0)
        if c >= SUBLANES:
            mids = [b * c + c // 2 - 1 for b in range(L // c)]
            return g - jnp.concatenate(
                [jnp.broadcast_to(g[m:m + 1, :], (c, W)) for m in mids], axis=0)
        lo = jnp.concatenate([jnp.broadcast_to(g[m:m + 1, :], (SUBLANES, W))
                              for m in range(1, L, SUBLANES)], axis=0)
        hi = jnp.concatenate([jnp.broadcast_to(g[m:m + 1, :], (SUBLANES, W))
                              for m in range(5, L, SUBLANES)], axis=0)
        return g - jnp.where(row % SUBLANES < 4, lo, hi)

    def head_stack(xb):
        return jnp.concatenate([xb] * N_HEADS, axis=0) * blk

    if True:
        qf = y_ref[r0:r0 + L, 0:W]
        key = y_ref[r0:r0 + L, W:2 * W]
        vb = y_ref[r0:r0 + L, 2 * W:3 * W]
        out_gate = y_ref[r0:r0 + L, 3 * W:4 * W]

        lf2 = lf_ref[r0:r0 + L, :] * LOG2_E
        g = _dot_const_left(tril, lf2)
        e_last = jnp.exp2(g[L - 1:L, :])
        q_hat = qf * jnp.exp2(g).astype(BF16)
        k_hat = key * jnp.exp2(g[L - 1:L, :] - g).astype(BF16)

        a = pm_ref[0] * _dot_nt(qf, head_stack(key))
        for i, c in enumerate(_hgrn_level_sizes()):
            e_i = jnp.exp2(-jnp.abs(level_gap(g, lf2, c))).astype(BF16)
            a = a + pm_ref[1 + i] * _dot_nt(qf * e_i, head_stack(key * e_i))
        o = _dot(a.astype(BF16), head_stack(vb)) + _dot_nt(q_hat, s_st[...].astype(BF16))

        o_ref[r0:r0 + L, :] = _head_rms(o, gain, blk) * out_gate.astype(F32)
        s_st[...] = s_st[...] * e_last + blk_f * _dot_tn(vb, k_hat)


def _swa_blocks(sink_ref, y_ref, o_ref, k_prev, v_prev, layer):
    T = SWA_T
    Wn = WINDOW
    QW = SWA_Q_HEADS * HEAD_DIM
    step = pl.program_id(1)

    k2 = [jnp.concatenate([k_prev[g], y_ref[:, QW + g * LANES:QW + (g + 1) * LANES]], axis=0)
          for g in range(SWA_KV_HEADS)]
    v2 = [jnp.concatenate([v_prev[g], y_ref[:, QW + (2 + g) * LANES:QW + (3 + g) * LANES]], axis=0)
          for g in range(SWA_KV_HEADS)]
    for g in range(SWA_KV_HEADS):
        k_prev[g] = k2[g][T:T + Wn]
        v_prev[g] = v2[g][T:T + Wn]

    qi = lax.broadcasted_iota(jnp.int32, (Wn, 2 * Wn), 0)
    ki = lax.broadcasted_iota(jnp.int32, (Wn, 2 * Wn), 1)
    band = (ki > qi) & (ki <= qi + Wn)
    band_first = band & (ki >= jnp.where(step > 0, 0, Wn))
    lane_w = lax.broadcasted_iota(jnp.int32, (Wn, LANES), 1) < HEAD_DIM
    lane_row = lax.broadcasted_iota(jnp.int32, (1, LANES), 1) < HEAD_DIM
    low_row = jnp.where(lane_row, 1.0, 0.0).astype(BF16)
    high_row = jnp.where(lane_row, 0.0, 1.0).astype(BF16)

    def scores(j, g):
        r0 = j * Wn
        kc = k2[g][r0:r0 + 2 * Wn]
        parts = []
        for c in (2 * g, 2 * g + 1):
            qc = y_ref[r0:r0 + Wn, c * LANES:(c + 1) * LANES]
            parts.append(qc * low_row)
            parts.append(qc * high_row)
        return _dot_nt(jnp.concatenate(parts, axis=0), kc)

    def finish(j, g, s):
        r0 = j * Wn
        mask = band_first if j == 0 else band
        if True:
            vc = v2[g][r0:r0 + 2 * Wn]
            outs = []
            for r in range(SWA_GROUP):
                sink = sink_ref[layer, SWA_GROUP * g + r] * LOG2_E
                sr = jnp.where(mask, s[r * Wn:(r + 1) * Wn], -jnp.inf)
                m = jnp.maximum(jnp.max(sr, axis=-1, keepdims=True), sink)
                pexp = jnp.exp2(sr - m)
                den = jnp.sum(pexp, axis=-1, keepdims=True) + jnp.exp2(sink - m)
                outs.append(_dot(pexp.astype(BF16), vc) / den)
            c0 = 2 * g * LANES
            o_ref[r0:r0 + Wn, c0:c0 + LANES] = jnp.where(lane_w, outs[0], outs[1])
            o_ref[r0:r0 + Wn, c0 + LANES:c0 + 2 * LANES] = jnp.where(lane_w, outs[2], outs[3])

    return [(functools.partial(scores, j, g), functools.partial(finish, j, g))
            for j in range(T // Wn) for g in range(SWA_KV_HEADS)]


def _mixers_kernel(sink_ref, ml_ref, gt_ref, fb_ref, gm_ref, hb_ref, lf_ref, gh_ref, sw_ref,
                   ones_ref, tril_ref, exp_ref, tril64_ref, pm_ref,
                   om_ref, oh_ref, os_ref,
                   c_st, n_st, m_st, s_st, k_prev, v_prev, *, layer):
    @pl.when(pl.program_id(1) == 0)
    def _():
        for ref in (c_st, n_st, m_st, s_st, k_prev, v_prev):
            ref[...] = jnp.zeros_like(ref)

    n_sub = MIX_ROWS // HGRN_L
    per_mlstm = n_sub // (MIX_ROWS // MLSTM_T)
    swa_blocks = [_swa_blocks(sink_ref, sw_ref.at[bb], os_ref.at[bb], k_prev.at[bb], v_prev.at[bb], layer)
                  for bb in range(MIX_BATCH)]
    assert len(swa_blocks[0]) == n_sub and n_sub % (MIX_ROWS // MLSTM_T) == 0
    scores = [swa_blocks[bb][0][0]() for bb in range(MIX_BATCH)]
    gates = []
    for bb in range(MIX_BATCH):
        m_run = m_st.at[bb][:, 0:1]
        per_chunk = []
        for ci in range(MIX_ROWS // MLSTM_T):
            per_chunk.append(_mlstm_gates(ci * MLSTM_T, gt_ref.at[bb], fb_ref, tril_ref, m_run))
            m_run = per_chunk[-1][3]
        m_st.at[bb][...] = jnp.broadcast_to(m_run, (SUBLANES, LANES))
        gates.append(per_chunk)
    for sub in range(n_sub):
        if sub > 0:
            scores = [swa_blocks[bb][sub][0]() for bb in range(MIX_BATCH)]
        for bb in range(MIX_BATCH):
            _hgrn_chunk(sub * HGRN_L, hb_ref.at[bb], lf_ref.at[bb], gh_ref, ones_ref, tril64_ref, pm_ref,
                        oh_ref.at[bb], s_st.at[bb])
        for bb in range(MIX_BATCH):
            swa_blocks[bb][sub][1](scores[bb])
        if sub % per_mlstm == per_mlstm - 1:
            ci = sub // per_mlstm
            for bb in range(MIX_BATCH):
                _mlstm_chunk(ci * MLSTM_T, gates[bb][ci], ml_ref.at[bb], gm_ref, ones_ref, exp_ref,
                             om_ref.at[bb], c_st.at[bb], n_st.at[bb])


def _mixers(ml, gt, hb, lf, sw, f_bias, mlstm_g, hgrn_g, sinks, layer, batch, consts):
    n = ml.shape[0]
    seq = n // batch
    T = MIX_ROWS
    NB = MIX_BATCH
    steps = seq // T
    W4 = 4 * GROUP_W
    QW = SWA_Q_HEADS * HEAD_DIM
    row = lambda bi, ci: (bi, ci, 0)
    lay = lambda bi, ci: (layer, 0, 0)
    cst = lambda bi, ci: (0, 0)
    seq3 = lambda a: a.reshape(batch, seq, a.shape[-1])
    blk_ones, tril, expand, tril64, pmask = consts
    mm, mh, ms = pl.pallas_call(
        functools.partial(_mixers_kernel, layer=layer),
        grid=(batch // NB, steps),
        in_specs=[pl.BlockSpec(memory_space=pltpu.SMEM),
                  pl.BlockSpec((NB, T, W4), row),
                  pl.BlockSpec((NB, T, LANES), row),
                  pl.BlockSpec((None, SUBLANES, LANES), lay),
                  pl.BlockSpec((None, 1, GROUP_W), lay),
                  pl.BlockSpec((NB, T, W4), row),
                  pl.BlockSpec((NB, T, GROUP_W), row),
                  pl.BlockSpec((None, 1, GROUP_W), lay),
                  pl.BlockSpec((NB, T, W4), row),
                  pl.BlockSpec((GROUP_W, GROUP_W), cst),
                  pl.BlockSpec((MLSTM_T, MLSTM_T), cst),
                  pl.BlockSpec((LANES, 3 * GROUP_W), cst),
                  pl.BlockSpec((HGRN_L, HGRN_L), cst),
                  pl.BlockSpec(pmask.shape, lambda bi, ci: (0, 0, 0))],
        out_specs=[pl.BlockSpec((NB, T, GROUP_W), row),
                   pl.BlockSpec((NB, T, GROUP_W), row),
                   pl.BlockSpec((NB, T, QW), row)],
        out_shape=[jax.ShapeDtypeStruct((batch, seq, GROUP_W), F32),
                   jax.ShapeDtypeStruct((batch, seq, GROUP_W), F32),
                   jax.ShapeDtypeStruct((batch, seq, QW), F32)],
        scratch_shapes=[pltpu.VMEM((NB, GROUP_W, GROUP_W), F32),
                        pltpu.VMEM((NB, SUBLANES, GROUP_W), F32),
                        pltpu.VMEM((NB, SUBLANES, LANES), F32),
                        pltpu.VMEM((NB, GROUP_W, GROUP_W), F32),
                        pltpu.VMEM((NB, SWA_KV_HEADS, WINDOW, LANES), BF16),
                        pltpu.VMEM((NB, SWA_KV_HEADS, WINDOW, LANES), BF16)],
        compiler_params=_params(2),
        name="mixers",
    )(sinks, seq3(ml), seq3(gt), f_bias, mlstm_g, seq3(hb), seq3(lf), hgrn_g, seq3(sw),
      blk_ones, tril, expand, tril64, pmask)
    return mm.reshape(n, GROUP_W), mh.reshape(n, GROUP_W), ms.reshape(n, QW)


def _tail_kernel(x_ref, mm_ref, mh_ref, ms_ref, p_ref, wo_ref, g2_ref, wu_ref, wd_ref, g3_ref,
                 wg_ref, wp_ref, g4_ref, o_ref):
    W = GROUP_W
    mix = _dot(mm_ref[...].astype(BF16), wo_ref[0:W, :])
    mix = mix + _dot(mh_ref[...].astype(BF16), wo_ref[W:2 * W, :])
    mix = mix + _dot(ms_ref[...].astype(BF16), wo_ref[2 * W:4 * W, :])
    x = x_ref[...] + mix

    h2 = _rms(x, g2_ref[...]).astype(BF16)
    mlp = None
    for c in range(D_FF // D_MODEL):
        u = _dot(h2, wu_ref[:, c * D_MODEL:(c + 1) * D_MODEL])
        act = jnp.square(jnp.maximum(u, 0.0)).astype(BF16)
        d = _dot(act, wd_ref[c * D_MODEL:(c + 1) * D_MODEL, :])
        mlp = d if mlp is None else mlp + d
    x = x + mlp

    gate = _sigmoid(_dot(_rms(x, g3_ref[...]).astype(BF16), wg_ref[...]))
    emb = _rms(_dot(p_ref[...].astype(BF16), wp_ref[...]), g4_ref[...])
    o_ref[...] = x + gate * emb


def _tail(x, mm, mh, ms, p, w_out, g2, w_up, w_down, g3, w_gate, w_proj, g4, layer):
    n = x.shape[0]
    tm = min(TAIL_TM, n)
    tiles = n // tm
    row = lambda i: (i, 0)
    lay = lambda i: (layer, 0, 0)
    once = pl.Buffered(1)

    def wspec(shape):
        return pl.BlockSpec((None,) + shape, lay, pipeline_mode=once)

    return pl.pallas_call(
        _tail_kernel,
        grid=(tiles,),
        in_specs=[pl.BlockSpec((tm, D_MODEL), row),
                  pl.BlockSpec((tm, GROUP_W), row),
                  pl.BlockSpec((tm, GROUP_W), row),
                  pl.BlockSpec((tm, 2 * GROUP_W), row),
                  pl.BlockSpec((None, tm, PLE_DIM), lambda i: (layer, i, 0)),
                  wspec((D_MODEL, D_MODEL)),
                  wspec((1, D_MODEL)),
                  wspec((D_MODEL, D_FF)),
                  wspec((D_FF, D_MODEL)),
                  wspec((1, D_MODEL)),
                  wspec((D_MODEL, D_MODEL)),
                  wspec((PLE_DIM, D_MODEL)),
                  wspec((1, D_MODEL))],
        out_specs=pl.BlockSpec((tm, D_MODEL), row),
        out_shape=jax.ShapeDtypeStruct((n, D_MODEL), F32),
        compiler_params=_params(1),
        name="tail",
    )(x, mm, mh, ms, p, w_out, g2, w_up, w_down, g3, w_gate, w_proj, g4)


def kernel(x, p, positions, in_norm_g, w_in, b_in, mlstm_f_bias, mlstm_conv_w, mlstm_conv_b,
           mlstm_norm_g, hgrn_lb_logits, hgrn_norm_g, swa_q_norm_g, swa_k_norm_g, swa_sinks,
           w_out, mlp_norm_g, w_up, w_down, ple_norm_g, w_ple_gate, w_ple_proj, ple_post_norm_g):
    batch, seq, d_model = x.shape
    depth = w_in.shape[0]
    n = batch * seq
    assert d_model == D_MODEL and seq % max(MIX_ROWS, IN_TM) == 0 and batch % MIX_BATCH == 0

    n_raw = GATE_OFF + GATE_COLS
    pad = LANES - GATE_COLS
    w_in_p = jnp.concatenate(
        [w_in[..., :n_raw], jnp.zeros((depth, D_MODEL, pad), w_in.dtype), w_in[..., n_raw:]],
        axis=-1).astype(BF16)
    b_in_p = jnp.concatenate(
        [b_in[..., :n_raw], jnp.zeros((depth, pad), b_in.dtype), b_in[..., n_raw:]],
        axis=-1).astype(F32).reshape(depth, 1, IN_COLS)
    f_bias = jnp.zeros((depth, SUBLANES, LANES), F32).at[:, N_HEADS:GATE_COLS, :].set(
        mlstm_f_bias.astype(F32)[:, :, None])
    row3 = lambda a: a.astype(F32).reshape(depth, 1, a.shape[-1])
    q_gain = row3(jnp.tile(swa_q_norm_g, (1, LANES // HEAD_DIM)))
    k_gain = row3(jnp.tile(swa_k_norm_g, (1, LANES // HEAD_DIM)))
    w_out_b, w_up_b, w_down_b = w_out.astype(BF16), w_up.astype(BF16), w_down.astype(BF16)
    w_gate_b, w_proj_b = w_ple_gate.astype(BF16), w_ple_proj.astype(BF16)

    blk256 = _blk_ones(GROUP_W)
    tril = jnp.asarray(np.tril(np.ones((MLSTM_T, MLSTM_T), np.float32)), dtype=BF16)
    src = np.arange(LANES)[:, None]
    dst = np.arange(3 * GROUP_W)[None, :]
    expand = jnp.asarray(
        (src == SUBLANES * (1 + dst // GROUP_W) + (dst % GROUP_W) // HEAD_DIM).astype(np.float32),
        dtype=BF16)
    hgrn_tril, hgrn_pmask = _hgrn_constants()

    cos_t, sin_t = _rope_tables(positions)
    la, l1m, oml = [row3(a) for a in _hgrn_lower_bounds(hgrn_lb_logits)]
    sinks = swa_sinks.astype(F32)

    xf = x.reshape(n, D_MODEL)
    for l in range(depth):
        ml, gt, hb, lf, sw = _in_proj(
            xf, row3(in_norm_g), w_in_p, b_in_p, mlstm_conv_w.astype(F32), row3(mlstm_conv_b),
            la, l1m, oml, cos_t, sin_t, q_gain, k_gain, blk256, l, seq)
        mm, mh, ms = _mixers(ml, gt, hb, lf, sw, f_bias, row3(mlstm_norm_g), row3(hgrn_norm_g), sinks,
                             l, batch, (blk256, tril, expand, hgrn_tril, hgrn_pmask))
        xf = _tail(xf, mm, mh, ms, p.reshape(depth, n, PLE_DIM), w_out_b, row3(mlp_norm_g), w_up_b,
                   w_down_b, row3(ple_norm_g), w_gate_b, w_proj_b, row3(ple_post_norm_g), l)
    return xf.reshape(batch, seq, D_MODEL)
```

```python
import functools

import numpy as np
import jax
import jax.numpy as jnp
from jax import lax
from jax.experimental import pallas as pl
from jax.experimental.pallas import tpu as pltpu

F32 = jnp.float32
BF16 = jnp.bfloat16

D_MODEL = 1024
N_HEADS = 4
HEAD_DIM = 64
GROUP_W = N_HEADS * HEAD_DIM
CONV_K = 4
SWA_Q_HEADS = 8
SWA_KV_HEADS = 2
SWA_GROUP = SWA_Q_HEADS // SWA_KV_HEADS
WINDOW = 128
ROPE_THETA = 500000.0
ROT_DIM = HEAD_DIM // 4
ROT_HALF = ROT_DIM // 2
D_FF = 4 * D_MODEL
PLE_DIM = 256
EPS = 1e-6
LOG2_E = 1.4426950408889634

LANES = 128
SUBLANES = 8
VMEM_LIMIT_BYTES = 56 * 1024 * 1024

GATE_COLS = 2 * N_HEADS
M_COLS = 4 * GROUP_W + LANES
H_COLS = 4 * GROUP_W
S_COLS = SWA_Q_HEADS * HEAD_DIM + 2 * SWA_KV_HEADS * HEAD_DIM
IN_COLS = M_COLS + H_COLS + S_COLS
GATE_OFF = 4 * GROUP_W

IN_TM = 512
TAIL_TM = 512
ROPE_TM = 2048
MIX_ROWS = 512
MLSTM_T = 256
HGRN_L = 64
SWA_T = MIX_ROWS

N_SPLIT = 2
assert HGRN_L == HEAD_DIM


def _params(n_grid_dims):
    return pltpu.CompilerParams(
        dimension_semantics=("arbitrary",) * n_grid_dims,
        vmem_limit_bytes=VMEM_LIMIT_BYTES)


def _dot(a, b):
    return jnp.dot(a, b, preferred_element_type=F32)


def _dot_nt(a, b):
    return lax.dot_general(a, b, (((1,), (1,)), ((), ())), preferred_element_type=F32)


def _dot_tn(a, b):
    return lax.dot_general(a, b, (((0,), (0,)), ((), ())), preferred_element_type=F32)


def _split(x, n=N_SPLIT):
    pieces = []
    r = x
    for _ in range(n):
        p = r.astype(BF16)
        pieces.append(p)
        r = r - p.astype(F32)
    return pieces


def _dot_const_left(m, x):
    acc = None
    for p in _split(x):
        t = _dot(m, p)
        acc = t if acc is None else acc + t
    return acc


def _dot_const_right(x, m):
    acc = None
    for p in _split(x):
        t = _dot(p, m)
        acc = t if acc is None else acc + t
    return acc


def _sigmoid(x):
    return 1.0 / (1.0 + jnp.exp(-x))


def _log_sigmoid(x):
    return jnp.minimum(x, 0.0) - jnp.log1p(jnp.exp(-jnp.abs(x)))


def _rms(x, g):
    return x * lax.rsqrt(jnp.mean(x * x, axis=-1, keepdims=True) + EPS) * g


def _head_rms(x, g, blk_ones):
    ms = _dot(jnp.square(x).astype(BF16), blk_ones) * (1.0 / HEAD_DIM)
    return x * lax.rsqrt(ms + EPS) * g


def _lane_head_id(shape):
    return lax.broadcasted_iota(jnp.int32, shape, len(shape) - 1) // HEAD_DIM


def _blk_ones(width):
    hid = np.arange(width) // HEAD_DIM
    return jnp.asarray((hid[:, None] == hid[None, :]).astype(np.float32), dtype=BF16)


def _hgrn_level_sizes():
    sizes = []
    c = 2
    while c <= HGRN_L:
        sizes.append(c)
        c *= 2
    return sizes


def _hgrn_constants():
    L = HGRN_L
    t = np.arange(L)[:, None]
    u = np.arange(L)[None, :]
    masks = [(t == u)]
    for c in _hgrn_level_sizes():
        mid = (t // c) * c + c // 2 - 1
        same_blk = (t // c) == (u // c)
        u_mid = (u // c) * c + c // 2 - 1
        masks.append(same_blk & (t > mid) & (u <= u_mid))
    tril = (u <= t).astype(np.float32)
    pmask = np.stack([np.tile(m.astype(np.float32), (1, N_HEADS)) for m in masks])
    return jnp.asarray(tril, dtype=BF16), jnp.asarray(pmask, dtype=F32)


def _rope_kernel(pos_ref, freq_ref, cos_ref, sin_ref):
    ang = pos_ref[...].astype(F32) * freq_ref[0:1, :]
    cos_ref[...] = jnp.cos(ang)
    sin_ref[...] = jnp.sin(ang)


def _rope_tables(positions):
    n = positions.size
    per_row = LANES // ROT_HALF
    inv_freq = ROPE_THETA ** (-jnp.arange(0, ROT_DIM, 2, dtype=F32) / ROT_DIM)
    freq = jnp.zeros((SUBLANES, LANES), F32).at[0].set(jnp.tile(inv_freq, per_row))
    pos = jnp.repeat(positions.reshape(n // per_row, per_row), ROT_HALF, axis=1)
    rows = n // per_row
    tm = min(ROPE_TM, rows)
    out = jax.ShapeDtypeStruct((rows, LANES), F32)
    cos8, sin8 = pl.pallas_call(
        _rope_kernel,
        grid=(rows // tm,),
        in_specs=[pl.BlockSpec((tm, LANES), lambda i: (i, 0)),
                  pl.BlockSpec((SUBLANES, LANES), lambda i: (0, 0))],
        out_specs=[pl.BlockSpec((tm, LANES), lambda i: (i, 0))] * 2,
        out_shape=[out, out],
        compiler_params=_params(1),
        name="rope_tables",
    )(pos, freq)
    cos8 = cos8.reshape(n, ROT_HALF)
    sin8 = sin8.reshape(n, ROT_HALF)
    rest = HEAD_DIM - ROT_DIM
    cos_h = jnp.concatenate([cos8, cos8, jnp.ones((n, rest), F32)], axis=1)
    sin_h = jnp.concatenate([-sin8, sin8, jnp.zeros((n, rest), F32)], axis=1)
    reps = LANES // HEAD_DIM
    return jnp.tile(cos_h, (1, reps)), jnp.tile(sin_h, (1, reps))


def _lb_kernel(lg_ref, la_ref, l1m_ref, oml_ref):
    z = lg_ref[...]
    e = jnp.exp(z - jnp.max(z, axis=0, keepdims=True))
    sm = e / jnp.sum(e, axis=0, keepdims=True)
    depth = z.shape[0]
    run = sm[0:1, :]
    first = run
    for l in range(depth):
        if l > 0:
            run = run + sm[l:l + 1, :]
        lb = run - first
        la_ref[l:l + 1, :] = jnp.log(lb)
        l1m_ref[l:l + 1, :] = jnp.log1p(-lb)
        oml_ref[l:l + 1, :] = 1.0 - lb


def _hgrn_lower_bounds(logits):
    out = jax.ShapeDtypeStruct(logits.shape, F32)
    return pl.pallas_call(
        _lb_kernel, out_shape=[out, out, out], name="hgrn_lower_bounds",
    )(logits.astype(F32))


def _in_proj_kernel(x_ref, g_ref, w_ref, b_ref, cw_ref, cb_ref, la_ref, l1m_ref, oml_ref,
                    cos_ref, sin_ref, qg_ref, kg_ref, ones_ref,
                    ml_ref, gt_ref, hb_ref, lf_ref, sw_ref, xbuf, *, tiles_per_seq):
    TM = x_ref.shape[0]
    W = GROUP_W

    @pl.when(pl.program_id(0) % tiles_per_seq == 0)
    def _():
        xbuf[0:SUBLANES, :] = jnp.zeros((SUBLANES, 2 * W), F32)

    h = _rms(x_ref[...], g_ref[...]).astype(BF16)

    def proj(lo, width):
        return _dot(h, w_ref[:, lo:lo + width]) + b_ref[:, lo:lo + width]

    cos_t = cos_ref[...]
    sin_t = sin_ref[...]
    blk = ones_ref[...]
    lane = lax.broadcasted_iota(jnp.int32, (TM, LANES), 1)
    low_half = lane < HEAD_DIM
    rot_first = (lane % HEAD_DIM) < ROT_HALF
    s0 = M_COLS + H_COLS
    QW = SWA_Q_HEADS * HEAD_DIM

    def head_sumsq(y):
        width = y.shape[-1]
        return _dot(jnp.square(y).astype(BF16), blk[0:width, 0:width])

    def norm_rope(v, sumsq, gain):
        xn = v * lax.rsqrt(sumsq * (1.0 / HEAD_DIM) + EPS) * gain
        partner = jnp.where(rot_first, pltpu.roll(xn, LANES - ROT_HALF, 1), pltpu.roll(xn, ROT_HALF, 1))
        return xn * cos_t + partner * sin_t

    def swa_q(c, y, sumsq):
        for half in range(2):
            sl = slice(half * LANES, (half + 1) * LANES)
            sw_ref[:, (c + half) * LANES:(c + half + 1) * LANES] = (
                norm_rope(y[:, sl], sumsq[:, sl], qg_ref[...]) * (HEAD_DIM ** -0.5 * LOG2_E)).astype(BF16)

    def swa_kv(y, sumsq):
        k = norm_rope(y[:, 0:LANES], sumsq, kg_ref[...])
        v = y[:, LANES:2 * LANES]
        k_sw = pltpu.roll(k, HEAD_DIM, 1)
        v_sw = pltpu.roll(v, HEAD_DIM, 1)
        sw_ref[:, QW:QW + LANES] = jnp.where(low_half, k, k_sw).astype(BF16)
        sw_ref[:, QW + LANES:QW + 2 * LANES] = jnp.where(low_half, k_sw, k).astype(BF16)
        sw_ref[:, QW + 2 * LANES:QW + 3 * LANES] = jnp.where(low_half, v, v_sw).astype(BF16)
        sw_ref[:, QW + 3 * LANES:QW + 4 * LANES] = jnp.where(low_half, v_sw, v).astype(BF16)

    def mlstm_qk(y):
        xbuf[SUBLANES:SUBLANES + TM, :] = y
        acc = cb_ref[...] + cw_ref[CONV_K - 1:CONV_K, :] * xbuf[SUBLANES:SUBLANES + TM, :]
        for j in range(1, CONV_K):
            acc = acc + cw_ref[CONV_K - 1 - j:CONV_K - j, :] * xbuf[SUBLANES - j:SUBLANES - j + TM, :]
        xbuf[0:SUBLANES, :] = xbuf[TM:TM + SUBLANES, :]
        qk = acc * _sigmoid(acc)
        ml_ref[:, 0:W] = qk[:, 0:W].astype(BF16)
        ml_ref[:, W:2 * W] = (qk[:, W:2 * W] * (HEAD_DIM ** -0.5)).astype(BF16)

    def hgrn_qf(y):
        hq = y[:, 0:W]
        hf = y[:, W:2 * W]
        la = la_ref[...]
        u = jnp.exp(-jnp.abs(hf))
        t = l1m_ref[...] + (jnp.minimum(hf, 0.0) - jnp.log1p(u))
        lf_ref[...] = jnp.maximum(la, t) + jnp.log1p(jnp.exp(-jnp.abs(la - t)))
        hb_ref[:, 0:W] = (hq * _sigmoid(hq)).astype(BF16)
        hb_ref[:, W:2 * W] = (oml_ref[...] * jnp.where(hf >= 0.0, u, 1.0) / (1.0 + u)).astype(BF16)

    y_h = proj(M_COLS, 2 * W)
    y_m = proj(0, 2 * W)
    hgrn_qf(y_h)
    y_kv = proj(s0 + QW, 2 * LANES)
    mlstm_qk(y_m)
    y_q0 = proj(s0, 2 * LANES)
    swa_kv(y_kv, head_sumsq(y_kv[:, 0:LANES]))
    y_q1 = proj(s0 + 2 * LANES, 2 * LANES)
    swa_q(0, y_q0, head_sumsq(y_q0))
    y_h2 = proj(M_COLS + 2 * W, 2 * W)
    swa_q(2, y_q1, head_sumsq(y_q1))
    y_m2 = proj(2 * W, 2 * W)
    hg = y_h2[:, W:2 * W]
    hb_ref[:, 2 * W:3 * W] = y_h2[:, 0:W].astype(BF16)
    hb_ref[:, 3 * W:4 * W] = (hg * _sigmoid(hg)).astype(BF16)
    gt_ref[...] = proj(GATE_OFF, LANES)
    ml_ref[:, 2 * W:3 * W] = y_m2[:, 0:W].astype(BF16)
    ml_ref[:, 3 * W:4 * W] = _sigmoid(y_m2[:, W:2 * W]).astype(BF16)


def _in_proj(x, g, w, b, conv_w, conv_b, la, l1m, oml, cos_t, sin_t, q_gain, k_gain, blk_ones,
             layer, seq):
    n = x.shape[0]
    tm = min(IN_TM, n)
    row = lambda i: (i, 0)
    lay = lambda i: (layer, 0, 0)
    W4 = 4 * GROUP_W
    return pl.pallas_call(
        functools.partial(_in_proj_kernel, tiles_per_seq=seq // tm),
        grid=(n // tm,),
        in_specs=[pl.BlockSpec((tm, D_MODEL), row),
                  pl.BlockSpec((None, 1, D_MODEL), lay),
                  pl.BlockSpec((None, D_MODEL, IN_COLS), lay),
                  pl.BlockSpec((None, 1, IN_COLS), lay),
                  pl.BlockSpec((None, CONV_K, 2 * GROUP_W), lay),
                  pl.BlockSpec((None, 1, 2 * GROUP_W), lay),
                  pl.BlockSpec((None, 1, GROUP_W), lay),
                  pl.BlockSpec((None, 1, GROUP_W), lay),
                  pl.BlockSpec((None, 1, GROUP_W), lay),
                  pl.BlockSpec((tm, LANES), row),
                  pl.BlockSpec((tm, LANES), row),
                  pl.BlockSpec((None, 1, LANES), lay),
                  pl.BlockSpec((None, 1, LANES), lay),
                  pl.BlockSpec((GROUP_W, GROUP_W), lambda i: (0, 0))],
        out_specs=[pl.BlockSpec((tm, W4), row),
                   pl.BlockSpec((tm, LANES), row),
                   pl.BlockSpec((tm, W4), row),
                   pl.BlockSpec((tm, GROUP_W), row),
                   pl.BlockSpec((tm, W4), row)],
        out_shape=[jax.ShapeDtypeStruct((n, W4), BF16),
                   jax.ShapeDtypeStruct((n, LANES), F32),
                   jax.ShapeDtypeStruct((n, W4), BF16),
                   jax.ShapeDtypeStruct((n, GROUP_W), F32),
                   jax.ShapeDtypeStruct((n, W4), BF16)],
        scratch_shapes=[pltpu.VMEM((tm + SUBLANES, 2 * GROUP_W), F32)],
        compiler_params=_params(1),
        name="in_proj",
    )(x, g, w, b, conv_w, conv_b, la, l1m, oml, cos_t, sin_t, q_gain, k_gain, blk_ones)


def _mlstm_chunk(r0, y_ref, gt_ref, fb_ref, g_ref, ones_ref, tril_ref, exp_ref, o_ref,
                 c_st, n_st, m_st):
    T = MLSTM_T
    W = GROUP_W
    qb = y_ref[r0:r0 + T, 0:W]
    kb = y_ref[r0:r0 + T, W:2 * W]
    vb = y_ref[r0:r0 + T, 2 * W:3 * W]
    out_gate = y_ref[r0:r0 + T, 3 * W:4 * W]

    gates = gt_ref[r0:r0 + T, :]
    i_pre = gates
    lf = pltpu.roll(_log_sigmoid(gates + fb_ref[...]), LANES - N_HEADS, 1)
    lf_hi = lf.astype(BF16)
    lf_lo = (lf - lf_hi.astype(F32)).astype(BF16)
    bb = _dot(tril_ref[...], jnp.concatenate([lf_hi, lf_lo], axis=1))
    b = bb[:, 0:LANES] + bb[:, LANES:2 * LANES]
    a = i_pre - b
    row = lax.broadcasted_iota(jnp.int32, (T, LANES), 0)
    cm = a
    sh = 1
    while sh < T:
        cm = jnp.maximum(cm, jnp.where(row >= sh, pltpu.roll(cm, sh, 0), -jnp.inf))
        sh *= 2
    m_prev = m_st[0:1, :]
    m_j = b + jnp.maximum(m_prev, cm)
    m_new = m_j[T - 1:T, :]
    b_last = b[T - 1:T, :]
    w_inter = jnp.exp(b + m_prev - m_j)
    e_den = jnp.exp(-m_j)
    w_s = jnp.exp(b_last + a - m_new)
    decay = jnp.exp(b_last + m_prev - m_new)
    cexp = (b - m_j) * LOG2_E
    a_t = (a * LOG2_E).T

    stacked = jnp.concatenate([w_inter, e_den, w_s], axis=0)
    head_lane = lax.broadcasted_iota(jnp.int32, stacked.shape, 1) < N_HEADS
    ex = _dot(jnp.where(head_lane, stacked, 0.0).astype(BF16), exp_ref[...])
    w_inter_x = ex[0:T]
    e_den_x = ex[T:2 * T]
    w_s_x = ex[2 * T:3 * T]
    decay8 = jnp.where(head_lane[0:SUBLANES], jnp.broadcast_to(decay, (SUBLANES, LANES)), 0.0)
    decay_x = _dot_const_right(decay8, exp_ref[...])[0:1]

    blk = ones_ref[...]
    hid = _lane_head_id((T, W))
    causal = (lax.broadcasted_iota(jnp.int32, (T, T), 1)
              <= lax.broadcasted_iota(jnp.int32, (T, T), 0))
    head_rows = [blk[h * HEAD_DIM:h * HEAD_DIM + 1, :] for h in range(N_HEADS)]
    s_all = _dot_nt(jnp.concatenate([qb * hr for hr in head_rows], axis=0), kb)
    num = None
    den = None
    for h in range(N_HEADS):
        dm = cexp[:, h:h + 1] + a_t[h:h + 1, :]
        s = s_all[h * T:(h + 1) * T] * jnp.exp2(jnp.where(causal, dm, -jnp.inf))
        dn = _dot(s.astype(BF16), vb * head_rows[h])
        dd = jnp.where(hid == h, jnp.sum(s, axis=-1, keepdims=True), 0.0)
        num = dn if num is None else num + dn
        den = dd if den is None else den + dd
    n_row = n_st[0:1, :].astype(BF16)
    num = num + w_inter_x * _dot(qb, c_st[...].astype(BF16))
    den = den + w_inter_x * _dot(qb * n_row, blk)
    hh = num / jnp.maximum(jnp.abs(den), e_den_x)

    o_ref[r0:r0 + T, :] = _head_rms(hh, g_ref[...], blk) * out_gate.astype(F32)

    kw = kb * w_s_x.astype(BF16)
    c_st[...] = decay_x * c_st[...] + blk.astype(F32) * _dot_tn(kw, vb)
    n_st[...] = decay_x * n_st[...] + _dot(jnp.ones((SUBLANES, T), BF16), kw)
    m_st[...] = jnp.broadcast_to(m_new, m_st.shape)


def _hgrn_chunk(r0, y_ref, lf_ref, g_ref, ones_ref, mst_ref, pm_ref, o_ref, s_st):
    L = HGRN_L
    W = GROUP_W
    gain = g_ref[...]
    blk = ones_ref[...]
    blk_f = blk.astype(F32)
    tril = mst_ref[...]
    row = lax.broadcasted_iota(jnp.int32, (L, W), 0)

    def level_gap(g, lf2, c):
        if c == 2:
            return jnp.where(row % 2 == 1, lf2, 0.0)
        if c >= SUBLANES:
            mids = [b * c + c // 2 - 1 for b in range(L // c)]
            return g - jnp.concatenate(
                [jnp.broadcast_to(g[m:m + 1, :], (c, W)) for m in mids], axis=0)
        lo = jnp.concatenate([jnp.broadcast_to(g[m:m + 1, :], (SUBLANES, W))
                              for m in range(1, L, SUBLANES)], axis=0)
        hi = jnp.concatenate([jnp.broadcast_to(g[m:m + 1, :], (SUBLANES, W))
                              for m in range(5, L, SUBLANES)], axis=0)
        return g - jnp.where(row % SUBLANES < 4, lo, hi)

    def head_stack(xb):
        return jnp.concatenate([xb] * N_HEADS, axis=0) * blk

    qf = y_ref[r0:r0 + L, 0:W]
    key = y_ref[r0:r0 + L, W:2 * W]
    vb = y_ref[r0:r0 + L, 2 * W:3 * W]
    out_gate = y_ref[r0:r0 + L, 3 * W:4 * W]

    lf2 = lf_ref[r0:r0 + L, :] * LOG2_E
    g = _dot_const_left(tril, lf2)
    e_last = jnp.exp2(g[L - 1:L, :])
    q_hat = qf * jnp.exp2(g).astype(BF16)
    k_hat = key * jnp.exp2(g[L - 1:L, :] - g).astype(BF16)

    a = pm_ref[0] * _dot_nt(qf, head_stack(key))
    for i, c in enumerate(_hgrn_level_sizes()):
        e_i = jnp.exp2(-jnp.abs(level_gap(g, lf2, c))).astype(BF16)
        a = a + pm_ref[1 + i] * _dot_nt(qf * e_i, head_stack(key * e_i))
    o = _dot(a.astype(BF16), head_stack(vb)) + _dot_nt(q_hat, s_st[...].astype(BF16))

    o_ref[r0:r0 + L, :] = _head_rms(o, gain, blk) * out_gate.astype(F32)
    s_st[...] = s_st[...] * e_last + blk_f * _dot_tn(vb, k_hat)


def _swa_blocks(sink_ref, y_ref, o_ref, k_prev, v_prev, layer):
    T = SWA_T
    Wn = WINDOW
    QW = SWA_Q_HEADS * HEAD_DIM
    step = pl.program_id(1)

    k2 = [jnp.concatenate([k_prev[g], y_ref[:, QW + g * LANES:QW + (g + 1) * LANES]], axis=0)
          for g in range(SWA_KV_HEADS)]
    v2 = [jnp.concatenate([v_prev[g], y_ref[:, QW + (2 + g) * LANES:QW + (3 + g) * LANES]], axis=0)
          for g in range(SWA_KV_HEADS)]
    for g in range(SWA_KV_HEADS):
        k_prev[g] = k2[g][T:T + Wn]
        v_prev[g] = v2[g][T:T + Wn]

    qi = lax.broadcasted_iota(jnp.int32, (Wn, 2 * Wn), 0)
    ki = lax.broadcasted_iota(jnp.int32, (Wn, 2 * Wn), 1)
    band = (ki > qi) & (ki <= qi + Wn)
    band_first = band & (ki >= jnp.where(step > 0, 0, Wn))
    lane_w = lax.broadcasted_iota(jnp.int32, (Wn, LANES), 1) < HEAD_DIM
    lane_row = lax.broadcasted_iota(jnp.int32, (1, LANES), 1) < HEAD_DIM
    low_row = jnp.where(lane_row, 1.0, 0.0).astype(BF16)
    high_row = jnp.where(lane_row, 0.0, 1.0).astype(BF16)

    def scores(j, g):
        r0 = j * Wn
        kc = k2[g][r0:r0 + 2 * Wn]
        parts = []
        for c in (2 * g, 2 * g + 1):
            qc = y_ref[r0:r0 + Wn, c * LANES:(c + 1) * LANES]
            parts.append(qc * low_row)
            parts.append(qc * high_row)
        return _dot_nt(jnp.concatenate(parts, axis=0), kc)

    def finish(j, g, s):
        r0 = j * Wn
        mask = band_first if j == 0 else band
        vc = v2[g][r0:r0 + 2 * Wn]
        outs = []
        for r in range(SWA_GROUP):
            sink = sink_ref[layer, SWA_GROUP * g + r] * LOG2_E
            sr = jnp.where(mask, s[r * Wn:(r + 1) * Wn], -jnp.inf)
            m = jnp.maximum(jnp.max(sr, axis=-1, keepdims=True), sink)
            pexp = jnp.exp2(sr - m)
            den = jnp.sum(pexp, axis=-1, keepdims=True) + jnp.exp2(sink - m)
            outs.append(_dot(pexp.astype(BF16), vc) / den)
        c0 = 2 * g * LANES
        o_ref[r0:r0 + Wn, c0:c0 + LANES] = jnp.where(lane_w, outs[0], outs[1])
        o_ref[r0:r0 + Wn, c0 + LANES:c0 + 2 * LANES] = jnp.where(lane_w, outs[2], outs[3])

    return [(functools.partial(scores, j, g), functools.partial(finish, j, g))
            for j in range(T // Wn) for g in range(SWA_KV_HEADS)]


def _mixers_kernel(sink_ref, ml_ref, gt_ref, fb_ref, gm_ref, hb_ref, lf_ref, gh_ref, sw_ref,
                   ones_ref, tril_ref, exp_ref, tril64_ref, pm_ref,
                   om_ref, oh_ref, os_ref,
                   c_st, n_st, m_st, s_st, k_prev, v_prev, *, layer):
    @pl.when(pl.program_id(1) == 0)
    def _():
        for ref in (c_st, n_st, m_st, s_st, k_prev, v_prev):
            ref[...] = jnp.zeros_like(ref)

    swa_blocks = _swa_blocks(sink_ref, sw_ref, os_ref, k_prev, v_prev, layer)
    n_sub = MIX_ROWS // HGRN_L
    assert len(swa_blocks) == n_sub and n_sub % (MIX_ROWS // MLSTM_T) == 0
    per_mlstm = n_sub // (MIX_ROWS // MLSTM_T)
    for sub in range(n_sub):
        swa_scores, swa_finish = swa_blocks[sub]
        s = swa_scores()
        _hgrn_chunk(sub * HGRN_L, hb_ref, lf_ref, gh_ref, ones_ref, tril64_ref, pm_ref, oh_ref, s_st)
        swa_finish(s)
        if sub % per_mlstm == 0:
            _mlstm_chunk((sub // per_mlstm) * MLSTM_T, ml_ref, gt_ref, fb_ref, gm_ref, ones_ref,
                         tril_ref, exp_ref, om_ref, c_st, n_st, m_st)


def _mixers(ml, gt, hb, lf, sw, f_bias, mlstm_g, hgrn_g, sinks, layer, batch, consts):
    n = ml.shape[0]
    seq = n // batch
    T = MIX_ROWS
    steps = seq // T
    W4 = 4 * GROUP_W
    QW = SWA_Q_HEADS * HEAD_DIM
    row = lambda bi, ci: (bi * steps + ci, 0)
    lay = lambda bi, ci: (layer, 0, 0)
    cst = lambda bi, ci: (0, 0)
    blk_ones, tril, expand, tril64, pmask = consts
    return pl.pallas_call(
        functools.partial(_mixers_kernel, layer=layer),
        grid=(batch, steps),
        in_specs=[pl.BlockSpec(memory_space=pltpu.SMEM),
                  pl.BlockSpec((T, W4), row),
                  pl.BlockSpec((T, LANES), row),
                  pl.BlockSpec((None, 1, LANES), lay),
                  pl.BlockSpec((None, 1, GROUP_W), lay),
                  pl.BlockSpec((T, W4), row),
                  pl.BlockSpec((T, GROUP_W), row),
                  pl.BlockSpec((None, 1, GROUP_W), lay),
                  pl.BlockSpec((T, W4), row),
                  pl.BlockSpec((GROUP_W, GROUP_W), cst),
                  pl.BlockSpec((MLSTM_T, MLSTM_T), cst),
                  pl.BlockSpec((LANES, GROUP_W), cst),
                  pl.BlockSpec((HGRN_L, HGRN_L), cst),
                  pl.BlockSpec(pmask.shape, lambda bi, ci: (0, 0, 0))],
        out_specs=[pl.BlockSpec((T, GROUP_W), row),
                   pl.BlockSpec((T, GROUP_W), row),
                   pl.BlockSpec((T, QW), row)],
        out_shape=[jax.ShapeDtypeStruct((n, GROUP_W), F32),
                   jax.ShapeDtypeStruct((n, GROUP_W), F32),
                   jax.ShapeDtypeStruct((n, QW), F32)],
        scratch_shapes=[pltpu.VMEM((GROUP_W, GROUP_W), F32),
                        pltpu.VMEM((SUBLANES, GROUP_W), F32),
                        pltpu.VMEM((SUBLANES, LANES), F32),
                        pltpu.VMEM((GROUP_W, GROUP_W), F32),
                        pltpu.VMEM((SWA_KV_HEADS, WINDOW, LANES), BF16),
                        pltpu.VMEM((SWA_KV_HEADS, WINDOW, LANES), BF16)],
        compiler_params=_params(2),
        name="mixers",
    )(sinks, ml, gt, f_bias, mlstm_g, hb, lf, hgrn_g, sw, blk_ones, tril, expand, tril64, pmask)


def _tail_kernel(x_ref, mm_ref, mh_ref, ms_ref, p_ref, wo_ref, g2_ref, wu_ref, wd_ref, g3_ref,
                 wg_ref, wp_ref, g4_ref, o_ref):
    W = GROUP_W
    mix = _dot(mm_ref[...].astype(BF16), wo_ref[0:W, :])
    mix = mix + _dot(mh_ref[...].astype(BF16), wo_ref[W:2 * W, :])
    mix = mix + _dot(ms_ref[...].astype(BF16), wo_ref[2 * W:4 * W, :])
    x = x_ref[...] + mix

    h2 = _rms(x, g2_ref[...]).astype(BF16)
    mlp = None
    for c in range(D_FF // D_MODEL):
        u = _dot(h2, wu_ref[:, c * D_MODEL:(c + 1) * D_MODEL])
        act = jnp.square(jnp.maximum(u, 0.0)).astype(BF16)
        d = _dot(act, wd_ref[c * D_MODEL:(c + 1) * D_MODEL, :])
        mlp = d if mlp is None else mlp + d
    x = x + mlp

    gate = _sigmoid(_dot(_rms(x, g3_ref[...]).astype(BF16), wg_ref[...]))
    emb = _rms(_dot(p_ref[...].astype(BF16), wp_ref[...]), g4_ref[...])
    o_ref[...] = x + gate * emb


def _tail(x, mm, mh, ms, p, w_out, g2, w_up, w_down, g3, w_gate, w_proj, g4, layer):
    n = x.shape[0]
    tm = min(TAIL_TM, n)
    tiles = n // tm
    row = lambda i: (i, 0)
    lay = lambda i: (layer, 0, 0)
    once = pl.Buffered(1)

    def wspec(shape):
        return pl.BlockSpec((None,) + shape, lay, pipeline_mode=once)

    return pl.pallas_call(
        _tail_kernel,
        grid=(tiles,),
        in_specs=[pl.BlockSpec((tm, D_MODEL), row),
                  pl.BlockSpec((tm, GROUP_W), row),
                  pl.BlockSpec((tm, GROUP_W), row),
                  pl.BlockSpec((tm, 2 * GROUP_W), row),
                  pl.BlockSpec((None, tm, PLE_DIM), lambda i: (layer, i, 0)),
                  wspec((D_MODEL, D_MODEL)),
                  wspec((1, D_MODEL)),
                  wspec((D_MODEL, D_FF)),
                  wspec((D_FF, D_MODEL)),
                  wspec((1, D_MODEL)),
                  wspec((D_MODEL, D_MODEL)),
                  wspec((PLE_DIM, D_MODEL)),
                  wspec((1, D_MODEL))],
        out_specs=pl.BlockSpec((tm, D_MODEL), row),
        out_shape=jax.ShapeDtypeStruct((n, D_MODEL), F32),
        compiler_params=_params(1),
        name="tail",
    )(x, mm, mh, ms, p, w_out, g2, w_up, w_down, g3, w_gate, w_proj, g4)


def kernel(x, p, positions, in_norm_g, w_in, b_in, mlstm_f_bias, mlstm_conv_w, mlstm_conv_b,
           mlstm_norm_g, hgrn_lb_logits, hgrn_norm_g, swa_q_norm_g, swa_k_norm_g, swa_sinks,
           w_out, mlp_norm_g, w_up, w_down, ple_norm_g, w_ple_gate, w_ple_proj, ple_post_norm_g):
    batch, seq, d_model = x.shape
    depth = w_in.shape[0]
    n = batch * seq
    assert d_model == D_MODEL and seq % max(MIX_ROWS, IN_TM) == 0

    n_raw = GATE_OFF + GATE_COLS
    pad = LANES - GATE_COLS
    w_in_p = jnp.concatenate(
        [w_in[..., :n_raw], jnp.zeros((depth, D_MODEL, pad), w_in.dtype), w_in[..., n_raw:]],
        axis=-1).astype(BF16)
    b_in_p = jnp.concatenate(
        [b_in[..., :n_raw], jnp.zeros((depth, pad), b_in.dtype), b_in[..., n_raw:]],
        axis=-1).astype(F32).reshape(depth, 1, IN_COLS)
    f_bias = jnp.zeros((depth, 1, LANES), F32).at[:, 0, N_HEADS:GATE_COLS].set(mlstm_f_bias.astype(F32))
    row3 = lambda a: a.astype(F32).reshape(depth, 1, a.shape[-1])
    q_gain = row3(jnp.tile(swa_q_norm_g, (1, LANES // HEAD_DIM)))
    k_gain = row3(jnp.tile(swa_k_norm_g, (1, LANES // HEAD_DIM)))
    w_out_b, w_up_b, w_down_b = w_out.astype(BF16), w_up.astype(BF16), w_down.astype(BF16)
    w_gate_b, w_proj_b = w_ple_gate.astype(BF16), w_ple_proj.astype(BF16)

    blk256 = _blk_ones(GROUP_W)
    tril = jnp.asarray(np.tril(np.ones((MLSTM_T, MLSTM_T), np.float32)), dtype=BF16)
    expand = jnp.asarray(
        (np.arange(LANES)[:, None] == (np.arange(GROUP_W) // HEAD_DIM)[None, :]).astype(np.float32),
        dtype=BF16)
    hgrn_tril, hgrn_pmask = _hgrn_constants()

    cos_t, sin_t = _rope_tables(positions)
    la, l1m, oml = [row3(a) for a in _hgrn_lower_bounds(hgrn_lb_logits)]
    sinks = swa_sinks.astype(F32)

    xf = x.reshape(n, D_MODEL)
    for l in range(depth):
        ml, gt, hb, lf, sw = _in_proj(
            xf, row3(in_norm_g), w_in_p, b_in_p, mlstm_conv_w.astype(F32), row3(mlstm_conv_b),
            la, l1m, oml, cos_t, sin_t, q_gain, k_gain, blk256, l, seq)
        mm, mh, ms = _mixers(ml, gt, hb, lf, sw, f_bias, row3(mlstm_norm_g), row3(hgrn_norm_g), sinks,
                             l, batch, (blk256, tril, expand, hgrn_tril, hgrn_pmask))
        xf = _tail(xf, mm, mh, ms, p.reshape(depth, n, PLE_DIM), w_out_b, row3(mlp_norm_g), w_up_b,
                   w_down_b, row3(ple_norm_g), w_gate_b, w_proj_b, row3(ple_post_norm_g), l)
    return xf.reshape(batch, seq, D_MODEL)
```

```python
import functools

import numpy as np
import jax
import jax.numpy as jnp
from jax import lax
from jax.experimental import pallas as pl
from jax.experimental.pallas import tpu as pltpu

F32 = jnp.float32
BF16 = jnp.bfloat16

D_MODEL = 1024
N_HEADS = 4
HEAD_DIM = 64
GROUP_W = N_HEADS * HEAD_DIM
CONV_K = 4
SWA_Q_HEADS = 8
SWA_KV_HEADS = 2
SWA_GROUP = SWA_Q_HEADS // SWA_KV_HEADS
WINDOW = 128
ROPE_THETA = 500000.0
ROT_DIM = HEAD_DIM // 4
ROT_HALF = ROT_DIM // 2
D_FF = 4 * D_MODEL
PLE_DIM = 256
EPS = 1e-6
LOG2_E = 1.4426950408889634

LANES = 128
SUBLANES = 8
VMEM_LIMIT_BYTES = 56 * 1024 * 1024

GATE_COLS = 2 * N_HEADS
M_COLS = 4 * GROUP_W + LANES
H_COLS = 4 * GROUP_W
S_COLS = SWA_Q_HEADS * HEAD_DIM + 2 * SWA_KV_HEADS * HEAD_DIM
IN_COLS = M_COLS + H_COLS + S_COLS
GATE_OFF = 4 * GROUP_W

IN_TM = 512
TAIL_TM = 512
ROPE_TM = 2048
MIX_ROWS = 512
MLSTM_T = 256
HGRN_L = 64
SWA_T = MIX_ROWS

N_SPLIT = 2
assert HGRN_L == HEAD_DIM


def _params(n_grid_dims):
    return pltpu.CompilerParams(
        dimension_semantics=("arbitrary",) * n_grid_dims,
        vmem_limit_bytes=VMEM_LIMIT_BYTES)


def _dot(a, b):
    return jnp.dot(a, b, preferred_element_type=F32)


def _dot_nt(a, b):
    return lax.dot_general(a, b, (((1,), (1,)), ((), ())), preferred_element_type=F32)


def _dot_tn(a, b):
    return lax.dot_general(a, b, (((0,), (0,)), ((), ())), preferred_element_type=F32)


def _split(x, n=N_SPLIT):
    pieces = []
    r = x
    for _ in range(n):
        p = r.astype(BF16)
        pieces.append(p)
        r = r - p.astype(F32)
    return pieces


def _dot_const_left(m, x):
    acc = None
    for p in _split(x):
        t = _dot(m, p)
        acc = t if acc is None else acc + t
    return acc


def _dot_const_right(x, m):
    acc = None
    for p in _split(x):
        t = _dot(p, m)
        acc = t if acc is None else acc + t
    return acc


def _sigmoid(x):
    return 1.0 / (1.0 + jnp.exp(-x))


def _log_sigmoid(x):
    return jnp.minimum(x, 0.0) - jnp.log1p(jnp.exp(-jnp.abs(x)))


def _rms(x, g):
    return x * lax.rsqrt(jnp.mean(x * x, axis=-1, keepdims=True) + EPS) * g


def _head_rms(x, g, blk_ones):
    ms = _dot(jnp.square(x).astype(BF16), blk_ones) * (1.0 / HEAD_DIM)
    return x * lax.rsqrt(ms + EPS) * g


def _lane_head_id(shape):
    return lax.broadcasted_iota(jnp.int32, shape, len(shape) - 1) // HEAD_DIM


def _blk_ones(width):
    hid = np.arange(width) // HEAD_DIM
    return jnp.asarray((hid[:, None] == hid[None, :]).astype(np.float32), dtype=BF16)


def _hgrn_level_sizes():
    sizes = []
    c = 2
    while c <= HGRN_L:
        sizes.append(c)
        c *= 2
    return sizes


def _hgrn_constants():
    L = HGRN_L
    t = np.arange(L)[:, None]
    u = np.arange(L)[None, :]
    masks = [(t == u)]
    for c in _hgrn_level_sizes():
        mid = (t // c) * c + c // 2 - 1
        same_blk = (t // c) == (u // c)
        u_mid = (u // c) * c + c // 2 - 1
        masks.append(same_blk & (t > mid) & (u <= u_mid))
    tril = (u <= t).astype(np.float32)
    pmask = np.stack([np.tile(m.astype(np.float32), (1, N_HEADS)) for m in masks])
    return jnp.asarray(tril, dtype=BF16), jnp.asarray(pmask, dtype=F32)


ROPE_PER_ROW = LANES // ROT_HALF


def _rope_kernel(pos_ref, freq_ref, sel_ref, one_ref, cos_ref, sin_ref):
    ang = pos_ref[...].astype(F32) * freq_ref[0:1, :]
    cos_ref[...] = _dot_const_right(jnp.cos(ang), sel_ref[0]) + one_ref[0:1, :]
    sin_ref[...] = _dot_const_right(jnp.sin(ang), sel_ref[1])


def _rope_tables(positions):
    n = positions.size
    per_row = ROPE_PER_ROW
    rows = n // per_row
    wide = per_row * LANES
    inv_freq = ROPE_THETA ** (-jnp.arange(0, ROT_DIM, 2, dtype=F32) / ROT_DIM)
    freq = jnp.zeros((SUBLANES, LANES), F32).at[0].set(jnp.tile(inv_freq, per_row))
    pos = jnp.repeat(positions.reshape(rows, per_row), ROT_HALF, axis=1)
    col = np.arange(wide)
    tok, lane = col // LANES, (col % LANES) % HEAD_DIM
    rotated = lane < ROT_DIM
    pick = (np.arange(LANES)[:, None] == (tok * ROT_HALF + lane % ROT_HALF)[None, :]) & rotated[None, :]
    sign = np.where(lane < ROT_HALF, -1.0, 1.0)[None, :]
    sel = jnp.asarray(np.stack([pick.astype(np.float32), pick * sign]), dtype=BF16)
    unrotated = jnp.asarray(np.broadcast_to((~rotated).astype(np.float32)[None, :], (SUBLANES, wide)))
    tm = min(ROPE_TM // per_row, rows)
    out = jax.ShapeDtypeStruct((rows, wide), F32)
    cos_w, sin_w = pl.pallas_call(
        _rope_kernel,
        grid=(rows // tm,),
        in_specs=[pl.BlockSpec((tm, LANES), lambda i: (i, 0)),
                  pl.BlockSpec((SUBLANES, LANES), lambda i: (0, 0)),
                  pl.BlockSpec(sel.shape, lambda i: (0, 0, 0)),
                  pl.BlockSpec((SUBLANES, wide), lambda i: (0, 0))],
        out_specs=[pl.BlockSpec((tm, wide), lambda i: (i, 0))] * 2,
        out_shape=[out, out],
        compiler_params=_params(1),
        name="rope_tables",
    )(pos, freq, sel, unrotated)
    return cos_w.reshape(n, LANES), sin_w.reshape(n, LANES)


def _lb_kernel(lg_ref, la_ref, l1m_ref, oml_ref):
    z = lg_ref[...]
    e = jnp.exp(z - jnp.max(z, axis=0, keepdims=True))
    sm = e / jnp.sum(e, axis=0, keepdims=True)
    depth = z.shape[0]
    run = sm[0:1, :]
    first = run
    for l in range(depth):
        if l > 0:
            run = run + sm[l:l + 1, :]
        lb = run - first
        la_ref[l:l + 1, :] = jnp.log(lb)
        l1m_ref[l:l + 1, :] = jnp.log1p(-lb)
        oml_ref[l:l + 1, :] = 1.0 - lb


def _hgrn_lower_bounds(logits):
    out = jax.ShapeDtypeStruct(logits.shape, F32)
    return pl.pallas_call(
        _lb_kernel, out_shape=[out, out, out], name="hgrn_lower_bounds",
    )(logits.astype(F32))


def _in_proj_kernel(x_ref, g_ref, w_ref, b_ref, cw_ref, cb_ref, la_ref, l1m_ref, oml_ref,
                    cos_ref, sin_ref, qg_ref, kg_ref, ones_ref,
                    ml_ref, gt_ref, hb_ref, lf_ref, sw_ref, xbuf, *, tiles_per_seq):
    TM = x_ref.shape[0]
    W = GROUP_W

    @pl.when(pl.program_id(0) % tiles_per_seq == 0)
    def _():
        xbuf[0:SUBLANES, :] = jnp.zeros((SUBLANES, 2 * W), F32)

    h = _rms(x_ref[...], g_ref[...]).astype(BF16)

    def proj(lo, width):
        return _dot(h, w_ref[:, lo:lo + width]) + b_ref[:, lo:lo + width]

    cos_t = cos_ref[...]
    sin_t = sin_ref[...]
    blk = ones_ref[...]
    lane = lax.broadcasted_iota(jnp.int32, (TM, LANES), 1)
    low_half = lane < HEAD_DIM
    rot_first = (lane % HEAD_DIM) < ROT_HALF
    s0 = M_COLS + H_COLS
    QW = SWA_Q_HEADS * HEAD_DIM

    def head_sumsq(y):
        width = y.shape[-1]
        return _dot(jnp.square(y).astype(BF16), blk[0:width, 0:width])

    def norm_rope(v, sumsq, gain):
        xn = v * lax.rsqrt(sumsq * (1.0 / HEAD_DIM) + EPS) * gain
        partner = jnp.where(rot_first, pltpu.roll(xn, LANES - ROT_HALF, 1), pltpu.roll(xn, ROT_HALF, 1))
        return xn * cos_t + partner * sin_t

    def swa_q(c, y, sumsq):
        for half in range(2):
            sl = slice(half * LANES, (half + 1) * LANES)
            sw_ref[:, (c + half) * LANES:(c + half + 1) * LANES] = (
                norm_rope(y[:, sl], sumsq[:, sl], qg_ref[...]) * (HEAD_DIM ** -0.5 * LOG2_E)).astype(BF16)

    def swa_kv(y, sumsq):
        k = norm_rope(y[:, 0:LANES], sumsq, kg_ref[...])
        v = y[:, LANES:2 * LANES]
        k_sw = pltpu.roll(k, HEAD_DIM, 1)
        v_sw = pltpu.roll(v, HEAD_DIM, 1)
        sw_ref[:, QW:QW + LANES] = jnp.where(low_half, k, k_sw).astype(BF16)
        sw_ref[:, QW + LANES:QW + 2 * LANES] = jnp.where(low_half, k_sw, k).astype(BF16)
        sw_ref[:, QW + 2 * LANES:QW + 3 * LANES] = jnp.where(low_half, v, v_sw).astype(BF16)
        sw_ref[:, QW + 3 * LANES:QW + 4 * LANES] = jnp.where(low_half, v_sw, v).astype(BF16)

    def mlstm_qk(y):
        xbuf[SUBLANES:SUBLANES + TM, :] = y
        acc = cb_ref[...] + cw_ref[CONV_K - 1:CONV_K, :] * xbuf[SUBLANES:SUBLANES + TM, :]
        for j in range(1, CONV_K):
            acc = acc + cw_ref[CONV_K - 1 - j:CONV_K - j, :] * xbuf[SUBLANES - j:SUBLANES - j + TM, :]
        xbuf[0:SUBLANES, :] = xbuf[TM:TM + SUBLANES, :]
        qk = acc * _sigmoid(acc)
        ml_ref[:, 0:W] = qk[:, 0:W].astype(BF16)
        ml_ref[:, W:2 * W] = (qk[:, W:2 * W] * (HEAD_DIM ** -0.5)).astype(BF16)

    def hgrn_qf(y):
        hq = y[:, 0:W]
        hf = y[:, W:2 * W]
        la = la_ref[...]
        u = jnp.exp(-jnp.abs(hf))
        t = l1m_ref[...] + (jnp.minimum(hf, 0.0) - jnp.log1p(u))
        lf_ref[...] = jnp.maximum(la, t) + jnp.log1p(jnp.exp(-jnp.abs(la - t)))
        hb_ref[:, 0:W] = (hq * _sigmoid(hq)).astype(BF16)
        hb_ref[:, W:2 * W] = (oml_ref[...] * jnp.where(hf >= 0.0, u, 1.0) / (1.0 + u)).astype(BF16)

    y_h = proj(M_COLS, 2 * W)
    y_m = proj(0, 2 * W)
    hgrn_qf(y_h)
    y_kv = proj(s0 + QW, 2 * LANES)
    mlstm_qk(y_m)
    y_q0 = proj(s0, 2 * LANES)
    swa_kv(y_kv, head_sumsq(y_kv[:, 0:LANES]))
    y_q1 = proj(s0 + 2 * LANES, 2 * LANES)
    swa_q(0, y_q0, head_sumsq(y_q0))
    y_h2 = proj(M_COLS + 2 * W, 2 * W)
    swa_q(2, y_q1, head_sumsq(y_q1))
    y_m2 = proj(2 * W, 2 * W)
    hg = y_h2[:, W:2 * W]
    hb_ref[:, 2 * W:3 * W] = y_h2[:, 0:W].astype(BF16)
    hb_ref[:, 3 * W:4 * W] = (hg * _sigmoid(hg)).astype(BF16)
    gt_ref[...] = proj(GATE_OFF, LANES)
    ml_ref[:, 2 * W:3 * W] = y_m2[:, 0:W].astype(BF16)
    ml_ref[:, 3 * W:4 * W] = _sigmoid(y_m2[:, W:2 * W]).astype(BF16)


def _in_proj(x, g, w, b, conv_w, conv_b, la, l1m, oml, cos_t, sin_t, q_gain, k_gain, blk_ones,
             layer, seq):
    n = x.shape[0]
    tm = min(IN_TM, n)
    row = lambda i: (i, 0)
    lay = lambda i: (layer, 0, 0)
    W4 = 4 * GROUP_W
    return pl.pallas_call(
        functools.partial(_in_proj_kernel, tiles_per_seq=seq // tm),
        grid=(n // tm,),
        in_specs=[pl.BlockSpec((tm, D_MODEL), row),
                  pl.BlockSpec((None, 1, D_MODEL), lay),
                  pl.BlockSpec((None, D_MODEL, IN_COLS), lay),
                  pl.BlockSpec((None, 1, IN_COLS), lay),
                  pl.BlockSpec((None, CONV_K, 2 * GROUP_W), lay),
                  pl.BlockSpec((None, 1, 2 * GROUP_W), lay),
                  pl.BlockSpec((None, 1, GROUP_W), lay),
                  pl.BlockSpec((None, 1, GROUP_W), lay),
                  pl.BlockSpec((None, 1, GROUP_W), lay),
                  pl.BlockSpec((tm, LANES), row),
                  pl.BlockSpec((tm, LANES), row),
                  pl.BlockSpec((None, 1, LANES), lay),
                  pl.BlockSpec((None, 1, LANES), lay),
                  pl.BlockSpec((GROUP_W, GROUP_W), lambda i: (0, 0))],
        out_specs=[pl.BlockSpec((tm, W4), row),
                   pl.BlockSpec((tm, LANES), row),
                   pl.BlockSpec((tm, W4), row),
                   pl.BlockSpec((tm, GROUP_W), row),
                   pl.BlockSpec((tm, W4), row)],
        out_shape=[jax.ShapeDtypeStruct((n, W4), BF16),
                   jax.ShapeDtypeStruct((n, LANES), F32),
                   jax.ShapeDtypeStruct((n, W4), BF16),
                   jax.ShapeDtypeStruct((n, GROUP_W), F32),
                   jax.ShapeDtypeStruct((n, W4), BF16)],
        scratch_shapes=[pltpu.VMEM((tm + SUBLANES, 2 * GROUP_W), F32)],
        compiler_params=_params(1),
        name="in_proj",
    )(x, g, w, b, conv_w, conv_b, la, l1m, oml, cos_t, sin_t, q_gain, k_gain, blk_ones)


def _mlstm_chunk(r0, y_ref, gt_ref, fb_ref, g_ref, ones_ref, tril_ref, exp_ref, o_ref,
                 c_st, n_st, m_st):
    T = MLSTM_T
    W = GROUP_W
    qb = y_ref[r0:r0 + T, 0:W]
    kb = y_ref[r0:r0 + T, W:2 * W]
    vb = y_ref[r0:r0 + T, 2 * W:3 * W]
    out_gate = y_ref[r0:r0 + T, 3 * W:4 * W]

    gates = gt_ref[r0:r0 + T, :]
    i_pre = gates
    lf = pltpu.roll(_log_sigmoid(gates + fb_ref[...]), LANES - N_HEADS, 1)
    lf_hi = lf.astype(BF16)
    lf_lo = (lf - lf_hi.astype(F32)).astype(BF16)
    bb = _dot(tril_ref[...], jnp.concatenate([lf_hi, lf_lo], axis=1))
    b = bb[:, 0:LANES] + bb[:, LANES:2 * LANES]
    a = i_pre - b
    row = lax.broadcasted_iota(jnp.int32, (T, LANES), 0)
    cm = a
    sh = 1
    while sh < T:
        cm = jnp.maximum(cm, jnp.where(row >= sh, pltpu.roll(cm, sh, 0), -jnp.inf))
        sh *= 2
    m_prev = m_st[0:1, :]
    m_j = b + jnp.maximum(m_prev, cm)
    m_new = m_j[T - 1:T, :]
    b_last = b[T - 1:T, :]
    w_inter = jnp.exp(b + m_prev - m_j)
    e_den = jnp.exp(-m_j)
    w_s = jnp.exp(b_last + a - m_new)
    decay = jnp.exp(b_last + m_prev - m_new)
    cexp = (b - m_j) * LOG2_E
    a_t = (a * LOG2_E).T

    stacked = jnp.concatenate([w_inter, e_den, w_s], axis=0)
    head_lane = lax.broadcasted_iota(jnp.int32, stacked.shape, 1) < N_HEADS
    ex = _dot(jnp.where(head_lane, stacked, 0.0).astype(BF16), exp_ref[...])
    w_inter_x = ex[0:T]
    e_den_x = ex[T:2 * T]
    w_s_x = ex[2 * T:3 * T]
    decay8 = jnp.where(head_lane[0:SUBLANES], jnp.broadcast_to(decay, (SUBLANES, LANES)), 0.0)
    decay_x = _dot_const_right(decay8, exp_ref[...])[0:1]

    blk = ones_ref[...]
    hid = _lane_head_id((T, W))
    causal = (lax.broadcasted_iota(jnp.int32, (T, T), 1)
              <= lax.broadcasted_iota(jnp.int32, (T, T), 0))
    head_rows = [blk[h * HEAD_DIM:h * HEAD_DIM + 1, :] for h in range(N_HEADS)]
    s_all = _dot_nt(jnp.concatenate([qb * hr for hr in head_rows], axis=0), kb)
    num = None
    den = None
    for h in range(N_HEADS):
        dm = cexp[:, h:h + 1] + a_t[h:h + 1, :]
        s = s_all[h * T:(h + 1) * T] * jnp.exp2(jnp.where(causal, dm, -jnp.inf))
        dn = _dot(s.astype(BF16), vb * head_rows[h])
        dd = jnp.where(hid == h, jnp.sum(s, axis=-1, keepdims=True), 0.0)
        num = dn if num is None else num + dn
        den = dd if den is None else den + dd
    n_row = n_st[0:1, :].astype(BF16)
    num = num + w_inter_x * _dot(qb, c_st[...].astype(BF16))
    den = den + w_inter_x * _dot(qb * n_row, blk)
    hh = num / jnp.maximum(jnp.abs(den), e_den_x)

    o_ref[r0:r0 + T, :] = _head_rms(hh, g_ref[...], blk) * out_gate.astype(F32)

    kw = kb * w_s_x.astype(BF16)
    c_st[...] = decay_x * c_st[...] + blk.astype(F32) * _dot_tn(kw, vb)
    n_st[...] = decay_x * n_st[...] + _dot(jnp.ones((SUBLANES, T), BF16), kw)
    m_st[...] = jnp.broadcast_to(m_new, m_st.shape)


def _hgrn_chunk(r0, y_ref, lf_ref, g_ref, ones_ref, mst_ref, pm_ref, o_ref, s_st):
    L = HGRN_L
    W = GROUP_W
    gain = g_ref[...]
    blk = ones_ref[...]
    blk_f = blk.astype(F32)
    tril = mst_ref[...]
    row = lax.broadcasted_iota(jnp.int32, (L, W), 0)

    def level_gap(g, lf2, c):
        if c == 2:
            return jnp.where(row % 2 == 1, lf2, 0.0)
        if c >= SUBLANES:
            mids = [b * c + c // 2 - 1 for b in range(L // c)]
            return g - jnp.concatenate(
                [jnp.broadcast_to(g[m:m + 1, :], (c, W)) for m in mids], axis=0)
        lo = jnp.concatenate([jnp.broadcast_to(g[m:m + 1, :], (SUBLANES, W))
                              for m in range(1, L, SUBLANES)], axis=0)
        hi = jnp.concatenate([jnp.broadcast_to(g[m:m + 1, :], (SUBLANES, W))
                              for m in range(5, L, SUBLANES)], axis=0)
        return g - jnp.where(row % SUBLANES < 4, lo, hi)

    def head_stack(xb):
        return jnp.concatenate([xb] * N_HEADS, axis=0) * blk

    qf = y_ref[r0:r0 + L, 0:W]
    key = y_ref[r0:r0 + L, W:2 * W]
    vb = y_ref[r0:r0 + L, 2 * W:3 * W]
    out_gate = y_ref[r0:r0 + L, 3 * W:4 * W]

    lf2 = lf_ref[r0:r0 + L, :] * LOG2_E
    g = _dot_const_left(tril, lf2)
    e_last = jnp.exp2(g[L - 1:L, :])
    q_hat = qf * jnp.exp2(g).astype(BF16)
    k_hat = key * jnp.exp2(g[L - 1:L, :] - g).astype(BF16)

    a = pm_ref[0] * _dot_nt(qf, head_stack(key))
    for i, c in enumerate(_hgrn_level_sizes()):
        e_i = jnp.exp2(-jnp.abs(level_gap(g, lf2, c))).astype(BF16)
        a = a + pm_ref[1 + i] * _dot_nt(qf * e_i, head_stack(key * e_i))
    o = _dot(a.astype(BF16), head_stack(vb)) + _dot_nt(q_hat, s_st[...].astype(BF16))

    o_ref[r0:r0 + L, :] = _head_rms(o, gain, blk) * out_gate.astype(F32)
    s_st[...] = s_st[...] * e_last + blk_f * _dot_tn(vb, k_hat)


def _swa_blocks(sink_ref, y_ref, o_ref, k_prev, v_prev, layer):
    T = SWA_T
    Wn = WINDOW
    QW = SWA_Q_HEADS * HEAD_DIM
    step = pl.program_id(1)

    k2 = [jnp.concatenate([k_prev[g], y_ref[:, QW + g * LANES:QW + (g + 1) * LANES]], axis=0)
          for g in range(SWA_KV_HEADS)]
    v2 = [jnp.concatenate([v_prev[g], y_ref[:, QW + (2 + g) * LANES:QW + (3 + g) * LANES]], axis=0)
          for g in range(SWA_KV_HEADS)]
    for g in range(SWA_KV_HEADS):
        k_prev[g] = k2[g][T:T + Wn]
        v_prev[g] = v2[g][T:T + Wn]

    qi = lax.broadcasted_iota(jnp.int32, (Wn, 2 * Wn), 0)
    ki = lax.broadcasted_iota(jnp.int32, (Wn, 2 * Wn), 1)
    band = (ki > qi) & (ki <= qi + Wn)
    band_first = band & (ki >= jnp.where(step > 0, 0, Wn))
    lane_w = lax.broadcasted_iota(jnp.int32, (Wn, LANES), 1) < HEAD_DIM
    lane_row = lax.broadcasted_iota(jnp.int32, (1, LANES), 1) < HEAD_DIM
    low_row = jnp.where(lane_row, 1.0, 0.0).astype(BF16)
    high_row = jnp.where(lane_row, 0.0, 1.0).astype(BF16)

    def scores(j, g):
        r0 = j * Wn
        kc = k2[g][r0:r0 + 2 * Wn]
        parts = []
        for c in (2 * g, 2 * g + 1):
            qc = y_ref[r0:r0 + Wn, c * LANES:(c + 1) * LANES]
            parts.append(qc * low_row)
            parts.append(qc * high_row)
        return _dot_nt(jnp.concatenate(parts, axis=0), kc)

    def finish(j, g, s):
        r0 = j * Wn
        mask = band_first if j == 0 else band
        vc = v2[g][r0:r0 + 2 * Wn]
        outs = []
        for r in range(SWA_GROUP):
            sink = sink_ref[layer, SWA_GROUP * g + r] * LOG2_E
            sr = jnp.where(mask, s[r * Wn:(r + 1) * Wn], -jnp.inf)
            m = jnp.maximum(jnp.max(sr, axis=-1, keepdims=True), sink)
            pexp = jnp.exp2(sr - m)
            den = jnp.sum(pexp, axis=-1, keepdims=True) + jnp.exp2(sink - m)
            outs.append(_dot(pexp.astype(BF16), vc) / den)
        c0 = 2 * g * LANES
        o_ref[r0:r0 + Wn, c0:c0 + LANES] = jnp.where(lane_w, outs[0], outs[1])
        o_ref[r0:r0 + Wn, c0 + LANES:c0 + 2 * LANES] = jnp.where(lane_w, outs[2], outs[3])

    return [(functools.partial(scores, j, g), functools.partial(finish, j, g))
            for j in range(T // Wn) for g in range(SWA_KV_HEADS)]


def _mixers_kernel(sink_ref, ml_ref, gt_ref, fb_ref, gm_ref, hb_ref, lf_ref, gh_ref, sw_ref,
                   ones_ref, tril_ref, exp_ref, tril64_ref, pm_ref,
                   om_ref, oh_ref, os_ref,
                   c_st, n_st, m_st, s_st, k_prev, v_prev, *, layer):
    @pl.when(pl.program_id(1) == 0)
    def _():
        for ref in (c_st, n_st, m_st, s_st, k_prev, v_prev):
            ref[...] = jnp.zeros_like(ref)

    swa_blocks = _swa_blocks(sink_ref, sw_ref, os_ref, k_prev, v_prev, layer)
    n_sub = MIX_ROWS // HGRN_L
    assert len(swa_blocks) == n_sub and n_sub % (MIX_ROWS // MLSTM_T) == 0
    per_mlstm = n_sub // (MIX_ROWS // MLSTM_T)
    for sub in range(n_sub):
        swa_scores, swa_finish = swa_blocks[sub]
        s = swa_scores()
        _hgrn_chunk(sub * HGRN_L, hb_ref, lf_ref, gh_ref, ones_ref, tril64_ref, pm_ref, oh_ref, s_st)
        swa_finish(s)
        if sub % per_mlstm == 0:
            _mlstm_chunk((sub // per_mlstm) * MLSTM_T, ml_ref, gt_ref, fb_ref, gm_ref, ones_ref,
                         tril_ref, exp_ref, om_ref, c_st, n_st, m_st)


def _mixers(ml, gt, hb, lf, sw, f_bias, mlstm_g, hgrn_g, sinks, layer, batch, consts):
    n = ml.shape[0]
    seq = n // batch
    T = MIX_ROWS
    steps = seq // T
    W4 = 4 * GROUP_W
    QW = SWA_Q_HEADS * HEAD_DIM
    row = lambda bi, ci: (bi * steps + ci, 0)
    lay = lambda bi, ci: (layer, 0, 0)
    cst = lambda bi, ci: (0, 0)
    blk_ones, tril, expand, tril64, pmask = consts
    return pl.pallas_call(
        functools.partial(_mixers_kernel, layer=layer),
        grid=(batch, steps),
        in_specs=[pl.BlockSpec(memory_space=pltpu.SMEM),
                  pl.BlockSpec((T, W4), row),
                  pl.BlockSpec((T, LANES), row),
                  pl.BlockSpec((None, 1, LANES), lay),
                  pl.BlockSpec((None, 1, GROUP_W), lay),
                  pl.BlockSpec((T, W4), row),
                  pl.BlockSpec((T, GROUP_W), row),
                  pl.BlockSpec((None, 1, GROUP_W), lay),
                  pl.BlockSpec((T, W4), row),
                  pl.BlockSpec((GROUP_W, GROUP_W), cst),
                  pl.BlockSpec((MLSTM_T, MLSTM_T), cst),
                  pl.BlockSpec((LANES, GROUP_W), cst),
                  pl.BlockSpec((HGRN_L, HGRN_L), cst),
                  pl.BlockSpec(pmask.shape, lambda bi, ci: (0, 0, 0))],
        out_specs=[pl.BlockSpec((T, GROUP_W), row),
                   pl.BlockSpec((T, GROUP_W), row),
                   pl.BlockSpec((T, QW), row)],
        out_shape=[jax.ShapeDtypeStruct((n, GROUP_W), F32),
                   jax.ShapeDtypeStruct((n, GROUP_W), F32),
                   jax.ShapeDtypeStruct((n, QW), F32)],
        scratch_shapes=[pltpu.VMEM((GROUP_W, GROUP_W), F32),
                        pltpu.VMEM((SUBLANES, GROUP_W), F32),
                        pltpu.VMEM((SUBLANES, LANES), F32),
                        pltpu.VMEM((GROUP_W, GROUP_W), F32),
                        pltpu.VMEM((SWA_KV_HEADS, WINDOW, LANES), BF16),
                        pltpu.VMEM((SWA_KV_HEADS, WINDOW, LANES), BF16)],
        compiler_params=_params(2),
        name="mixers",
    )(sinks, ml, gt, f_bias, mlstm_g, hb, lf, hgrn_g, sw, blk_ones, tril, expand, tril64, pmask)


def _tail_kernel(x_ref, mm_ref, mh_ref, ms_ref, p_ref, wo_ref, g2_ref, wu_ref, wd_ref, g3_ref,
                 wg_ref, wp_ref, g4_ref, o_ref):
    W = GROUP_W
    mix = _dot(mm_ref[...].astype(BF16), wo_ref[0:W, :])
    mix = mix + _dot(mh_ref[...].astype(BF16), wo_ref[W:2 * W, :])
    mix = mix + _dot(ms_ref[...].astype(BF16), wo_ref[2 * W:4 * W, :])
    x = x_ref[...] + mix

    h2 = _rms(x, g2_ref[...]).astype(BF16)
    mlp = None
    for c in range(D_FF // D_MODEL):
        u = _dot(h2, wu_ref[:, c * D_MODEL:(c + 1) * D_MODEL])
        act = jnp.square(jnp.maximum(u, 0.0)).astype(BF16)
        d = _dot(act, wd_ref[c * D_MODEL:(c + 1) * D_MODEL, :])
        mlp = d if mlp is None else mlp + d
    x = x + mlp

    gate = _sigmoid(_dot(_rms(x, g3_ref[...]).astype(BF16), wg_ref[...]))
    emb = _rms(_dot(p_ref[...].astype(BF16), wp_ref[...]), g4_ref[...])
    o_ref[...] = x + gate * emb


def _tail(x, mm, mh, ms, p, w_out, g2, w_up, w_down, g3, w_gate, w_proj, g4, layer):
    n = x.shape[0]
    tm = min(TAIL_TM, n)
    tiles = n // tm
    row = lambda i: (i, 0)
    lay = lambda i: (layer, 0, 0)
    once = pl.Buffered(1)

    def wspec(shape):
        return pl.BlockSpec((None,) + shape, lay, pipeline_mode=once)

    return pl.pallas_call(
        _tail_kernel,
        grid=(tiles,),
        in_specs=[pl.BlockSpec((tm, D_MODEL), row),
                  pl.BlockSpec((tm, GROUP_W), row),
                  pl.BlockSpec((tm, GROUP_W), row),
                  pl.BlockSpec((tm, 2 * GROUP_W), row),
                  pl.BlockSpec((None, tm, PLE_DIM), lambda i: (layer, i, 0)),
                  wspec((D_MODEL, D_MODEL)),
                  wspec((1, D_MODEL)),
                  wspec((D_MODEL, D_FF)),
                  wspec((D_FF, D_MODEL)),
                  wspec((1, D_MODEL)),
                  wspec((D_MODEL, D_MODEL)),
                  wspec((PLE_DIM, D_MODEL)),
                  wspec((1, D_MODEL))],
        out_specs=pl.BlockSpec((tm, D_MODEL), row),
        out_shape=jax.ShapeDtypeStruct((n, D_MODEL), F32),
        compiler_params=_params(1),
        name="tail",
    )(x, mm, mh, ms, p, w_out, g2, w_up, w_down, g3, w_gate, w_proj, g4)


def kernel(x, p, positions, in_norm_g, w_in, b_in, mlstm_f_bias, mlstm_conv_w, mlstm_conv_b,
           mlstm_norm_g, hgrn_lb_logits, hgrn_norm_g, swa_q_norm_g, swa_k_norm_g, swa_sinks,
           w_out, mlp_norm_g, w_up, w_down, ple_norm_g, w_ple_gate, w_ple_proj, ple_post_norm_g):
    batch, seq, d_model = x.shape
    depth = w_in.shape[0]
    n = batch * seq
    assert d_model == D_MODEL and seq % max(MIX_ROWS, IN_TM) == 0

    n_raw = GATE_OFF + GATE_COLS
    pad = LANES - GATE_COLS
    w_in_p = jnp.concatenate(
        [w_in[..., :n_raw], jnp.zeros((depth, D_MODEL, pad), w_in.dtype), w_in[..., n_raw:]],
        axis=-1).astype(BF16)
    b_in_p = jnp.concatenate(
        [b_in[..., :n_raw], jnp.zeros((depth, pad), b_in.dtype), b_in[..., n_raw:]],
        axis=-1).astype(F32).reshape(depth, 1, IN_COLS)
    f_bias = jnp.zeros((depth, 1, LANES), F32).at[:, 0, N_HEADS:GATE_COLS].set(mlstm_f_bias.astype(F32))
    row3 = lambda a: a.astype(F32).reshape(depth, 1, a.shape[-1])
    q_gain = row3(jnp.tile(swa_q_norm_g, (1, LANES // HEAD_DIM)))
    k_gain = row3(jnp.tile(swa_k_norm_g, (1, LANES // HEAD_DIM)))
    w_out_b, w_up_b, w_down_b = w_out.astype(BF16), w_up.astype(BF16), w_down.astype(BF16)
    w_gate_b, w_proj_b = w_ple_gate.astype(BF16), w_ple_proj.astype(BF16)

    blk256 = _blk_ones(GROUP_W)
    tril = jnp.asarray(np.tril(np.ones((MLSTM_T, MLSTM_T), np.float32)), dtype=BF16)
    expand = jnp.asarray(
        (np.arange(LANES)[:, None] == (np.arange(GROUP_W) // HEAD_DIM)[None, :]).astype(np.float32),
        dtype=BF16)
    hgrn_tril, hgrn_pmask = _hgrn_constants()

    cos_t, sin_t = _rope_tables(positions)
    la, l1m, oml = [row3(a) for a in _hgrn_lower_bounds(hgrn_lb_logits)]
    sinks = swa_sinks.astype(F32)

    xf = x.reshape(n, D_MODEL)
    for l in range(depth):
        ml, gt, hb, lf, sw = _in_proj(
            xf, row3(in_norm_g), w_in_p, b_in_p, mlstm_conv_w.astype(F32), row3(mlstm_conv_b),
            la, l1m, oml, cos_t, sin_t, q_gain, k_gain, blk256, l, seq)
        mm, mh, ms = _mixers(ml, gt, hb, lf, sw, f_bias, row3(mlstm_norm_g), row3(hgrn_norm_g), sinks,
                             l, batch, (blk256, tril, expand, hgrn_tril, hgrn_pmask))
        xf = _tail(xf, mm, mh, ms, p.reshape(depth, n, PLE_DIM), w_out_b, row3(mlp_norm_g), w_up_b,
                   w_down_b, row3(ple_norm_g), w_gate_b, w_proj_b, row3(ple_post_norm_g), l)
    return xf.reshape(batch, seq, D_MODEL)
```

```python
import functools

import numpy as np
import jax
import jax.numpy as jnp
from jax import lax
from jax.experimental import pallas as pl
from jax.experimental.pallas import tpu as pltpu

F32 = jnp.float32
BF16 = jnp.bfloat16

D_MODEL = 1024
N_HEADS = 4
HEAD_DIM = 64
GROUP_W = N_HEADS * HEAD_DIM
CONV_K = 4
SWA_Q_HEADS = 8
SWA_KV_HEADS = 2
SWA_GROUP = SWA_Q_HEADS // SWA_KV_HEADS
WINDOW = 128
ROPE_THETA = 500000.0
ROT_DIM = HEAD_DIM // 4
ROT_HALF = ROT_DIM // 2
D_FF = 4 * D_MODEL
PLE_DIM = 256
EPS = 1e-6
LOG2_E = 1.4426950408889634

LANES = 128
SUBLANES = 8
VMEM_LIMIT_BYTES = 56 * 1024 * 1024

GATE_COLS = 2 * N_HEADS
M_COLS = 4 * GROUP_W + LANES
H_COLS = 4 * GROUP_W
S_COLS = SWA_Q_HEADS * HEAD_DIM + 2 * SWA_KV_HEADS * HEAD_DIM
IN_COLS = M_COLS + H_COLS + S_COLS
GATE_OFF = 4 * GROUP_W

IN_TM = 512
TAIL_TM = 512
FF_CHUNK = 1024
GATE_CHUNK = 256
ROPE_TM = 2048
MIX_ROWS = 512
MLSTM_T = 256
HGRN_L = 64
SWA_T = MIX_ROWS

N_SPLIT = 2
assert HGRN_L == HEAD_DIM


def _params(n_grid_dims):
    return pltpu.CompilerParams(
        dimension_semantics=("arbitrary",) * n_grid_dims,
        vmem_limit_bytes=VMEM_LIMIT_BYTES)


def _dot(a, b):
    return jnp.dot(a, b, preferred_element_type=F32)


def _dot_nt(a, b):
    return lax.dot_general(a, b, (((1,), (1,)), ((), ())), preferred_element_type=F32)


def _dot_tn(a, b):
    return lax.dot_general(a, b, (((0,), (0,)), ((), ())), preferred_element_type=F32)


def _split(x, n=N_SPLIT):
    pieces = []
    r = x
    for _ in range(n):
        p = r.astype(BF16)
        pieces.append(p)
        r = r - p.astype(F32)
    return pieces


def _dot_const_left(m, x):
    acc = None
    for p in _split(x):
        t = _dot(m, p)
        acc = t if acc is None else acc + t
    return acc


def _dot_const_right(x, m):
    acc = None
    for p in _split(x):
        t = _dot(p, m)
        acc = t if acc is None else acc + t
    return acc


def _sigmoid(x):
    return 1.0 / (1.0 + jnp.exp(-x))


def _log_sigmoid(x):
    return jnp.minimum(x, 0.0) - jnp.log1p(jnp.exp(-jnp.abs(x)))


def _rms(x, g):
    return x * lax.rsqrt(jnp.mean(x * x, axis=-1, keepdims=True) + EPS) * g


def _head_rms(x, g, blk_ones):
    ms = _dot(jnp.square(x).astype(BF16), blk_ones) * (1.0 / HEAD_DIM)
    return x * lax.rsqrt(ms + EPS) * g


def _lane_head_id(shape):
    return lax.broadcasted_iota(jnp.int32, shape, len(shape) - 1) // HEAD_DIM


def _blk_ones(width):
    hid = np.arange(width) // HEAD_DIM
    return jnp.asarray((hid[:, None] == hid[None, :]).astype(np.float32), dtype=BF16)


def _hgrn_level_sizes():
    sizes = []
    c = 2
    while c <= HGRN_L:
        sizes.append(c)
        c *= 2
    return sizes


def _hgrn_constants():
    L = HGRN_L
    t = np.arange(L)[:, None]
    u = np.arange(L)[None, :]
    masks = [(t == u)]
    for c in _hgrn_level_sizes():
        mid = (t // c) * c + c // 2 - 1
        same_blk = (t // c) == (u // c)
        u_mid = (u // c) * c + c // 2 - 1
        masks.append(same_blk & (t > mid) & (u <= u_mid))
    tril = (u <= t).astype(np.float32)
    pmask = np.stack([np.tile(m.astype(np.float32), (1, N_HEADS)) for m in masks])
    return jnp.asarray(tril, dtype=BF16), jnp.asarray(pmask, dtype=F32)


ROPE_PER_ROW = LANES // ROT_HALF


def _rope_kernel(pos_ref, freq_ref, sel_ref, one_ref, cos_ref, sin_ref):
    ang = pos_ref[...].astype(F32) * freq_ref[0:1, :]
    cos_ref[...] = _dot_const_right(jnp.cos(ang), sel_ref[0]) + one_ref[0:1, :]
    sin_ref[...] = _dot_const_right(jnp.sin(ang), sel_ref[1])


def _rope_tables(positions):
    n = positions.size
    per_row = ROPE_PER_ROW
    rows = n // per_row
    wide = per_row * LANES
    inv_freq = ROPE_THETA ** (-jnp.arange(0, ROT_DIM, 2, dtype=F32) / ROT_DIM)
    freq = jnp.zeros((SUBLANES, LANES), F32).at[0].set(jnp.tile(inv_freq, per_row))
    pos = jnp.repeat(positions.reshape(rows, per_row), ROT_HALF, axis=1)
    col = np.arange(wide)
    tok, lane = col // LANES, (col % LANES) % HEAD_DIM
    rotated = lane < ROT_DIM
    pick = (np.arange(LANES)[:, None] == (tok * ROT_HALF + lane % ROT_HALF)[None, :]) & rotated[None, :]
    sign = np.where(lane < ROT_HALF, -1.0, 1.0)[None, :]
    sel = jnp.asarray(np.stack([pick.astype(np.float32), pick * sign]), dtype=BF16)
    unrotated = jnp.asarray(np.broadcast_to((~rotated).astype(np.float32)[None, :], (SUBLANES, wide)))
    tm = min(ROPE_TM // per_row, rows)
    out = jax.ShapeDtypeStruct((rows, wide), F32)
    cos_w, sin_w = pl.pallas_call(
        _rope_kernel,
        grid=(rows // tm,),
        in_specs=[pl.BlockSpec((tm, LANES), lambda i: (i, 0)),
                  pl.BlockSpec((SUBLANES, LANES), lambda i: (0, 0)),
                  pl.BlockSpec(sel.shape, lambda i: (0, 0, 0)),
                  pl.BlockSpec((SUBLANES, wide), lambda i: (0, 0))],
        out_specs=[pl.BlockSpec((tm, wide), lambda i: (i, 0))] * 2,
        out_shape=[out, out],
        compiler_params=_params(1),
        name="rope_tables",
    )(pos, freq, sel, unrotated)
    return cos_w.reshape(n, LANES), sin_w.reshape(n, LANES)


def _lb_kernel(lg_ref, la_ref, l1m_ref, oml_ref):
    z = lg_ref[...]
    e = jnp.exp(z - jnp.max(z, axis=0, keepdims=True))
    sm = e / jnp.sum(e, axis=0, keepdims=True)
    depth = z.shape[0]
    run = sm[0:1, :]
    first = run
    for l in range(depth):
        if l > 0:
            run = run + sm[l:l + 1, :]
        lb = run - first
        la_ref[l:l + 1, :] = jnp.log(lb)
        l1m_ref[l:l + 1, :] = jnp.log1p(-lb)
        oml_ref[l:l + 1, :] = 1.0 - lb


def _hgrn_lower_bounds(logits):
    out = jax.ShapeDtypeStruct(logits.shape, F32)
    return pl.pallas_call(
        _lb_kernel, out_shape=[out, out, out], name="hgrn_lower_bounds",
    )(logits.astype(F32))


def _in_proj_kernel(x_ref, g_ref, w_ref, b_ref, cw_ref, cb_ref, la_ref, l1m_ref, oml_ref,
                    cos_ref, sin_ref, qg_ref, kg_ref, ones_ref,
                    ml_ref, gt_ref, hb_ref, lf_ref, sw_ref, xbuf, *, tiles_per_seq):
    TM = x_ref.shape[0]
    W = GROUP_W

    @pl.when(pl.program_id(0) % tiles_per_seq == 0)
    def _():
        xbuf[0:SUBLANES, :] = jnp.zeros((SUBLANES, 2 * W), F32)

    h = _rms(x_ref[...], g_ref[...]).astype(BF16)

    def proj(lo, width):
        return _dot(h, w_ref[:, lo:lo + width]) + b_ref[:, lo:lo + width]

    cos_t = cos_ref[...]
    sin_t = sin_ref[...]
    blk = ones_ref[...]
    lane = lax.broadcasted_iota(jnp.int32, (TM, LANES), 1)
    low_half = lane < HEAD_DIM
    rot_first = (lane % HEAD_DIM) < ROT_HALF
    s0 = M_COLS + H_COLS
    QW = SWA_Q_HEADS * HEAD_DIM

    def head_sumsq(y):
        width = y.shape[-1]
        return _dot(jnp.square(y).astype(BF16), blk[0:width, 0:width])

    def norm_rope(v, sumsq, gain):
        xn = v * lax.rsqrt(sumsq * (1.0 / HEAD_DIM) + EPS) * gain
        partner = jnp.where(rot_first, pltpu.roll(xn, LANES - ROT_HALF, 1), pltpu.roll(xn, ROT_HALF, 1))
        return xn * cos_t + partner * sin_t

    def swa_q(c, y, sumsq):
        for half in range(2):
            sl = slice(half * LANES, (half + 1) * LANES)
            sw_ref[:, (c + half) * LANES:(c + half + 1) * LANES] = (
                norm_rope(y[:, sl], sumsq[:, sl], qg_ref[...]) * (HEAD_DIM ** -0.5 * LOG2_E)).astype(BF16)

    def swa_kv(y, sumsq):
        k = norm_rope(y[:, 0:LANES], sumsq, kg_ref[...])
        v = y[:, LANES:2 * LANES]
        k_sw = pltpu.roll(k, HEAD_DIM, 1)
        v_sw = pltpu.roll(v, HEAD_DIM, 1)
        sw_ref[:, QW:QW + LANES] = jnp.where(low_half, k, k_sw).astype(BF16)
        sw_ref[:, QW + LANES:QW + 2 * LANES] = jnp.where(low_half, k_sw, k).astype(BF16)
        sw_ref[:, QW + 2 * LANES:QW + 3 * LANES] = jnp.where(low_half, v, v_sw).astype(BF16)
        sw_ref[:, QW + 3 * LANES:QW + 4 * LANES] = jnp.where(low_half, v_sw, v).astype(BF16)

    def mlstm_qk(y):
        xbuf[SUBLANES:SUBLANES + TM, :] = y
        acc = cb_ref[...] + cw_ref[CONV_K - 1:CONV_K, :] * xbuf[SUBLANES:SUBLANES + TM, :]
        for j in range(1, CONV_K):
            acc = acc + cw_ref[CONV_K - 1 - j:CONV_K - j, :] * xbuf[SUBLANES - j:SUBLANES - j + TM, :]
        xbuf[0:SUBLANES, :] = xbuf[TM:TM + SUBLANES, :]
        qk = acc * _sigmoid(acc)
        ml_ref[:, 0:W] = qk[:, 0:W].astype(BF16)
        ml_ref[:, W:2 * W] = (qk[:, W:2 * W] * (HEAD_DIM ** -0.5)).astype(BF16)

    def hgrn_qf(y):
        hq = y[:, 0:W]
        hf = y[:, W:2 * W]
        la = la_ref[...]
        u = jnp.exp(-jnp.abs(hf))
        t = l1m_ref[...] + (jnp.minimum(hf, 0.0) - jnp.log1p(u))
        lf_ref[...] = jnp.maximum(la, t) + jnp.log1p(jnp.exp(-jnp.abs(la - t)))
        hb_ref[:, 0:W] = (hq * _sigmoid(hq)).astype(BF16)
        hb_ref[:, W:2 * W] = (oml_ref[...] * jnp.where(hf >= 0.0, u, 1.0) / (1.0 + u)).astype(BF16)

    y_h = proj(M_COLS, 2 * W)
    y_m = proj(0, 2 * W)
    hgrn_qf(y_h)
    y_kv = proj(s0 + QW, 2 * LANES)
    mlstm_qk(y_m)
    y_q0 = proj(s0, 2 * LANES)
    swa_kv(y_kv, head_sumsq(y_kv[:, 0:LANES]))
    y_q1 = proj(s0 + 2 * LANES, 2 * LANES)
    swa_q(0, y_q0, head_sumsq(y_q0))
    y_h2 = proj(M_COLS + 2 * W, 2 * W)
    swa_q(2, y_q1, head_sumsq(y_q1))
    y_m2 = proj(2 * W, 2 * W)
    hg = y_h2[:, W:2 * W]
    hb_ref[:, 2 * W:3 * W] = y_h2[:, 0:W].astype(BF16)
    hb_ref[:, 3 * W:4 * W] = (hg * _sigmoid(hg)).astype(BF16)
    gt_ref[...] = proj(GATE_OFF, LANES)
    ml_ref[:, 2 * W:3 * W] = y_m2[:, 0:W].astype(BF16)
    ml_ref[:, 3 * W:4 * W] = _sigmoid(y_m2[:, W:2 * W]).astype(BF16)


def _in_proj(x, g, w, b, conv_w, conv_b, la, l1m, oml, cos_t, sin_t, q_gain, k_gain, blk_ones,
             layer, seq):
    n = x.shape[0]
    tm = min(IN_TM, n)
    row = lambda i: (i, 0)
    lay = lambda i: (layer, 0, 0)
    W4 = 4 * GROUP_W
    return pl.pallas_call(
        functools.partial(_in_proj_kernel, tiles_per_seq=seq // tm),
        grid=(n // tm,),
        in_specs=[pl.BlockSpec((tm, D_MODEL), row),
                  pl.BlockSpec((None, 1, D_MODEL), lay),
                  pl.BlockSpec((None, D_MODEL, IN_COLS), lay),
                  pl.BlockSpec((None, 1, IN_COLS), lay),
                  pl.BlockSpec((None, CONV_K, 2 * GROUP_W), lay),
                  pl.BlockSpec((None, 1, 2 * GROUP_W), lay),
                  pl.BlockSpec((None, 1, GROUP_W), lay),
                  pl.BlockSpec((None, 1, GROUP_W), lay),
                  pl.BlockSpec((None, 1, GROUP_W), lay),
                  pl.BlockSpec((tm, LANES), row),
                  pl.BlockSpec((tm, LANES), row),
                  pl.BlockSpec((None, 1, LANES), lay),
                  pl.BlockSpec((None, 1, LANES), lay),
                  pl.BlockSpec((GROUP_W, GROUP_W), lambda i: (0, 0))],
        out_specs=[pl.BlockSpec((tm, W4), row),
                   pl.BlockSpec((tm, LANES), row),
                   pl.BlockSpec((tm, W4), row),
                   pl.BlockSpec((tm, GROUP_W), row),
                   pl.BlockSpec((tm, W4), row)],
        out_shape=[jax.ShapeDtypeStruct((n, W4), BF16),
                   jax.ShapeDtypeStruct((n, LANES), F32),
                   jax.ShapeDtypeStruct((n, W4), BF16),
                   jax.ShapeDtypeStruct((n, GROUP_W), F32),
                   jax.ShapeDtypeStruct((n, W4), BF16)],
        scratch_shapes=[pltpu.VMEM((tm + SUBLANES, 2 * GROUP_W), F32)],
        compiler_params=_params(1),
        name="in_proj",
    )(x, g, w, b, conv_w, conv_b, la, l1m, oml, cos_t, sin_t, q_gain, k_gain, blk_ones)


def _mlstm_chunk(r0, y_ref, gt_ref, fb_ref, g_ref, ones_ref, tril_ref, exp_ref, o_ref,
                 c_st, n_st, m_st):
    T = MLSTM_T
    W = GROUP_W
    qb = y_ref[r0:r0 + T, 0:W]
    kb = y_ref[r0:r0 + T, W:2 * W]
    vb = y_ref[r0:r0 + T, 2 * W:3 * W]
    out_gate = y_ref[r0:r0 + T, 3 * W:4 * W]

    gates = gt_ref[r0:r0 + T, :]
    i_pre = gates
    lf = pltpu.roll(_log_sigmoid(gates + fb_ref[...]), LANES - N_HEADS, 1)
    lf_hi = lf.astype(BF16)
    lf_lo = (lf - lf_hi.astype(F32)).astype(BF16)
    bb = _dot(tril_ref[...], jnp.concatenate([lf_hi, lf_lo], axis=1))
    b = bb[:, 0:LANES] + bb[:, LANES:2 * LANES]
    a = i_pre - b
    row = lax.broadcasted_iota(jnp.int32, (T, LANES), 0)
    cm = a
    sh = 1
    while sh < T:
        cm = jnp.maximum(cm, jnp.where(row >= sh, pltpu.roll(cm, sh, 0), -jnp.inf))
        sh *= 2
    m_prev = m_st[0:1, :]
    m_j = b + jnp.maximum(m_prev, cm)
    m_new = m_j[T - 1:T, :]
    b_last = b[T - 1:T, :]
    w_inter = jnp.exp(b + m_prev - m_j)
    e_den = jnp.exp(-m_j)
    w_s = jnp.exp(b_last + a - m_new)
    decay = jnp.exp(b_last + m_prev - m_new)
    cexp = (b - m_j) * LOG2_E
    a_t = (a * LOG2_E).T

    stacked = jnp.concatenate([w_inter, e_den, w_s], axis=0)
    head_lane = lax.broadcasted_iota(jnp.int32, stacked.shape, 1) < N_HEADS
    ex = _dot(jnp.where(head_lane, stacked, 0.0).astype(BF16), exp_ref[...])
    w_inter_x = ex[0:T]
    e_den_x = ex[T:2 * T]
    w_s_x = ex[2 * T:3 * T]
    decay8 = jnp.where(head_lane[0:SUBLANES], jnp.broadcast_to(decay, (SUBLANES, LANES)), 0.0)
    decay_x = _dot_const_right(decay8, exp_ref[...])[0:1]

    blk = ones_ref[...]
    hid = _lane_head_id((T, W))
    causal = (lax.broadcasted_iota(jnp.int32, (T, T), 1)
              <= lax.broadcasted_iota(jnp.int32, (T, T), 0))
    head_rows = [blk[h * HEAD_DIM:h * HEAD_DIM + 1, :] for h in range(N_HEADS)]
    s_all = _dot_nt(jnp.concatenate([qb * hr for hr in head_rows], axis=0), kb)
    num = None
    den = None
    for h in range(N_HEADS):
        dm = cexp[:, h:h + 1] + a_t[h:h + 1, :]
        s = s_all[h * T:(h + 1) * T] * jnp.exp2(jnp.where(causal, dm, -jnp.inf))
        dn = _dot(s.astype(BF16), vb * head_rows[h])
        dd = jnp.where(hid == h, jnp.sum(s, axis=-1, keepdims=True), 0.0)
        num = dn if num is None else num + dn
        den = dd if den is None else den + dd
    n_row = n_st[0:1, :].astype(BF16)
    num = num + w_inter_x * _dot(qb, c_st[...].astype(BF16))
    den = den + w_inter_x * _dot(qb * n_row, blk)
    hh = num / jnp.maximum(jnp.abs(den), e_den_x)

    o_ref[r0:r0 + T, :] = _head_rms(hh, g_ref[...], blk) * out_gate.astype(F32)

    kw = kb * w_s_x.astype(BF16)
    c_st[...] = decay_x * c_st[...] + blk.astype(F32) * _dot_tn(kw, vb)
    n_st[...] = decay_x * n_st[...] + _dot(jnp.ones((SUBLANES, T), BF16), kw)
    m_st[...] = jnp.broadcast_to(m_new, m_st.shape)


def _hgrn_chunk(r0, y_ref, lf_ref, g_ref, ones_ref, mst_ref, pm_ref, o_ref, s_st):
    L = HGRN_L
    W = GROUP_W
    gain = g_ref[...]
    blk = ones_ref[...]
    blk_f = blk.astype(F32)
    tril = mst_ref[...]
    row = lax.broadcasted_iota(jnp.int32, (L, W), 0)

    def level_gap(g, lf2, c):
        if c == 2:
            return jnp.where(row % 2 == 1, lf2, 0.0)
        if c >= SUBLANES:
            mids = [b * c + c // 2 - 1 for b in range(L // c)]
            return g - jnp.concatenate(
                [jnp.broadcast_to(g[m:m + 1, :], (c, W)) for m in mids], axis=0)
        lo = jnp.concatenate([jnp.broadcast_to(g[m:m + 1, :], (SUBLANES, W))
                              for m in range(1, L, SUBLANES)], axis=0)
        hi = jnp.concatenate([jnp.broadcast_to(g[m:m + 1, :], (SUBLANES, W))
                              for m in range(5, L, SUBLANES)], axis=0)
        return g - jnp.where(row % SUBLANES < 4, lo, hi)

    def head_stack(xb):
        return jnp.concatenate([xb] * N_HEADS, axis=0) * blk

    qf = y_ref[r0:r0 + L, 0:W]
    key = y_ref[r0:r0 + L, W:2 * W]
    vb = y_ref[r0:r0 + L, 2 * W:3 * W]
    out_gate = y_ref[r0:r0 + L, 3 * W:4 * W]

    lf2 = lf_ref[r0:r0 + L, :] * LOG2_E
    g = _dot_const_left(tril, lf2)
    e_last = jnp.exp2(g[L - 1:L, :])
    q_hat = qf * jnp.exp2(g).astype(BF16)
    k_hat = key * jnp.exp2(g[L - 1:L, :] - g).astype(BF16)

    a = pm_ref[0] * _dot_nt(qf, head_stack(key))
    for i, c in enumerate(_hgrn_level_sizes()):
        e_i = jnp.exp2(-jnp.abs(level_gap(g, lf2, c))).astype(BF16)
        a = a + pm_ref[1 + i] * _dot_nt(qf * e_i, head_stack(key * e_i))
    o = _dot(a.astype(BF16), head_stack(vb)) + _dot_nt(q_hat, s_st[...].astype(BF16))

    o_ref[r0:r0 + L, :] = _head_rms(o, gain, blk) * out_gate.astype(F32)
    s_st[...] = s_st[...] * e_last + blk_f * _dot_tn(vb, k_hat)


def _swa_blocks(sink_ref, y_ref, o_ref, k_prev, v_prev, layer):
    T = SWA_T
    Wn = WINDOW
    QW = SWA_Q_HEADS * HEAD_DIM
    step = pl.program_id(1)

    k2 = [jnp.concatenate([k_prev[g], y_ref[:, QW + g * LANES:QW + (g + 1) * LANES]], axis=0)
          for g in range(SWA_KV_HEADS)]
    v2 = [jnp.concatenate([v_prev[g], y_ref[:, QW + (2 + g) * LANES:QW + (3 + g) * LANES]], axis=0)
          for g in range(SWA_KV_HEADS)]
    for g in range(SWA_KV_HEADS):
        k_prev[g] = k2[g][T:T + Wn]
        v_prev[g] = v2[g][T:T + Wn]

    qi = lax.broadcasted_iota(jnp.int32, (Wn, 2 * Wn), 0)
    ki = lax.broadcasted_iota(jnp.int32, (Wn, 2 * Wn), 1)
    band = (ki > qi) & (ki <= qi + Wn)
    band_first = band & (ki >= jnp.where(step > 0, 0, Wn))
    lane_w = lax.broadcasted_iota(jnp.int32, (Wn, LANES), 1) < HEAD_DIM
    lane_row = lax.broadcasted_iota(jnp.int32, (1, LANES), 1) < HEAD_DIM
    low_row = jnp.where(lane_row, 1.0, 0.0).astype(BF16)
    high_row = jnp.where(lane_row, 0.0, 1.0).astype(BF16)

    def scores(j, g):
        r0 = j * Wn
        kc = k2[g][r0:r0 + 2 * Wn]
        parts = []
        for c in (2 * g, 2 * g + 1):
            qc = y_ref[r0:r0 + Wn, c * LANES:(c + 1) * LANES]
            parts.append(qc * low_row)
            parts.append(qc * high_row)
        return _dot_nt(jnp.concatenate(parts, axis=0), kc)

    def finish(j, g, s):
        r0 = j * Wn
        mask = band_first if j == 0 else band
        vc = v2[g][r0:r0 + 2 * Wn]
        outs = []
        for r in range(SWA_GROUP):
            sink = sink_ref[layer, SWA_GROUP * g + r] * LOG2_E
            sr = jnp.where(mask, s[r * Wn:(r + 1) * Wn], -jnp.inf)
            m = jnp.maximum(jnp.max(sr, axis=-1, keepdims=True), sink)
            pexp = jnp.exp2(sr - m)
            den = jnp.sum(pexp, axis=-1, keepdims=True) + jnp.exp2(sink - m)
            outs.append(_dot(pexp.astype(BF16), vc) / den)
        c0 = 2 * g * LANES
        o_ref[r0:r0 + Wn, c0:c0 + LANES] = jnp.where(lane_w, outs[0], outs[1])
        o_ref[r0:r0 + Wn, c0 + LANES:c0 + 2 * LANES] = jnp.where(lane_w, outs[2], outs[3])

    return [(functools.partial(scores, j, g), functools.partial(finish, j, g))
            for j in range(T // Wn) for g in range(SWA_KV_HEADS)]


def _mixers_kernel(sink_ref, ml_ref, gt_ref, fb_ref, gm_ref, hb_ref, lf_ref, gh_ref, sw_ref,
                   ones_ref, tril_ref, exp_ref, tril64_ref, pm_ref,
                   om_ref, oh_ref, os_ref,
                   c_st, n_st, m_st, s_st, k_prev, v_prev, *, layer):
    @pl.when(pl.program_id(1) == 0)
    def _():
        for ref in (c_st, n_st, m_st, s_st, k_prev, v_prev):
            ref[...] = jnp.zeros_like(ref)

    swa_blocks = _swa_blocks(sink_ref, sw_ref, os_ref, k_prev, v_prev, layer)
    n_sub = MIX_ROWS // HGRN_L
    assert len(swa_blocks) == n_sub and n_sub % (MIX_ROWS // MLSTM_T) == 0
    per_mlstm = n_sub // (MIX_ROWS // MLSTM_T)
    for sub in range(n_sub):
        swa_scores, swa_finish = swa_blocks[sub]
        s = swa_scores()
        _hgrn_chunk(sub * HGRN_L, hb_ref, lf_ref, gh_ref, ones_ref, tril64_ref, pm_ref, oh_ref, s_st)
        swa_finish(s)
        if sub % per_mlstm == 0:
            _mlstm_chunk((sub // per_mlstm) * MLSTM_T, ml_ref, gt_ref, fb_ref, gm_ref, ones_ref,
                         tril_ref, exp_ref, om_ref, c_st, n_st, m_st)


def _mixers(ml, gt, hb, lf, sw, f_bias, mlstm_g, hgrn_g, sinks, layer, batch, consts):
    n = ml.shape[0]
    seq = n // batch
    T = MIX_ROWS
    steps = seq // T
    W4 = 4 * GROUP_W
    QW = SWA_Q_HEADS * HEAD_DIM
    row = lambda bi, ci: (bi * steps + ci, 0)
    lay = lambda bi, ci: (layer, 0, 0)
    cst = lambda bi, ci: (0, 0)
    blk_ones, tril, expand, tril64, pmask = consts
    return pl.pallas_call(
        functools.partial(_mixers_kernel, layer=layer),
        grid=(batch, steps),
        in_specs=[pl.BlockSpec(memory_space=pltpu.SMEM),
                  pl.BlockSpec((T, W4), row),
                  pl.BlockSpec((T, LANES), row),
                  pl.BlockSpec((None, 1, LANES), lay),
                  pl.BlockSpec((None, 1, GROUP_W), lay),
                  pl.BlockSpec((T, W4), row),
                  pl.BlockSpec((T, GROUP_W), row),
                  pl.BlockSpec((None, 1, GROUP_W), lay),
                  pl.BlockSpec((T, W4), row),
                  pl.BlockSpec((GROUP_W, GROUP_W), cst),
                  pl.BlockSpec((MLSTM_T, MLSTM_T), cst),
                  pl.BlockSpec((LANES, GROUP_W), cst),
                  pl.BlockSpec((HGRN_L, HGRN_L), cst),
                  pl.BlockSpec(pmask.shape, lambda bi, ci: (0, 0, 0))],
        out_specs=[pl.BlockSpec((T, GROUP_W), row),
                   pl.BlockSpec((T, GROUP_W), row),
                   pl.BlockSpec((T, QW), row)],
        out_shape=[jax.ShapeDtypeStruct((n, GROUP_W), F32),
                   jax.ShapeDtypeStruct((n, GROUP_W), F32),
                   jax.ShapeDtypeStruct((n, QW), F32)],
        scratch_shapes=[pltpu.VMEM((GROUP_W, GROUP_W), F32),
                        pltpu.VMEM((SUBLANES, GROUP_W), F32),
                        pltpu.VMEM((SUBLANES, LANES), F32),
                        pltpu.VMEM((GROUP_W, GROUP_W), F32),
                        pltpu.VMEM((SWA_KV_HEADS, WINDOW, LANES), BF16),
                        pltpu.VMEM((SWA_KV_HEADS, WINDOW, LANES), BF16)],
        compiler_params=_params(2),
        name="mixers",
    )(sinks, ml, gt, f_bias, mlstm_g, hb, lf, hgrn_g, sw, blk_ones, tril, expand, tril64, pmask)


def _tail_kernel(x_ref, mm_ref, mh_ref, ms_ref, p_ref, wo_ref, g2_ref, wu_ref, wd_ref, g3_ref,
                 wg_ref, wp_ref, g4_ref, o_ref):
    W = GROUP_W
    mix = _dot(mm_ref[...].astype(BF16), wo_ref[0:W, :])
    mix = mix + _dot(mh_ref[...].astype(BF16), wo_ref[W:2 * W, :])
    mix = mix + _dot(ms_ref[...].astype(BF16), wo_ref[2 * W:4 * W, :])
    x = x_ref[...] + mix
    emb = _rms(_dot(p_ref[...].astype(BF16), wp_ref[...]), g4_ref[...])

    h2 = _rms(x, g2_ref[...]).astype(BF16)
    mlp = None
    for c in range(D_FF // FF_CHUNK):
        u = _dot(h2, wu_ref[:, c * FF_CHUNK:(c + 1) * FF_CHUNK])
        act = jnp.square(jnp.maximum(u, 0.0)).astype(BF16)
        d = _dot(act, wd_ref[c * FF_CHUNK:(c + 1) * FF_CHUNK, :])
        mlp = d if mlp is None else mlp + d
    x = x + mlp

    h3 = _rms(x, g3_ref[...]).astype(BF16)
    for c in range(D_MODEL // GATE_CHUNK):
        cols = slice(c * GATE_CHUNK, (c + 1) * GATE_CHUNK)
        gate = _sigmoid(_dot(h3, wg_ref[:, cols]))
        o_ref[:, cols] = x[:, cols] + gate * emb[:, cols]


def _tail(x, mm, mh, ms, p, w_out, g2, w_up, w_down, g3, w_gate, w_proj, g4, layer):
    n = x.shape[0]
    tm = min(TAIL_TM, n)
    tiles = n // tm
    row = lambda i: (i, 0)
    lay = lambda i: (layer, 0, 0)
    once = pl.Buffered(1)

    def wspec(shape):
        return pl.BlockSpec((None,) + shape, lay, pipeline_mode=once)

    return pl.pallas_call(
        _tail_kernel,
        grid=(tiles,),
        in_specs=[pl.BlockSpec((tm, D_MODEL), row),
                  pl.BlockSpec((tm, GROUP_W), row),
                  pl.BlockSpec((tm, GROUP_W), row),
                  pl.BlockSpec((tm, 2 * GROUP_W), row),
                  pl.BlockSpec((None, tm, PLE_DIM), lambda i: (layer, i, 0)),
                  wspec((D_MODEL, D_MODEL)),
                  wspec((1, D_MODEL)),
                  wspec((D_MODEL, D_FF)),
                  wspec((D_FF, D_MODEL)),
                  wspec((1, D_MODEL)),
                  wspec((D_MODEL, D_MODEL)),
                  wspec((PLE_DIM, D_MODEL)),
                  wspec((1, D_MODEL))],
        out_specs=pl.BlockSpec((tm, D_MODEL), row),
        out_shape=jax.ShapeDtypeStruct((n, D_MODEL), F32),
        compiler_params=_params(1),
        name="tail",
    )(x, mm, mh, ms, p, w_out, g2, w_up, w_down, g3, w_gate, w_proj, g4)


def kernel(x, p, positions, in_norm_g, w_in, b_in, mlstm_f_bias, mlstm_conv_w, mlstm_conv_b,
           mlstm_norm_g, hgrn_lb_logits, hgrn_norm_g, swa_q_norm_g, swa_k_norm_g, swa_sinks,
           w_out, mlp_norm_g, w_up, w_down, ple_norm_g, w_ple_gate, w_ple_proj, ple_post_norm_g):
    batch, seq, d_model = x.shape
    depth = w_in.shape[0]
    n = batch * seq
    assert d_model == D_MODEL and seq % max(MIX_ROWS, IN_TM) == 0

    n_raw = GATE_OFF + GATE_COLS
    pad = LANES - GATE_COLS
    w_in_p = jnp.concatenate(
        [w_in[..., :n_raw], jnp.zeros((depth, D_MODEL, pad), w_in.dtype), w_in[..., n_raw:]],
        axis=-1).astype(BF16)
    b_in_p = jnp.concatenate(
        [b_in[..., :n_raw], jnp.zeros((depth, pad), b_in.dtype), b_in[..., n_raw:]],
        axis=-1).astype(F32).reshape(depth, 1, IN_COLS)
    f_bias = jnp.zeros((depth, 1, LANES), F32).at[:, 0, N_HEADS:GATE_COLS].set(mlstm_f_bias.astype(F32))
    row3 = lambda a: a.astype(F32).reshape(depth, 1, a.shape[-1])
    q_gain = row3(jnp.tile(swa_q_norm_g, (1, LANES // HEAD_DIM)))
    k_gain = row3(jnp.tile(swa_k_norm_g, (1, LANES // HEAD_DIM)))
    w_out_b, w_up_b, w_down_b = w_out.astype(BF16), w_up.astype(BF16), w_down.astype(BF16)
    w_gate_b, w_proj_b = w_ple_gate.astype(BF16), w_ple_proj.astype(BF16)

    blk256 = _blk_ones(GROUP_W)
    tril = jnp.asarray(np.tril(np.ones((MLSTM_T, MLSTM_T), np.float32)), dtype=BF16)
    expand = jnp.asarray(
        (np.arange(LANES)[:, None] == (np.arange(GROUP_W) // HEAD_DIM)[None, :]).astype(np.float32),
        dtype=BF16)
    hgrn_tril, hgrn_pmask = _hgrn_constants()

    cos_t, sin_t = _rope_tables(positions)
    la, l1m, oml = [row3(a) for a in _hgrn_lower_bounds(hgrn_lb_logits)]
    sinks = swa_sinks.astype(F32)

    xf = x.reshape(n, D_MODEL)
    for l in range(depth):
        ml, gt, hb, lf, sw = _in_proj(
            xf, row3(in_norm_g), w_in_p, b_in_p, mlstm_conv_w.astype(F32), row3(mlstm_conv_b),
            la, l1m, oml, cos_t, sin_t, q_gain, k_gain, blk256, l, seq)
        mm, mh, ms = _mixers(ml, gt, hb, lf, sw, f_bias, row3(mlstm_norm_g), row3(hgrn_norm_g), sinks,
                             l, batch, (blk256, tril, expand, hgrn_tril, hgrn_pmask))
        xf = _tail(xf, mm, mh, ms, p.reshape(depth, n, PLE_DIM), w_out_b, row3(mlp_norm_g), w_up_b,
                   w_down_b, row3(ple_norm_g), w_gate_b, w_proj_b, row3(ple_post_norm_g), l)
    return xf.reshape(batch, seq, D_MODEL)
```

```python
import functools

import numpy as np
import jax
import jax.numpy as jnp
from jax import lax
from jax.experimental import pallas as pl
from jax.experimental.pallas import tpu as pltpu

F32 = jnp.float32
BF16 = jnp.bfloat16

D_MODEL = 1024
N_HEADS = 4
HEAD_DIM = 64
GROUP_W = N_HEADS * HEAD_DIM
CONV_K = 4
SWA_Q_HEADS = 8
SWA_KV_HEADS = 2
SWA_GROUP = SWA_Q_HEADS // SWA_KV_HEADS
WINDOW = 128
ROPE_THETA = 500000.0
ROT_DIM = HEAD_DIM // 4
ROT_HALF = ROT_DIM // 2
D_FF = 4 * D_MODEL
PLE_DIM = 256
EPS = 1e-6
LOG2_E = 1.4426950408889634

LANES = 128
SUBLANES = 8
VMEM_LIMIT_BYTES = 56 * 1024 * 1024

GATE_COLS = 2 * N_HEADS
M_COLS = 4 * GROUP_W + LANES
H_COLS = 4 * GROUP_W
S_COLS = SWA_Q_HEADS * HEAD_DIM + 2 * SWA_KV_HEADS * HEAD_DIM
IN_COLS = M_COLS + H_COLS + S_COLS
GATE_OFF = 4 * GROUP_W

IN_TM = 512
TAIL_TM = 512
FF_CHUNK = 1024
GATE_CHUNK = 256
ROPE_TM = 2048
MIX_ROWS = 512
MLSTM_T = 256
HGRN_L = 64
HGRN_SAFE_LOG2 = 100.0
SWA_T = MIX_ROWS

N_SPLIT = 2
assert HGRN_L == HEAD_DIM


def _params(n_grid_dims):
    return pltpu.CompilerParams(
        dimension_semantics=("arbitrary",) * n_grid_dims,
        vmem_limit_bytes=VMEM_LIMIT_BYTES)


def _dot(a, b):
    return jnp.dot(a, b, preferred_element_type=F32)


def _dot_nt(a, b):
    return lax.dot_general(a, b, (((1,), (1,)), ((), ())), preferred_element_type=F32)


def _dot_tn(a, b):
    return lax.dot_general(a, b, (((0,), (0,)), ((), ())), preferred_element_type=F32)


def _split(x, n=N_SPLIT):
    pieces = []
    r = x
    for _ in range(n):
        p = r.astype(BF16)
        pieces.append(p)
        r = r - p.astype(F32)
    return pieces


def _dot_const_left(m, x):
    acc = None
    for p in _split(x):
        t = _dot(m, p)
        acc = t if acc is None else acc + t
    return acc


def _dot_const_right(x, m):
    acc = None
    for p in _split(x):
        t = _dot(p, m)
        acc = t if acc is None else acc + t
    return acc


def _sigmoid(x):
    return 1.0 / (1.0 + jnp.exp(-x))


def _log_sigmoid(x):
    return jnp.minimum(x, 0.0) - jnp.log1p(jnp.exp(-jnp.abs(x)))


def _rms(x, g):
    return x * lax.rsqrt(jnp.mean(x * x, axis=-1, keepdims=True) + EPS) * g


def _head_rms(x, g, blk_ones):
    ms = _dot(jnp.square(x).astype(BF16), blk_ones) * (1.0 / HEAD_DIM)
    return x * lax.rsqrt(ms + EPS) * g


def _lane_head_id(shape):
    return lax.broadcasted_iota(jnp.int32, shape, len(shape) - 1) // HEAD_DIM


def _blk_ones(width):
    hid = np.arange(width) // HEAD_DIM
    return jnp.asarray((hid[:, None] == hid[None, :]).astype(np.float32), dtype=BF16)


def _hgrn_level_sizes():
    sizes = []
    c = 2
    while c <= HGRN_L:
        sizes.append(c)
        c *= 2
    return sizes


def _hgrn_constants():
    L = HGRN_L
    t = np.arange(L)[:, None]
    u = np.arange(L)[None, :]
    masks = [(t == u)]
    for c in _hgrn_level_sizes():
        mid = (t // c) * c + c // 2 - 1
        same_blk = (t // c) == (u // c)
        u_mid = (u // c) * c + c // 2 - 1
        masks.append(same_blk & (t > mid) & (u <= u_mid))
    masks.append(u <= t)
    tril = (u <= t).astype(np.float32)
    pmask = np.stack([np.tile(m.astype(np.float32), (1, N_HEADS)) for m in masks])
    return jnp.asarray(tril, dtype=BF16), jnp.asarray(pmask, dtype=F32)


ROPE_PER_ROW = LANES // ROT_HALF


def _rope_kernel(pos_ref, freq_ref, sel_ref, one_ref, cos_ref, sin_ref):
    ang = pos_ref[...].astype(F32) * freq_ref[0:1, :]
    cos_ref[...] = _dot_const_right(jnp.cos(ang), sel_ref[0]) + one_ref[0:1, :]
    sin_ref[...] = _dot_const_right(jnp.sin(ang), sel_ref[1])


def _rope_tables(positions):
    n = positions.size
    per_row = ROPE_PER_ROW
    rows = n // per_row
    wide = per_row * LANES
    inv_freq = ROPE_THETA ** (-jnp.arange(0, ROT_DIM, 2, dtype=F32) / ROT_DIM)
    freq = jnp.zeros((SUBLANES, LANES), F32).at[0].set(jnp.tile(inv_freq, per_row))
    pos = jnp.repeat(positions.reshape(rows, per_row), ROT_HALF, axis=1)
    col = np.arange(wide)
    tok, lane = col // LANES, (col % LANES) % HEAD_DIM
    rotated = lane < ROT_DIM
    pick = (np.arange(LANES)[:, None] == (tok * ROT_HALF + lane % ROT_HALF)[None, :]) & rotated[None, :]
    sign = np.where(lane < ROT_HALF, -1.0, 1.0)[None, :]
    sel = jnp.asarray(np.stack([pick.astype(np.float32), pick * sign]), dtype=BF16)
    unrotated = jnp.asarray(np.broadcast_to((~rotated).astype(np.float32)[None, :], (SUBLANES, wide)))
    tm = min(ROPE_TM // per_row, rows)
    out = jax.ShapeDtypeStruct((rows, wide), F32)
    cos_w, sin_w = pl.pallas_call(
        _rope_kernel,
        grid=(rows // tm,),
        in_specs=[pl.BlockSpec((tm, LANES), lambda i: (i, 0)),
                  pl.BlockSpec((SUBLANES, LANES), lambda i: (0, 0)),
                  pl.BlockSpec(sel.shape, lambda i: (0, 0, 0)),
                  pl.BlockSpec((SUBLANES, wide), lambda i: (0, 0))],
        out_specs=[pl.BlockSpec((tm, wide), lambda i: (i, 0))] * 2,
        out_shape=[out, out],
        compiler_params=_params(1),
        name="rope_tables",
    )(pos, freq, sel, unrotated)
    return cos_w.reshape(n, LANES), sin_w.reshape(n, LANES)


def _lb_kernel(lg_ref, la_ref, l1m_ref, oml_ref):
    z = lg_ref[...]
    e = jnp.exp(z - jnp.max(z, axis=0, keepdims=True))
    sm = e / jnp.sum(e, axis=0, keepdims=True)
    depth = z.shape[0]
    run = sm[0:1, :]
    first = run
    for l in range(depth):
        if l > 0:
            run = run + sm[l:l + 1, :]
        lb = run - first
        la_ref[l:l + 1, :] = jnp.log(lb)
        l1m_ref[l:l + 1, :] = jnp.log1p(-lb)
        oml_ref[l:l + 1, :] = 1.0 - lb


def _hgrn_lower_bounds(logits):
    out = jax.ShapeDtypeStruct(logits.shape, F32)
    return pl.pallas_call(
        _lb_kernel, out_shape=[out, out, out], name="hgrn_lower_bounds",
    )(logits.astype(F32))


def _in_proj_kernel(x_ref, g_ref, w_ref, b_ref, cw_ref, cb_ref, la_ref, l1m_ref, oml_ref,
                    cos_ref, sin_ref, qg_ref, kg_ref, ones_ref,
                    ml_ref, gt_ref, hb_ref, lf_ref, sw_ref, xbuf, *, tiles_per_seq):
    TM = x_ref.shape[0]
    W = GROUP_W

    @pl.when(pl.program_id(0) % tiles_per_seq == 0)
    def _():
        xbuf[0:SUBLANES, :] = jnp.zeros((SUBLANES, 2 * W), F32)

    h = _rms(x_ref[...], g_ref[...]).astype(BF16)

    def proj(lo, width):
        return _dot(h, w_ref[:, lo:lo + width]) + b_ref[:, lo:lo + width]

    cos_t = cos_ref[...]
    sin_t = sin_ref[...]
    blk = ones_ref[...]
    lane = lax.broadcasted_iota(jnp.int32, (TM, LANES), 1)
    low_half = lane < HEAD_DIM
    rot_first = (lane % HEAD_DIM) < ROT_HALF
    s0 = M_COLS + H_COLS
    QW = SWA_Q_HEADS * HEAD_DIM

    def head_sumsq(y):
        width = y.shape[-1]
        return _dot(jnp.square(y).astype(BF16), blk[0:width, 0:width])

    def norm_rope(v, sumsq, gain):
        xn = v * lax.rsqrt(sumsq * (1.0 / HEAD_DIM) + EPS) * gain
        partner = jnp.where(rot_first, pltpu.roll(xn, LANES - ROT_HALF, 1), pltpu.roll(xn, ROT_HALF, 1))
        return xn * cos_t + partner * sin_t

    def swa_q(c, y, sumsq):
        for half in range(2):
            sl = slice(half * LANES, (half + 1) * LANES)
            sw_ref[:, (c + half) * LANES:(c + half + 1) * LANES] = (
                norm_rope(y[:, sl], sumsq[:, sl], qg_ref[...]) * (HEAD_DIM ** -0.5 * LOG2_E)).astype(BF16)

    def swa_kv(y, sumsq):
        k = norm_rope(y[:, 0:LANES], sumsq, kg_ref[...])
        v = y[:, LANES:2 * LANES]
        k_sw = pltpu.roll(k, HEAD_DIM, 1)
        v_sw = pltpu.roll(v, HEAD_DIM, 1)
        sw_ref[:, QW:QW + LANES] = jnp.where(low_half, k, k_sw).astype(BF16)
        sw_ref[:, QW + LANES:QW + 2 * LANES] = jnp.where(low_half, k_sw, k).astype(BF16)
        sw_ref[:, QW + 2 * LANES:QW + 3 * LANES] = jnp.where(low_half, v, v_sw).astype(BF16)
        sw_ref[:, QW + 3 * LANES:QW + 4 * LANES] = jnp.where(low_half, v_sw, v).astype(BF16)

    def mlstm_qk(y):
        xbuf[SUBLANES:SUBLANES + TM, :] = y
        acc = cb_ref[...] + cw_ref[CONV_K - 1:CONV_K, :] * xbuf[SUBLANES:SUBLANES + TM, :]
        for j in range(1, CONV_K):
            acc = acc + cw_ref[CONV_K - 1 - j:CONV_K - j, :] * xbuf[SUBLANES - j:SUBLANES - j + TM, :]
        xbuf[0:SUBLANES, :] = xbuf[TM:TM + SUBLANES, :]
        qk = acc * _sigmoid(acc)
        ml_ref[:, 0:W] = qk[:, 0:W].astype(BF16)
        ml_ref[:, W:2 * W] = (qk[:, W:2 * W] * (HEAD_DIM ** -0.5)).astype(BF16)

    def hgrn_qf(y):
        hq = y[:, 0:W]
        hf = y[:, W:2 * W]
        la = la_ref[...]
        u = jnp.exp(-jnp.abs(hf))
        t = l1m_ref[...] + (jnp.minimum(hf, 0.0) - jnp.log1p(u))
        lf_ref[...] = jnp.maximum(la, t) + jnp.log1p(jnp.exp(-jnp.abs(la - t)))
        hb_ref[:, 0:W] = (hq * _sigmoid(hq)).astype(BF16)
        hb_ref[:, W:2 * W] = (oml_ref[...] * jnp.where(hf >= 0.0, u, 1.0) / (1.0 + u)).astype(BF16)

    y_h = proj(M_COLS, 2 * W)
    y_m = proj(0, 2 * W)
    hgrn_qf(y_h)
    y_kv = proj(s0 + QW, 2 * LANES)
    mlstm_qk(y_m)
    y_q0 = proj(s0, 2 * LANES)
    swa_kv(y_kv, head_sumsq(y_kv[:, 0:LANES]))
    y_q1 = proj(s0 + 2 * LANES, 2 * LANES)
    swa_q(0, y_q0, head_sumsq(y_q0))
    y_h2 = proj(M_COLS + 2 * W, 2 * W)
    swa_q(2, y_q1, head_sumsq(y_q1))
    y_m2 = proj(2 * W, 2 * W)
    hg = y_h2[:, W:2 * W]
    hb_ref[:, 2 * W:3 * W] = y_h2[:, 0:W].astype(BF16)
    hb_ref[:, 3 * W:4 * W] = (hg * _sigmoid(hg)).astype(BF16)
    gt_ref[...] = proj(GATE_OFF, LANES)
    ml_ref[:, 2 * W:3 * W] = y_m2[:, 0:W].astype(BF16)
    ml_ref[:, 3 * W:4 * W] = _sigmoid(y_m2[:, W:2 * W]).astype(BF16)


def _in_proj(x, g, w, b, conv_w, conv_b, la, l1m, oml, cos_t, sin_t, q_gain, k_gain, blk_ones,
             layer, seq):
    n = x.shape[0]
    tm = min(IN_TM, n)
    row = lambda i: (i, 0)
    lay = lambda i: (layer, 0, 0)
    W4 = 4 * GROUP_W
    return pl.pallas_call(
        functools.partial(_in_proj_kernel, tiles_per_seq=seq // tm),
        grid=(n // tm,),
        in_specs=[pl.BlockSpec((tm, D_MODEL), row),
                  pl.BlockSpec((None, 1, D_MODEL), lay),
                  pl.BlockSpec((None, D_MODEL, IN_COLS), lay),
                  pl.BlockSpec((None, 1, IN_COLS), lay),
                  pl.BlockSpec((None, CONV_K, 2 * GROUP_W), lay),
                  pl.BlockSpec((None, 1, 2 * GROUP_W), lay),
                  pl.BlockSpec((None, 1, GROUP_W), lay),
                  pl.BlockSpec((None, 1, GROUP_W), lay),
                  pl.BlockSpec((None, 1, GROUP_W), lay),
                  pl.BlockSpec((tm, LANES), row),
                  pl.BlockSpec((tm, LANES), row),
                  pl.BlockSpec((None, 1, LANES), lay),
                  pl.BlockSpec((None, 1, LANES), lay),
                  pl.BlockSpec((GROUP_W, GROUP_W), lambda i: (0, 0))],
        out_specs=[pl.BlockSpec((tm, W4), row),
                   pl.BlockSpec((tm, LANES), row),
                   pl.BlockSpec((tm, W4), row),
                   pl.BlockSpec((tm, GROUP_W), row),
                   pl.BlockSpec((tm, W4), row)],
        out_shape=[jax.ShapeDtypeStruct((n, W4), BF16),
                   jax.ShapeDtypeStruct((n, LANES), F32),
                   jax.ShapeDtypeStruct((n, W4), BF16),
                   jax.ShapeDtypeStruct((n, GROUP_W), F32),
                   jax.ShapeDtypeStruct((n, W4), BF16)],
        scratch_shapes=[pltpu.VMEM((tm + SUBLANES, 2 * GROUP_W), F32)],
        compiler_params=_params(1),
        name="in_proj",
    )(x, g, w, b, conv_w, conv_b, la, l1m, oml, cos_t, sin_t, q_gain, k_gain, blk_ones)


def _mlstm_chunk(r0, y_ref, gt_ref, fb_ref, g_ref, ones_ref, tril_ref, exp_ref, o_ref,
                 c_st, n_st, m_st):
    T = MLSTM_T
    W = GROUP_W
    qb = y_ref[r0:r0 + T, 0:W]
    kb = y_ref[r0:r0 + T, W:2 * W]
    vb = y_ref[r0:r0 + T, 2 * W:3 * W]
    out_gate = y_ref[r0:r0 + T, 3 * W:4 * W]

    gates = gt_ref[r0:r0 + T, :]
    i_pre = gates
    lf = pltpu.roll(_log_sigmoid(gates + fb_ref[...]), LANES - N_HEADS, 1)
    lf_hi = lf.astype(BF16)
    lf_lo = (lf - lf_hi.astype(F32)).astype(BF16)
    bb = _dot(tril_ref[...], jnp.concatenate([lf_hi, lf_lo], axis=1))
    b = bb[:, 0:LANES] + bb[:, LANES:2 * LANES]
    a = i_pre - b
    row = lax.broadcasted_iota(jnp.int32, (T, LANES), 0)
    cm = a
    sh = 1
    while sh < T:
        cm = jnp.maximum(cm, jnp.where(row >= sh, pltpu.roll(cm, sh, 0), -jnp.inf))
        sh *= 2
    m_prev = m_st[0:1, :]
    m_j = b + jnp.maximum(m_prev, cm)
    m_new = m_j[T - 1:T, :]
    b_last = b[T - 1:T, :]
    w_inter = jnp.exp(b + m_prev - m_j)
    e_den = jnp.exp(-m_j)
    w_s = jnp.exp(b_last + a - m_new)
    decay = jnp.exp(b_last + m_prev - m_new)
    cexp = (b - m_j) * LOG2_E
    a_t = (a * LOG2_E).T

    stacked = jnp.concatenate([w_inter, e_den, w_s], axis=0)
    head_lane = lax.broadcasted_iota(jnp.int32, stacked.shape, 1) < N_HEADS
    ex = _dot(jnp.where(head_lane, stacked, 0.0).astype(BF16), exp_ref[...])
    w_inter_x = ex[0:T]
    e_den_x = ex[T:2 * T]
    w_s_x = ex[2 * T:3 * T]
    decay8 = jnp.where(head_lane[0:SUBLANES], jnp.broadcast_to(decay, (SUBLANES, LANES)), 0.0)
    decay_x = _dot_const_right(decay8, exp_ref[...])[0:1]

    blk = ones_ref[...]
    hid = _lane_head_id((T, W))
    causal = (lax.broadcasted_iota(jnp.int32, (T, T), 1)
              <= lax.broadcasted_iota(jnp.int32, (T, T), 0))
    head_rows = [blk[h * HEAD_DIM:h * HEAD_DIM + 1, :] for h in range(N_HEADS)]
    s_all = _dot_nt(jnp.concatenate([qb * hr for hr in head_rows], axis=0), kb)
    num = None
    den = None
    for h in range(N_HEADS):
        dm = cexp[:, h:h + 1] + a_t[h:h + 1, :]
        s = s_all[h * T:(h + 1) * T] * jnp.exp2(jnp.where(causal, dm, -jnp.inf))
        dn = _dot(s.astype(BF16), vb * head_rows[h])
        dd = jnp.where(hid == h, jnp.sum(s, axis=-1, keepdims=True), 0.0)
        num = dn if num is None else num + dn
        den = dd if den is None else den + dd
    n_row = n_st[0:1, :].astype(BF16)
    num = num + w_inter_x * _dot(qb, c_st[...].astype(BF16))
    den = den + w_inter_x * _dot(qb * n_row, blk)
    hh = num / jnp.maximum(jnp.abs(den), e_den_x)

    o_ref[r0:r0 + T, :] = _head_rms(hh, g_ref[...], blk) * out_gate.astype(F32)

    kw = kb * w_s_x.astype(BF16)
    c_st[...] = decay_x * c_st[...] + blk.astype(F32) * _dot_tn(kw, vb)
    n_st[...] = decay_x * n_st[...] + _dot(jnp.ones((SUBLANES, T), BF16), kw)
    m_st[...] = jnp.broadcast_to(m_new, m_st.shape)


def _hgrn_chunk(r0, y_ref, lf_ref, g_ref, ones_ref, mst_ref, pm_ref, o_ref, s_st, bounded):
    L = HGRN_L
    W = GROUP_W
    gain = g_ref[...]
    blk = ones_ref[...]
    blk_f = blk.astype(F32)
    tril = mst_ref[...]
    row = lax.broadcasted_iota(jnp.int32, (L, W), 0)

    def level_gap(g, lf2, c):
        if c == 2:
            return jnp.where(row % 2 == 1, lf2, 0.0)
        if c >= SUBLANES:
            mids = [b * c + c // 2 - 1 for b in range(L // c)]
            return g - jnp.concatenate(
                [jnp.broadcast_to(g[m:m + 1, :], (c, W)) for m in mids], axis=0)
        lo = jnp.concatenate([jnp.broadcast_to(g[m:m + 1, :], (SUBLANES, W))
                              for m in range(1, L, SUBLANES)], axis=0)
        hi = jnp.concatenate([jnp.broadcast_to(g[m:m + 1, :], (SUBLANES, W))
                              for m in range(5, L, SUBLANES)], axis=0)
        return g - jnp.where(row % SUBLANES < 4, lo, hi)

    def head_stack(xb):
        return jnp.concatenate([xb] * N_HEADS, axis=0) * blk

    qf = y_ref[r0:r0 + L, 0:W]
    key = y_ref[r0:r0 + L, W:2 * W]
    vb = y_ref[r0:r0 + L, 2 * W:3 * W]
    out_gate = y_ref[r0:r0 + L, 3 * W:4 * W]

    lf2 = lf_ref[r0:r0 + L, :] * LOG2_E
    g = _dot_const_left(tril, lf2)
    e_last = jnp.exp2(g[L - 1:L, :])
    q_hat = qf * jnp.exp2(g).astype(BF16)
    k_hat = key * jnp.exp2(g[L - 1:L, :] - g).astype(BF16)

    if bounded:
        g_mid = g[L // 2 - 1:L // 2, :]
        a = pm_ref[pm_ref.shape[0] - 1] * _dot_nt(
            qf * jnp.exp2(g - g_mid).astype(BF16), head_stack(key * jnp.exp2(g_mid - g).astype(BF16)))
    else:
        a = pm_ref[0] * _dot_nt(qf, head_stack(key))
        for i, c in enumerate(_hgrn_level_sizes()):
            e_i = jnp.exp2(-jnp.abs(level_gap(g, lf2, c))).astype(BF16)
            a = a + pm_ref[1 + i] * _dot_nt(qf * e_i, head_stack(key * e_i))
    o = _dot(a.astype(BF16), head_stack(vb)) + _dot_nt(q_hat, s_st[...].astype(BF16))

    o_ref[r0:r0 + L, :] = _head_rms(o, gain, blk) * out_gate.astype(F32)
    s_st[...] = s_st[...] * e_last + blk_f * _dot_tn(vb, k_hat)


def _swa_blocks(sink_ref, y_ref, o_ref, k_prev, v_prev, layer):
    T = SWA_T
    Wn = WINDOW
    QW = SWA_Q_HEADS * HEAD_DIM
    step = pl.program_id(1)

    k2 = [jnp.concatenate([k_prev[g], y_ref[:, QW + g * LANES:QW + (g + 1) * LANES]], axis=0)
          for g in range(SWA_KV_HEADS)]
    v2 = [jnp.concatenate([v_prev[g], y_ref[:, QW + (2 + g) * LANES:QW + (3 + g) * LANES]], axis=0)
          for g in range(SWA_KV_HEADS)]
    for g in range(SWA_KV_HEADS):
        k_prev[g] = k2[g][T:T + Wn]
        v_prev[g] = v2[g][T:T + Wn]

    qi = lax.broadcasted_iota(jnp.int32, (Wn, 2 * Wn), 0)
    ki = lax.broadcasted_iota(jnp.int32, (Wn, 2 * Wn), 1)
    band = (ki > qi) & (ki <= qi + Wn)
    band_first = band & (ki >= jnp.where(step > 0, 0, Wn))
    lane_w = lax.broadcasted_iota(jnp.int32, (Wn, LANES), 1) < HEAD_DIM
    lane_row = lax.broadcasted_iota(jnp.int32, (1, LANES), 1) < HEAD_DIM
    low_row = jnp.where(lane_row, 1.0, 0.0).astype(BF16)
    high_row = jnp.where(lane_row, 0.0, 1.0).astype(BF16)

    def scores(j, g):
        r0 = j * Wn
        kc = k2[g][r0:r0 + 2 * Wn]
        parts = []
        for c in (2 * g, 2 * g + 1):
            qc = y_ref[r0:r0 + Wn, c * LANES:(c + 1) * LANES]
            parts.append(qc * low_row)
            parts.append(qc * high_row)
        return _dot_nt(jnp.concatenate(parts, axis=0), kc)

    def finish(j, g, s):
        r0 = j * Wn
        mask = band_first if j == 0 else band
        vc = v2[g][r0:r0 + 2 * Wn]
        outs = []
        for r in range(SWA_GROUP):
            sink = sink_ref[layer, SWA_GROUP * g + r] * LOG2_E
            sr = jnp.where(mask, s[r * Wn:(r + 1) * Wn], -jnp.inf)
            m = jnp.maximum(jnp.max(sr, axis=-1, keepdims=True), sink)
            pexp = jnp.exp2(sr - m)
            den = jnp.sum(pexp, axis=-1, keepdims=True) + jnp.exp2(sink - m)
            outs.append(_dot(pexp.astype(BF16), vc) / den)
        c0 = 2 * g * LANES
        o_ref[r0:r0 + Wn, c0:c0 + LANES] = jnp.where(lane_w, outs[0], outs[1])
        o_ref[r0:r0 + Wn, c0 + LANES:c0 + 2 * LANES] = jnp.where(lane_w, outs[2], outs[3])

    return [(functools.partial(scores, j, g), functools.partial(finish, j, g))
            for j in range(T // Wn) for g in range(SWA_KV_HEADS)]


def _mixers_kernel(sink_ref, ml_ref, gt_ref, fb_ref, gm_ref, hb_ref, lf_ref, gh_ref, sw_ref,
                   ones_ref, tril_ref, exp_ref, tril64_ref, pm_ref,
                   om_ref, oh_ref, os_ref,
                   c_st, n_st, m_st, s_st, k_prev, v_prev, *, layer, bounded):
    @pl.when(pl.program_id(1) == 0)
    def _():
        for ref in (c_st, n_st, m_st, s_st, k_prev, v_prev):
            ref[...] = jnp.zeros_like(ref)

    swa_blocks = _swa_blocks(sink_ref, sw_ref, os_ref, k_prev, v_prev, layer)
    n_sub = MIX_ROWS // HGRN_L
    assert len(swa_blocks) == n_sub and n_sub % (MIX_ROWS // MLSTM_T) == 0
    per_mlstm = n_sub // (MIX_ROWS // MLSTM_T)
    for sub in range(n_sub):
        swa_scores, swa_finish = swa_blocks[sub]
        s = swa_scores()
        _hgrn_chunk(sub * HGRN_L, hb_ref, lf_ref, gh_ref, ones_ref, tril64_ref, pm_ref, oh_ref, s_st,
                    bounded)
        swa_finish(s)
        if sub % per_mlstm == 0:
            _mlstm_chunk((sub // per_mlstm) * MLSTM_T, ml_ref, gt_ref, fb_ref, gm_ref, ones_ref,
                         tril_ref, exp_ref, om_ref, c_st, n_st, m_st)


def _mixers(ml, gt, hb, lf, sw, f_bias, mlstm_g, hgrn_g, sinks, layer, batch, consts, bounded):
    n = ml.shape[0]
    seq = n // batch
    T = MIX_ROWS
    steps = seq // T
    W4 = 4 * GROUP_W
    QW = SWA_Q_HEADS * HEAD_DIM
    row = lambda bi, ci: (bi * steps + ci, 0)
    lay = lambda bi, ci: (layer, 0, 0)
    cst = lambda bi, ci: (0, 0)
    blk_ones, tril, expand, tril64, pmask = consts
    return pl.pallas_call(
        functools.partial(_mixers_kernel, layer=layer, bounded=bounded),
        grid=(batch, steps),
        in_specs=[pl.BlockSpec(memory_space=pltpu.SMEM),
                  pl.BlockSpec((T, W4), row),
                  pl.BlockSpec((T, LANES), row),
                  pl.BlockSpec((None, 1, LANES), lay),
                  pl.BlockSpec((None, 1, GROUP_W), lay),
                  pl.BlockSpec((T, W4), row),
                  pl.BlockSpec((T, GROUP_W), row),
                  pl.BlockSpec((None, 1, GROUP_W), lay),
                  pl.BlockSpec((T, W4), row),
                  pl.BlockSpec((GROUP_W, GROUP_W), cst),
                  pl.BlockSpec((MLSTM_T, MLSTM_T), cst),
                  pl.BlockSpec((LANES, GROUP_W), cst),
                  pl.BlockSpec((HGRN_L, HGRN_L), cst),
                  pl.BlockSpec(pmask.shape, lambda bi, ci: (0, 0, 0))],
        out_specs=[pl.BlockSpec((T, GROUP_W), row),
                   pl.BlockSpec((T, GROUP_W), row),
                   pl.BlockSpec((T, QW), row)],
        out_shape=[jax.ShapeDtypeStruct((n, GROUP_W), F32),
                   jax.ShapeDtypeStruct((n, GROUP_W), F32),
                   jax.ShapeDtypeStruct((n, QW), F32)],
        scratch_shapes=[pltpu.VMEM((GROUP_W, GROUP_W), F32),
                        pltpu.VMEM((SUBLANES, GROUP_W), F32),
                        pltpu.VMEM((SUBLANES, LANES), F32),
                        pltpu.VMEM((GROUP_W, GROUP_W), F32),
                        pltpu.VMEM((SWA_KV_HEADS, WINDOW, LANES), BF16),
                        pltpu.VMEM((SWA_KV_HEADS, WINDOW, LANES), BF16)],
        compiler_params=_params(2),
        name="mixers",
    )(sinks, ml, gt, f_bias, mlstm_g, hb, lf, hgrn_g, sw, blk_ones, tril, expand, tril64, pmask)


def _tail_kernel(x_ref, mm_ref, mh_ref, ms_ref, p_ref, wo_ref, g2_ref, wu_ref, wd_ref, g3_ref,
                 wg_ref, wp_ref, g4_ref, o_ref):
    W = GROUP_W
    mix = _dot(mm_ref[...].astype(BF16), wo_ref[0:W, :])
    mix = mix + _dot(mh_ref[...].astype(BF16), wo_ref[W:2 * W, :])
    mix = mix + _dot(ms_ref[...].astype(BF16), wo_ref[2 * W:4 * W, :])
    x = x_ref[...] + mix
    emb = _rms(_dot(p_ref[...].astype(BF16), wp_ref[...]), g4_ref[...])

    h2 = _rms(x, g2_ref[...]).astype(BF16)
    mlp = None
    for c in range(D_FF // FF_CHUNK):
        u = _dot(h2, wu_ref[:, c * FF_CHUNK:(c + 1) * FF_CHUNK])
        act = jnp.square(jnp.maximum(u, 0.0)).astype(BF16)
        d = _dot(act, wd_ref[c * FF_CHUNK:(c + 1) * FF_CHUNK, :])
        mlp = d if mlp is None else mlp + d
    x = x + mlp

    h3 = _rms(x, g3_ref[...]).astype(BF16)
    for c in range(D_MODEL // GATE_CHUNK):
        cols = slice(c * GATE_CHUNK, (c + 1) * GATE_CHUNK)
        gate = _sigmoid(_dot(h3, wg_ref[:, cols]))
        o_ref[:, cols] = x[:, cols] + gate * emb[:, cols]


def _tail(x, mm, mh, ms, p, w_out, g2, w_up, w_down, g3, w_gate, w_proj, g4, layer):
    n = x.shape[0]
    tm = min(TAIL_TM, n)
    tiles = n // tm
    row = lambda i: (i, 0)
    lay = lambda i: (layer, 0, 0)
    once = pl.Buffered(1)

    def wspec(shape):
        return pl.BlockSpec((None,) + shape, lay, pipeline_mode=once)

    return pl.pallas_call(
        _tail_kernel,
        grid=(tiles,),
        in_specs=[pl.BlockSpec((tm, D_MODEL), row),
                  pl.BlockSpec((tm, GROUP_W), row),
                  pl.BlockSpec((tm, GROUP_W), row),
                  pl.BlockSpec((tm, 2 * GROUP_W), row),
                  pl.BlockSpec((None, tm, PLE_DIM), lambda i: (layer, i, 0)),
                  wspec((D_MODEL, D_MODEL)),
                  wspec((1, D_MODEL)),
                  wspec((D_MODEL, D_FF)),
                  wspec((D_FF, D_MODEL)),
                  wspec((1, D_MODEL)),
                  wspec((D_MODEL, D_MODEL)),
                  wspec((PLE_DIM, D_MODEL)),
                  wspec((1, D_MODEL))],
        out_specs=pl.BlockSpec((tm, D_MODEL), row),
        out_shape=jax.ShapeDtypeStruct((n, D_MODEL), F32),
        compiler_params=_params(1),
        name="tail",
    )(x, mm, mh, ms, p, w_out, g2, w_up, w_down, g3, w_gate, w_proj, g4)


def kernel(x, p, positions, in_norm_g, w_in, b_in, mlstm_f_bias, mlstm_conv_w, mlstm_conv_b,
           mlstm_norm_g, hgrn_lb_logits, hgrn_norm_g, swa_q_norm_g, swa_k_norm_g, swa_sinks,
           w_out, mlp_norm_g, w_up, w_down, ple_norm_g, w_ple_gate, w_ple_proj, ple_post_norm_g):
    batch, seq, d_model = x.shape
    depth = w_in.shape[0]
    n = batch * seq
    assert d_model == D_MODEL and seq % max(MIX_ROWS, IN_TM) == 0

    n_raw = GATE_OFF + GATE_COLS
    pad = LANES - GATE_COLS
    w_in_p = jnp.concatenate(
        [w_in[..., :n_raw], jnp.zeros((depth, D_MODEL, pad), w_in.dtype), w_in[..., n_raw:]],
        axis=-1).astype(BF16)
    b_in_p = jnp.concatenate(
        [b_in[..., :n_raw], jnp.zeros((depth, pad), b_in.dtype), b_in[..., n_raw:]],
        axis=-1).astype(F32).reshape(depth, 1, IN_COLS)
    f_bias = jnp.zeros((depth, 1, LANES), F32).at[:, 0, N_HEADS:GATE_COLS].set(mlstm_f_bias.astype(F32))
    row3 = lambda a: a.astype(F32).reshape(depth, 1, a.shape[-1])
    q_gain = row3(jnp.tile(swa_q_norm_g, (1, LANES // HEAD_DIM)))
    k_gain = row3(jnp.tile(swa_k_norm_g, (1, LANES // HEAD_DIM)))
    w_out_b, w_up_b, w_down_b = w_out.astype(BF16), w_up.astype(BF16), w_down.astype(BF16)
    w_gate_b, w_proj_b = w_ple_gate.astype(BF16), w_ple_proj.astype(BF16)

    blk256 = _blk_ones(GROUP_W)
    tril = jnp.asarray(np.tril(np.ones((MLSTM_T, MLSTM_T), np.float32)), dtype=BF16)
    expand = jnp.asarray(
        (np.arange(LANES)[:, None] == (np.arange(GROUP_W) // HEAD_DIM)[None, :]).astype(np.float32),
        dtype=BF16)
    hgrn_tril, hgrn_pmask = _hgrn_constants()

    cos_t, sin_t = _rope_tables(positions)
    la, l1m, oml = [row3(a) for a in _hgrn_lower_bounds(hgrn_lb_logits)]
    worst_log2 = (HGRN_L // 2) * LOG2_E * jnp.min(la, axis=(1, 2))
    hgrn_bounded = worst_log2 > -HGRN_SAFE_LOG2
    sinks = swa_sinks.astype(F32)

    xf = x.reshape(n, D_MODEL)
    for l in range(depth):
        ml, gt, hb, lf, sw = _in_proj(
            xf, row3(in_norm_g), w_in_p, b_in_p, mlstm_conv_w.astype(F32), row3(mlstm_conv_b),
            la, l1m, oml, cos_t, sin_t, q_gain, k_gain, blk256, l, seq)
        mix_consts = (blk256, tril, expand, hgrn_tril, hgrn_pmask)
        mm, mh, ms = lax.cond(
            hgrn_bounded[l],
            lambda *a: _mixers(*a, l, batch, mix_consts, True),
            lambda *a: _mixers(*a, l, batch, mix_consts, False),
            ml, gt, hb, lf, sw, f_bias, row3(mlstm_norm_g), row3(hgrn_norm_g), sinks)
        xf = _tail(xf, mm, mh, ms, p.reshape(depth, n, PLE_DIM), w_out_b, row3(mlp_norm_g), w_up_b,
                   w_down_b, row3(ple_norm_g), w_gate_b, w_proj_b, row3(ple_post_norm_g), l)
    return xf.reshape(batch, seq, D_MODEL)
```

```python
import functools

import numpy as np
import jax
import jax.numpy as jnp
from jax import lax
from jax.experimental import pallas as pl
from jax.experimental.pallas import tpu as pltpu

F32 = jnp.float32
BF16 = jnp.bfloat16

D_MODEL = 1024
N_HEADS = 4
HEAD_DIM = 64
GROUP_W = N_HEADS * HEAD_DIM
CONV_K = 4
SWA_Q_HEADS = 8
SWA_KV_HEADS = 2
SWA_GROUP = SWA_Q_HEADS // SWA_KV_HEADS
WINDOW = 128
ROPE_THETA = 500000.0
ROT_DIM = HEAD_DIM // 4
ROT_HALF = ROT_DIM // 2
D_FF = 4 * D_MODEL
PLE_DIM = 256
EPS = 1e-6
LOG2_E = 1.4426950408889634

LANES = 128
SUBLANES = 8
VMEM_LIMIT_BYTES = 56 * 1024 * 1024

GATE_COLS = 2 * N_HEADS
M_COLS = 4 * GROUP_W + LANES
H_COLS = 4 * GROUP_W
S_COLS = SWA_Q_HEADS * HEAD_DIM + 2 * SWA_KV_HEADS * HEAD_DIM
IN_COLS = M_COLS + H_COLS + S_COLS
GATE_OFF = 4 * GROUP_W

IN_TM = 1024
TAIL_TM = 512
FF_CHUNK = 1024
GATE_CHUNK = 256
ROPE_TM = 2048
MIX_ROWS = 1024
MLSTM_T = 256
HGRN_L = 64
HGRN_SAFE_LOG2 = 100.0
SWA_T = MIX_ROWS

N_SPLIT = 2
assert HGRN_L == HEAD_DIM


def _params(n_grid_dims):
    return pltpu.CompilerParams(
        dimension_semantics=("arbitrary",) * n_grid_dims,
        vmem_limit_bytes=VMEM_LIMIT_BYTES)


def _dot(a, b):
    return jnp.dot(a, b, preferred_element_type=F32)


def _dot_nt(a, b):
    return lax.dot_general(a, b, (((1,), (1,)), ((), ())), preferred_element_type=F32)


def _dot_tn(a, b):
    return lax.dot_general(a, b, (((0,), (0,)), ((), ())), preferred_element_type=F32)


def _split(x, n=N_SPLIT):
    pieces = []
    r = x
    for _ in range(n):
        p = r.astype(BF16)
        pieces.append(p)
        r = r - p.astype(F32)
    return pieces


def _dot_const_left(m, x):
    acc = None
    for p in _split(x):
        t = _dot(m, p)
        acc = t if acc is None else acc + t
    return acc


def _dot_const_right(x, m):
    acc = None
    for p in _split(x):
        t = _dot(p, m)
        acc = t if acc is None else acc + t
    return acc


def _sigmoid(x):
    return 1.0 / (1.0 + jnp.exp(-x))


def _log_sigmoid(x):
    return jnp.minimum(x, 0.0) - jnp.log1p(jnp.exp(-jnp.abs(x)))


def _rms(x, g):
    return x * lax.rsqrt(jnp.mean(x * x, axis=-1, keepdims=True) + EPS) * g


def _head_rms(x, g, blk_ones):
    ms = _dot(jnp.square(x).astype(BF16), blk_ones) * (1.0 / HEAD_DIM)
    return x * lax.rsqrt(ms + EPS) * g


def _lane_head_id(shape):
    return lax.broadcasted_iota(jnp.int32, shape, len(shape) - 1) // HEAD_DIM


def _blk_ones(width):
    hid = np.arange(width) // HEAD_DIM
    return jnp.asarray((hid[:, None] == hid[None, :]).astype(np.float32), dtype=BF16)


def _hgrn_level_sizes():
    sizes = []
    c = 2
    while c <= HGRN_L:
        sizes.append(c)
        c *= 2
    return sizes


def _hgrn_constants():
    L = HGRN_L
    t = np.arange(L)[:, None]
    u = np.arange(L)[None, :]
    masks = [(t == u)]
    for c in _hgrn_level_sizes():
        mid = (t // c) * c + c // 2 - 1
        same_blk = (t // c) == (u // c)
        u_mid = (u // c) * c + c // 2 - 1
        masks.append(same_blk & (t > mid) & (u <= u_mid))
    masks.append(u <= t)
    tril = (u <= t).astype(np.float32)
    pmask = np.stack([np.tile(m.astype(np.float32), (1, N_HEADS)) for m in masks])
    return jnp.asarray(tril, dtype=BF16), jnp.asarray(pmask, dtype=F32)


ROPE_PER_ROW = LANES // ROT_HALF


def _rope_kernel(pos_ref, freq_ref, sel_ref, one_ref, cos_ref, sin_ref):
    ang = pos_ref[...].astype(F32) * freq_ref[0:1, :]
    cos_ref[...] = _dot_const_right(jnp.cos(ang), sel_ref[0]) + one_ref[0:1, :]
    sin_ref[...] = _dot_const_right(jnp.sin(ang), sel_ref[1])


def _rope_tables(positions):
    n = positions.size
    per_row = ROPE_PER_ROW
    rows = n // per_row
    wide = per_row * LANES
    inv_freq = ROPE_THETA ** (-jnp.arange(0, ROT_DIM, 2, dtype=F32) / ROT_DIM)
    freq = jnp.zeros((SUBLANES, LANES), F32).at[0].set(jnp.tile(inv_freq, per_row))
    pos = jnp.repeat(positions.reshape(rows, per_row), ROT_HALF, axis=1)
    col = np.arange(wide)
    tok, lane = col // LANES, (col % LANES) % HEAD_DIM
    rotated = lane < ROT_DIM
    pick = (np.arange(LANES)[:, None] == (tok * ROT_HALF + lane % ROT_HALF)[None, :]) & rotated[None, :]
    sign = np.where(lane < ROT_HALF, -1.0, 1.0)[None, :]
    sel = jnp.asarray(np.stack([pick.astype(np.float32), pick * sign]), dtype=BF16)
    unrotated = jnp.asarray(np.broadcast_to((~rotated).astype(np.float32)[None, :], (SUBLANES, wide)))
    tm = min(ROPE_TM // per_row, rows)
    out = jax.ShapeDtypeStruct((rows, wide), F32)
    cos_w, sin_w = pl.pallas_call(
        _rope_kernel,
        grid=(rows // tm,),
        in_specs=[pl.BlockSpec((tm, LANES), lambda i: (i, 0)),
                  pl.BlockSpec((SUBLANES, LANES), lambda i: (0, 0)),
                  pl.BlockSpec(sel.shape, lambda i: (0, 0, 0)),
                  pl.BlockSpec((SUBLANES, wide), lambda i: (0, 0))],
        out_specs=[pl.BlockSpec((tm, wide), lambda i: (i, 0))] * 2,
        out_shape=[out, out],
        compiler_params=_params(1),
        name="rope_tables",
    )(pos, freq, sel, unrotated)
    return cos_w.reshape(n, LANES), sin_w.reshape(n, LANES)


def _lb_kernel(lg_ref, la_ref, l1m_ref, oml_ref):
    z = lg_ref[...]
    e = jnp.exp(z - jnp.max(z, axis=0, keepdims=True))
    sm = e / jnp.sum(e, axis=0, keepdims=True)
    depth = z.shape[0]
    run = sm[0:1, :]
    first = run
    for l in range(depth):
        if l > 0:
            run = run + sm[l:l + 1, :]
        lb = run - first
        la_ref[l:l + 1, :] = jnp.log(lb)
        l1m_ref[l:l + 1, :] = jnp.log1p(-lb)
        oml_ref[l:l + 1, :] = 1.0 - lb


def _hgrn_lower_bounds(logits):
    out = jax.ShapeDtypeStruct(logits.shape, F32)
    return pl.pallas_call(
        _lb_kernel, out_shape=[out, out, out], name="hgrn_lower_bounds",
    )(logits.astype(F32))


def _in_proj_kernel(x_ref, g_ref, w_ref, b_ref, cw_ref, cb_ref, la_ref, l1m_ref, oml_ref,
                    cos_ref, sin_ref, qg_ref, kg_ref, ones_ref,
                    ml_ref, gt_ref, hb_ref, lf_ref, sw_ref, xbuf, *, tiles_per_seq):
    TM = x_ref.shape[0]
    W = GROUP_W

    @pl.when(pl.program_id(0) % tiles_per_seq == 0)
    def _():
        xbuf[0:SUBLANES, :] = jnp.zeros((SUBLANES, 2 * W), F32)

    h = _rms(x_ref[...], g_ref[...]).astype(BF16)

    def proj(lo, width):
        return _dot(h, w_ref[:, lo:lo + width]) + b_ref[:, lo:lo + width]

    cos_t = cos_ref[...]
    sin_t = sin_ref[...]
    blk = ones_ref[...]
    lane = lax.broadcasted_iota(jnp.int32, (TM, LANES), 1)
    low_half = lane < HEAD_DIM
    rot_first = (lane % HEAD_DIM) < ROT_HALF
    s0 = M_COLS + H_COLS
    QW = SWA_Q_HEADS * HEAD_DIM

    def head_sumsq(y):
        width = y.shape[-1]
        return _dot(jnp.square(y).astype(BF16), blk[0:width, 0:width])

    def norm_rope(v, sumsq, gain):
        xn = v * lax.rsqrt(sumsq * (1.0 / HEAD_DIM) + EPS) * gain
        partner = jnp.where(rot_first, pltpu.roll(xn, LANES - ROT_HALF, 1), pltpu.roll(xn, ROT_HALF, 1))
        return xn * cos_t + partner * sin_t

    def swa_q(c, y, sumsq):
        for half in range(2):
            sl = slice(half * LANES, (half + 1) * LANES)
            sw_ref[:, (c + half) * LANES:(c + half + 1) * LANES] = (
                norm_rope(y[:, sl], sumsq[:, sl], qg_ref[...]) * (HEAD_DIM ** -0.5 * LOG2_E)).astype(BF16)

    def swa_kv(y, sumsq):
        k = norm_rope(y[:, 0:LANES], sumsq, kg_ref[...])
        v = y[:, LANES:2 * LANES]
        k_sw = pltpu.roll(k, HEAD_DIM, 1)
        v_sw = pltpu.roll(v, HEAD_DIM, 1)
        sw_ref[:, QW:QW + LANES] = jnp.where(low_half, k, k_sw).astype(BF16)
        sw_ref[:, QW + LANES:QW + 2 * LANES] = jnp.where(low_half, k_sw, k).astype(BF16)
        sw_ref[:, QW + 2 * LANES:QW + 3 * LANES] = jnp.where(low_half, v, v_sw).astype(BF16)
        sw_ref[:, QW + 3 * LANES:QW + 4 * LANES] = jnp.where(low_half, v_sw, v).astype(BF16)

    def mlstm_qk(y):
        xbuf[SUBLANES:SUBLANES + TM, :] = y
        acc = cb_ref[...] + cw_ref[CONV_K - 1:CONV_K, :] * xbuf[SUBLANES:SUBLANES + TM, :]
        for j in range(1, CONV_K):
            acc = acc + cw_ref[CONV_K - 1 - j:CONV_K - j, :] * xbuf[SUBLANES - j:SUBLANES - j + TM, :]
        xbuf[0:SUBLANES, :] = xbuf[TM:TM + SUBLANES, :]
        qk = acc * _sigmoid(acc)
        ml_ref[:, 0:W] = qk[:, 0:W].astype(BF16)
        ml_ref[:, W:2 * W] = (qk[:, W:2 * W] * (HEAD_DIM ** -0.5)).astype(BF16)

    def hgrn_qf(y):
        hq = y[:, 0:W]
        hf = y[:, W:2 * W]
        la = la_ref[...]
        u = jnp.exp(-jnp.abs(hf))
        t = l1m_ref[...] + (jnp.minimum(hf, 0.0) - jnp.log1p(u))
        lf_ref[...] = jnp.maximum(la, t) + jnp.log1p(jnp.exp(-jnp.abs(la - t)))
        hb_ref[:, 0:W] = (hq * _sigmoid(hq)).astype(BF16)
        hb_ref[:, W:2 * W] = (oml_ref[...] * jnp.where(hf >= 0.0, u, 1.0) / (1.0 + u)).astype(BF16)

    y_h = proj(M_COLS, 2 * W)
    y_m = proj(0, 2 * W)
    hgrn_qf(y_h)
    y_kv = proj(s0 + QW, 2 * LANES)
    mlstm_qk(y_m)
    y_q0 = proj(s0, 2 * LANES)
    swa_kv(y_kv, head_sumsq(y_kv[:, 0:LANES]))
    y_q1 = proj(s0 + 2 * LANES, 2 * LANES)
    swa_q(0, y_q0, head_sumsq(y_q0))
    y_h2 = proj(M_COLS + 2 * W, 2 * W)
    swa_q(2, y_q1, head_sumsq(y_q1))
    y_m2 = proj(2 * W, 2 * W)
    hg = y_h2[:, W:2 * W]
    hb_ref[:, 2 * W:3 * W] = y_h2[:, 0:W].astype(BF16)
    hb_ref[:, 3 * W:4 * W] = (hg * _sigmoid(hg)).astype(BF16)
    gt_ref[...] = proj(GATE_OFF, LANES)
    ml_ref[:, 2 * W:3 * W] = y_m2[:, 0:W].astype(BF16)
    ml_ref[:, 3 * W:4 * W] = _sigmoid(y_m2[:, W:2 * W]).astype(BF16)


def _in_proj(x, g, w, b, conv_w, conv_b, la, l1m, oml, cos_t, sin_t, q_gain, k_gain, blk_ones,
             layer, seq):
    n = x.shape[0]
    tm = min(IN_TM, n)
    row = lambda i: (i, 0)
    lay = lambda i: (layer, 0, 0)
    W4 = 4 * GROUP_W
    return pl.pallas_call(
        functools.partial(_in_proj_kernel, tiles_per_seq=seq // tm),
        grid=(n // tm,),
        in_specs=[pl.BlockSpec((tm, D_MODEL), row),
                  pl.BlockSpec((None, 1, D_MODEL), lay),
                  pl.BlockSpec((None, D_MODEL, IN_COLS), lay),
                  pl.BlockSpec((None, 1, IN_COLS), lay),
                  pl.BlockSpec((None, CONV_K, 2 * GROUP_W), lay),
                  pl.BlockSpec((None, 1, 2 * GROUP_W), lay),
                  pl.BlockSpec((None, 1, GROUP_W), lay),
                  pl.BlockSpec((None, 1, GROUP_W), lay),
                  pl.BlockSpec((None, 1, GROUP_W), lay),
                  pl.BlockSpec((tm, LANES), row),
                  pl.BlockSpec((tm, LANES), row),
                  pl.BlockSpec((None, 1, LANES), lay),
                  pl.BlockSpec((None, 1, LANES), lay),
                  pl.BlockSpec((GROUP_W, GROUP_W), lambda i: (0, 0))],
        out_specs=[pl.BlockSpec((tm, W4), row),
                   pl.BlockSpec((tm, LANES), row),
                   pl.BlockSpec((tm, W4), row),
                   pl.BlockSpec((tm, GROUP_W), row),
                   pl.BlockSpec((tm, W4), row)],
        out_shape=[jax.ShapeDtypeStruct((n, W4), BF16),
                   jax.ShapeDtypeStruct((n, LANES), F32),
                   jax.ShapeDtypeStruct((n, W4), BF16),
                   jax.ShapeDtypeStruct((n, GROUP_W), F32),
                   jax.ShapeDtypeStruct((n, W4), BF16)],
        scratch_shapes=[pltpu.VMEM((tm + SUBLANES, 2 * GROUP_W), F32)],
        compiler_params=_params(1),
        name="in_proj",
    )(x, g, w, b, conv_w, conv_b, la, l1m, oml, cos_t, sin_t, q_gain, k_gain, blk_ones)


def _mlstm_chunk(r0, y_ref, gt_ref, fb_ref, g_ref, ones_ref, tril_ref, exp_ref, o_ref,
                 c_st, n_st, m_st):
    T = MLSTM_T
    W = GROUP_W
    qb = y_ref[r0:r0 + T, 0:W]
    kb = y_ref[r0:r0 + T, W:2 * W]
    vb = y_ref[r0:r0 + T, 2 * W:3 * W]
    out_gate = y_ref[r0:r0 + T, 3 * W:4 * W]

    gates = gt_ref[r0:r0 + T, :]
    i_pre = gates
    lf = pltpu.roll(_log_sigmoid(gates + fb_ref[...]), LANES - N_HEADS, 1)
    lf_hi = lf.astype(BF16)
    lf_lo = (lf - lf_hi.astype(F32)).astype(BF16)
    bb = _dot(tril_ref[...], jnp.concatenate([lf_hi, lf_lo], axis=1))
    b = bb[:, 0:LANES] + bb[:, LANES:2 * LANES]
    a = i_pre - b
    row = lax.broadcasted_iota(jnp.int32, (T, LANES), 0)
    cm = a
    sh = 1
    while sh < T:
        cm = jnp.maximum(cm, jnp.where(row >= sh, pltpu.roll(cm, sh, 0), -jnp.inf))
        sh *= 2
    m_prev = m_st[0:1, :]
    m_j = b + jnp.maximum(m_prev, cm)
    m_new = m_j[T - 1:T, :]
    b_last = b[T - 1:T, :]
    w_inter = jnp.exp(b + m_prev - m_j)
    e_den = jnp.exp(-m_j)
    w_s = jnp.exp(b_last + a - m_new)
    decay = jnp.exp(b_last + m_prev - m_new)
    cexp = (b - m_j) * LOG2_E
    a_t = (a * LOG2_E).T

    stacked = jnp.concatenate([w_inter, e_den, w_s], axis=0)
    head_lane = lax.broadcasted_iota(jnp.int32, stacked.shape, 1) < N_HEADS
    ex = _dot(jnp.where(head_lane, stacked, 0.0).astype(BF16), exp_ref[...])
    w_inter_x = ex[0:T]
    e_den_x = ex[T:2 * T]
    w_s_x = ex[2 * T:3 * T]
    decay8 = jnp.where(head_lane[0:SUBLANES], jnp.broadcast_to(decay, (SUBLANES, LANES)), 0.0)
    decay_x = _dot_const_right(decay8, exp_ref[...])[0:1]

    blk = ones_ref[...]
    hid = _lane_head_id((T, W))
    causal = (lax.broadcasted_iota(jnp.int32, (T, T), 1)
              <= lax.broadcasted_iota(jnp.int32, (T, T), 0))
    head_rows = [blk[h * HEAD_DIM:h * HEAD_DIM + 1, :] for h in range(N_HEADS)]
    s_all = _dot_nt(jnp.concatenate([qb * hr for hr in head_rows], axis=0), kb)
    num = None
    den = None
    for h in range(N_HEADS):
        dm = cexp[:, h:h + 1] + a_t[h:h + 1, :]
        s = s_all[h * T:(h + 1) * T] * jnp.exp2(jnp.where(causal, dm, -jnp.inf))
        dn = _dot(s.astype(BF16), vb * head_rows[h])
        dd = jnp.where(hid == h, jnp.sum(s, axis=-1, keepdims=True), 0.0)
        num = dn if num is None else num + dn
        den = dd if den is None else den + dd
    n_row = n_st[0:1, :].astype(BF16)
    num = num + w_inter_x * _dot(qb, c_st[...].astype(BF16))
    den = den + w_inter_x * _dot(qb * n_row, blk)
    hh = num / jnp.maximum(jnp.abs(den), e_den_x)

    o_ref[r0:r0 + T, :] = _head_rms(hh, g_ref[...], blk) * out_gate.astype(F32)

    kw = kb * w_s_x.astype(BF16)
    c_st[...] = decay_x * c_st[...] + blk.astype(F32) * _dot_tn(kw, vb)
    n_st[...] = decay_x * n_st[...] + _dot(jnp.ones((SUBLANES, T), BF16), kw)
    m_st[...] = jnp.broadcast_to(m_new, m_st.shape)


def _hgrn_chunk(r0, y_ref, lf_ref, g_ref, ones_ref, mst_ref, pm_ref, o_ref, s_st, bounded):
    L = HGRN_L
    W = GROUP_W
    gain = g_ref[...]
    blk = ones_ref[...]
    blk_f = blk.astype(F32)
    tril = mst_ref[...]
    row = lax.broadcasted_iota(jnp.int32, (L, W), 0)

    def level_gap(g, lf2, c):
        if c == 2:
            return jnp.where(row % 2 == 1, lf2, 0.0)
        if c >= SUBLANES:
            mids = [b * c + c // 2 - 1 for b in range(L // c)]
            return g - jnp.concatenate(
                [jnp.broadcast_to(g[m:m + 1, :], (c, W)) for m in mids], axis=0)
        lo = jnp.concatenate([jnp.broadcast_to(g[m:m + 1, :], (SUBLANES, W))
                              for m in range(1, L, SUBLANES)], axis=0)
        hi = jnp.concatenate([jnp.broadcast_to(g[m:m + 1, :], (SUBLANES, W))
                              for m in range(5, L, SUBLANES)], axis=0)
        return g - jnp.where(row % SUBLANES < 4, lo, hi)

    def head_stack(xb):
        return jnp.concatenate([xb] * N_HEADS, axis=0) * blk

    qf = y_ref[r0:r0 + L, 0:W]
    key = y_ref[r0:r0 + L, W:2 * W]
    vb = y_ref[r0:r0 + L, 2 * W:3 * W]
    out_gate = y_ref[r0:r0 + L, 3 * W:4 * W]

    lf2 = lf_ref[r0:r0 + L, :] * LOG2_E
    g = _dot_const_left(tril, lf2)
    e_last = jnp.exp2(g[L - 1:L, :])
    q_hat = qf * jnp.exp2(g).astype(BF16)
    k_hat = key * jnp.exp2(g[L - 1:L, :] - g).astype(BF16)

    if bounded:
        g_mid = g[L // 2 - 1:L // 2, :]
        a = pm_ref[pm_ref.shape[0] - 1] * _dot_nt(
            qf * jnp.exp2(g - g_mid).astype(BF16), head_stack(key * jnp.exp2(g_mid - g).astype(BF16)))
    else:
        a = pm_ref[0] * _dot_nt(qf, head_stack(key))
        for i, c in enumerate(_hgrn_level_sizes()):
            e_i = jnp.exp2(-jnp.abs(level_gap(g, lf2, c))).astype(BF16)
            a = a + pm_ref[1 + i] * _dot_nt(qf * e_i, head_stack(key * e_i))
    o = _dot(a.astype(BF16), head_stack(vb)) + _dot_nt(q_hat, s_st[...].astype(BF16))

    o_ref[r0:r0 + L, :] = _head_rms(o, gain, blk) * out_gate.astype(F32)
    s_st[...] = s_st[...] * e_last + blk_f * _dot_tn(vb, k_hat)


def _swa_blocks(sink_ref, y_ref, o_ref, k_prev, v_prev, layer):
    T = SWA_T
    Wn = WINDOW
    QW = SWA_Q_HEADS * HEAD_DIM
    step = pl.program_id(1)

    k2 = [jnp.concatenate([k_prev[g], y_ref[:, QW + g * LANES:QW + (g + 1) * LANES]], axis=0)
          for g in range(SWA_KV_HEADS)]
    v2 = [jnp.concatenate([v_prev[g], y_ref[:, QW + (2 + g) * LANES:QW + (3 + g) * LANES]], axis=0)
          for g in range(SWA_KV_HEADS)]
    for g in range(SWA_KV_HEADS):
        k_prev[g] = k2[g][T:T + Wn]
        v_prev[g] = v2[g][T:T + Wn]

    qi = lax.broadcasted_iota(jnp.int32, (Wn, 2 * Wn), 0)
    ki = lax.broadcasted_iota(jnp.int32, (Wn, 2 * Wn), 1)
    band = (ki > qi) & (ki <= qi + Wn)
    band_first = band & (ki >= jnp.where(step > 0, 0, Wn))
    lane_w = lax.broadcasted_iota(jnp.int32, (Wn, LANES), 1) < HEAD_DIM
    lane_row = lax.broadcasted_iota(jnp.int32, (1, LANES), 1) < HEAD_DIM
    low_row = jnp.where(lane_row, 1.0, 0.0).astype(BF16)
    high_row = jnp.where(lane_row, 0.0, 1.0).astype(BF16)

    def scores(j, g):
        r0 = j * Wn
        kc = k2[g][r0:r0 + 2 * Wn]
        parts = []
        for c in (2 * g, 2 * g + 1):
            qc = y_ref[r0:r0 + Wn, c * LANES:(c + 1) * LANES]
            parts.append(qc * low_row)
            parts.append(qc * high_row)
        return _dot_nt(jnp.concatenate(parts, axis=0), kc)

    def finish(j, g, s):
        r0 = j * Wn
        mask = band_first if j == 0 else band
        vc = v2[g][r0:r0 + 2 * Wn]
        outs = []
        for r in range(SWA_GROUP):
            sink = sink_ref[layer, SWA_GROUP * g + r] * LOG2_E
            sr = jnp.where(mask, s[r * Wn:(r + 1) * Wn], -jnp.inf)
            m = jnp.maximum(jnp.max(sr, axis=-1, keepdims=True), sink)
            pexp = jnp.exp2(sr - m)
            den = jnp.sum(pexp, axis=-1, keepdims=True) + jnp.exp2(sink - m)
            outs.append(_dot(pexp.astype(BF16), vc) / den)
        c0 = 2 * g * LANES
        o_ref[r0:r0 + Wn, c0:c0 + LANES] = jnp.where(lane_w, outs[0], outs[1])
        o_ref[r0:r0 + Wn, c0 + LANES:c0 + 2 * LANES] = jnp.where(lane_w, outs[2], outs[3])

    return [(functools.partial(scores, j, g), functools.partial(finish, j, g))
            for j in range(T // Wn) for g in range(SWA_KV_HEADS)]


def _mixers_kernel(sink_ref, ml_ref, gt_ref, fb_ref, gm_ref, hb_ref, lf_ref, gh_ref, sw_ref,
                   ones_ref, tril_ref, exp_ref, tril64_ref, pm_ref,
                   om_ref, oh_ref, os_ref,
                   c_st, n_st, m_st, s_st, k_prev, v_prev, *, layer, bounded):
    @pl.when(pl.program_id(1) == 0)
    def _():
        for ref in (c_st, n_st, m_st, s_st, k_prev, v_prev):
            ref[...] = jnp.zeros_like(ref)

    swa_blocks = _swa_blocks(sink_ref, sw_ref, os_ref, k_prev, v_prev, layer)
    n_sub = MIX_ROWS // HGRN_L
    assert len(swa_blocks) == n_sub and n_sub % (MIX_ROWS // MLSTM_T) == 0
    per_mlstm = n_sub // (MIX_ROWS // MLSTM_T)
    for sub in range(n_sub):
        swa_scores, swa_finish = swa_blocks[sub]
        s = swa_scores()
        _hgrn_chunk(sub * HGRN_L, hb_ref, lf_ref, gh_ref, ones_ref, tril64_ref, pm_ref, oh_ref, s_st,
                    bounded)
        swa_finish(s)
        if sub % per_mlstm == 0:
            _mlstm_chunk((sub // per_mlstm) * MLSTM_T, ml_ref, gt_ref, fb_ref, gm_ref, ones_ref,
                         tril_ref, exp_ref, om_ref, c_st, n_st, m_st)


def _mixers(ml, gt, hb, lf, sw, f_bias, mlstm_g, hgrn_g, sinks, layer, batch, consts, bounded):
    n = ml.shape[0]
    seq = n // batch
    T = MIX_ROWS
    steps = seq // T
    W4 = 4 * GROUP_W
    QW = SWA_Q_HEADS * HEAD_DIM
    row = lambda bi, ci: (bi * steps + ci, 0)
    lay = lambda bi, ci: (layer, 0, 0)
    cst = lambda bi, ci: (0, 0)
    blk_ones, tril, expand, tril64, pmask = consts
    return pl.pallas_call(
        functools.partial(_mixers_kernel, layer=layer, bounded=bounded),
        grid=(batch, steps),
        in_specs=[pl.BlockSpec(memory_space=pltpu.SMEM),
                  pl.BlockSpec((T, W4), row),
                  pl.BlockSpec((T, LANES), row),
                  pl.BlockSpec((None, 1, LANES), lay),
                  pl.BlockSpec((None, 1, GROUP_W), lay),
                  pl.BlockSpec((T, W4), row),
                  pl.BlockSpec((T, GROUP_W), row),
                  pl.BlockSpec((None, 1, GROUP_W), lay),
                  pl.BlockSpec((T, W4), row),
                  pl.BlockSpec((GROUP_W, GROUP_W), cst),
                  pl.BlockSpec((MLSTM_T, MLSTM_T), cst),
                  pl.BlockSpec((LANES, GROUP_W), cst),
                  pl.BlockSpec((HGRN_L, HGRN_L), cst),
                  pl.BlockSpec(pmask.shape, lambda bi, ci: (0, 0, 0))],
        out_specs=[pl.BlockSpec((T, GROUP_W), row),
                   pl.BlockSpec((T, GROUP_W), row),
                   pl.BlockSpec((T, QW), row)],
        out_shape=[jax.ShapeDtypeStruct((n, GROUP_W), F32),
                   jax.ShapeDtypeStruct((n, GROUP_W), F32),
                   jax.ShapeDtypeStruct((n, QW), F32)],
        scratch_shapes=[pltpu.VMEM((GROUP_W, GROUP_W), F32),
                        pltpu.VMEM((SUBLANES, GROUP_W), F32),
                        pltpu.VMEM((SUBLANES, LANES), F32),
                        pltpu.VMEM((GROUP_W, GROUP_W), F32),
                        pltpu.VMEM((SWA_KV_HEADS, WINDOW, LANES), BF16),
                        pltpu.VMEM((SWA_KV_HEADS, WINDOW, LANES), BF16)],
        compiler_params=_params(2),
        name="mixers",
    )(sinks, ml, gt, f_bias, mlstm_g, hb, lf, hgrn_g, sw, blk_ones, tril, expand, tril64, pmask)


def _tail_kernel(x_ref, mm_ref, mh_ref, ms_ref, p_ref, wo_ref, g2_ref, wu_ref, wd_ref, g3_ref,
                 wg_ref, wp_ref, g4_ref, o_ref):
    W = GROUP_W
    mix = _dot(mm_ref[...].astype(BF16), wo_ref[0:W, :])
    mix = mix + _dot(mh_ref[...].astype(BF16), wo_ref[W:2 * W, :])
    mix = mix + _dot(ms_ref[...].astype(BF16), wo_ref[2 * W:4 * W, :])
    x = x_ref[...] + mix
    emb = _rms(_dot(p_ref[...].astype(BF16), wp_ref[...]), g4_ref[...])

    h2 = _rms(x, g2_ref[...]).astype(BF16)
    mlp = None
    for c in range(D_FF // FF_CHUNK):
        u = _dot(h2, wu_ref[:, c * FF_CHUNK:(c + 1) * FF_CHUNK])
        act = jnp.square(jnp.maximum(u, 0.0)).astype(BF16)
        d = _dot(act, wd_ref[c * FF_CHUNK:(c + 1) * FF_CHUNK, :])
        mlp = d if mlp is None else mlp + d
    x = x + mlp

    h3 = _rms(x, g3_ref[...]).astype(BF16)
    for c in range(D_MODEL // GATE_CHUNK):
        cols = slice(c * GATE_CHUNK, (c + 1) * GATE_CHUNK)
        gate = _sigmoid(_dot(h3, wg_ref[:, cols]))
        o_ref[:, cols] = x[:, cols] + gate * emb[:, cols]


def _tail(x, mm, mh, ms, p, w_out, g2, w_up, w_down, g3, w_gate, w_proj, g4, layer):
    n = x.shape[0]
    tm = min(TAIL_TM, n)
    tiles = n // tm
    row = lambda i: (i, 0)
    lay = lambda i: (layer, 0, 0)
    once = pl.Buffered(1)

    def wspec(shape):
        return pl.BlockSpec((None,) + shape, lay, pipeline_mode=once)

    return pl.pallas_call(
        _tail_kernel,
        grid=(tiles,),
        in_specs=[pl.BlockSpec((tm, D_MODEL), row),
                  pl.BlockSpec((tm, GROUP_W), row),
                  pl.BlockSpec((tm, GROUP_W), row),
                  pl.BlockSpec((tm, 2 * GROUP_W), row),
                  pl.BlockSpec((None, tm, PLE_DIM), lambda i: (layer, i, 0)),
                  wspec((D_MODEL, D_MODEL)),
                  wspec((1, D_MODEL)),
                  wspec((D_MODEL, D_FF)),
                  wspec((D_FF, D_MODEL)),
                  wspec((1, D_MODEL)),
                  wspec((D_MODEL, D_MODEL)),
                  wspec((PLE_DIM, D_MODEL)),
                  wspec((1, D_MODEL))],
        out_specs=pl.BlockSpec((tm, D_MODEL), row),
        out_shape=jax.ShapeDtypeStruct((n, D_MODEL), F32),
        compiler_params=_params(1),
        name="tail",
    )(x, mm, mh, ms, p, w_out, g2, w_up, w_down, g3, w_gate, w_proj, g4)


def kernel(x, p, positions, in_norm_g, w_in, b_in, mlstm_f_bias, mlstm_conv_w, mlstm_conv_b,
           mlstm_norm_g, hgrn_lb_logits, hgrn_norm_g, swa_q_norm_g, swa_k_norm_g, swa_sinks,
           w_out, mlp_norm_g, w_up, w_down, ple_norm_g, w_ple_gate, w_ple_proj, ple_post_norm_g):
    batch, seq, d_model = x.shape
    depth = w_in.shape[0]
    n = batch * seq
    assert d_model == D_MODEL and seq % max(MIX_ROWS, IN_TM) == 0

    n_raw = GATE_OFF + GATE_COLS
    pad = LANES - GATE_COLS
    w_in_p = jnp.concatenate(
        [w_in[..., :n_raw], jnp.zeros((depth, D_MODEL, pad), w_in.dtype), w_in[..., n_raw:]],
        axis=-1).astype(BF16)
    b_in_p = jnp.concatenate(
        [b_in[..., :n_raw], jnp.zeros((depth, pad), b_in.dtype), b_in[..., n_raw:]],
        axis=-1).astype(F32).reshape(depth, 1, IN_COLS)
    f_bias = jnp.zeros((depth, 1, LANES), F32).at[:, 0, N_HEADS:GATE_COLS].set(mlstm_f_bias.astype(F32))
    row3 = lambda a: a.astype(F32).reshape(depth, 1, a.shape[-1])
    q_gain = row3(jnp.tile(swa_q_norm_g, (1, LANES // HEAD_DIM)))
    k_gain = row3(jnp.tile(swa_k_norm_g, (1, LANES // HEAD_DIM)))
    w_out_b, w_up_b, w_down_b = w_out.astype(BF16), w_up.astype(BF16), w_down.astype(BF16)
    w_gate_b, w_proj_b = w_ple_gate.astype(BF16), w_ple_proj.astype(BF16)

    blk256 = _blk_ones(GROUP_W)
    tril = jnp.asarray(np.tril(np.ones((MLSTM_T, MLSTM_T), np.float32)), dtype=BF16)
    expand = jnp.asarray(
        (np.arange(LANES)[:, None] == (np.arange(GROUP_W) // HEAD_DIM)[None, :]).astype(np.float32),
        dtype=BF16)
    hgrn_tril, hgrn_pmask = _hgrn_constants()

    cos_t, sin_t = _rope_tables(positions)
    la, l1m, oml = [row3(a) for a in _hgrn_lower_bounds(hgrn_lb_logits)]
    worst_log2 = (HGRN_L // 2) * LOG2_E * jnp.min(la, axis=(1, 2))
    hgrn_bounded = worst_log2 > -HGRN_SAFE_LOG2
    sinks = swa_sinks.astype(F32)

    xf = x.reshape(n, D_MODEL)
    for l in range(depth):
        ml, gt, hb, lf, sw = _in_proj(
            xf, row3(in_norm_g), w_in_p, b_in_p, mlstm_conv_w.astype(F32), row3(mlstm_conv_b),
            la, l1m, oml, cos_t, sin_t, q_gain, k_gain, blk256, l, seq)
        mix_consts = (blk256, tril, expand, hgrn_tril, hgrn_pmask)
        mm, mh, ms = lax.cond(
            hgrn_bounded[l],
            lambda *a: _mixers(*a, l, batch, mix_consts, True),
            lambda *a: _mixers(*a, l, batch, mix_consts, False),
            ml, gt, hb, lf, sw, f_bias, row3(mlstm_norm_g), row3(hgrn_norm_g), sinks)
        xf = _tail(xf, mm, mh, ms, p.reshape(depth, n, PLE_DIM), w_out_b, row3(mlp_norm_g), w_up_b,
                   w_down_b, row3(ple_norm_g), w_gate_b, w_proj_b, row3(ple_post_norm_g), l)
    return xf.reshape(batch, seq, D_MODEL)
```

```python
import functools

import numpy as np
import jax
import jax.numpy as jnp
from jax import lax
from jax.experimental import pallas as pl
from jax.experimental.pallas import tpu as pltpu

F32 = jnp.float32
BF16 = jnp.bfloat16

D_MODEL = 1024
N_HEADS = 4
HEAD_DIM = 64
GROUP_W = N_HEADS * HEAD_DIM
CONV_K = 4
SWA_Q_HEADS = 8
SWA_KV_HEADS = 2
SWA_GROUP = SWA_Q_HEADS // SWA_KV_HEADS
WINDOW = 128
ROPE_THETA = 500000.0
ROT_DIM = HEAD_DIM // 4
ROT_HALF = ROT_DIM // 2
D_FF = 4 * D_MODEL
PLE_DIM = 256
EPS = 1e-6
LOG2_E = 1.4426950408889634

LANES = 128
SUBLANES = 8
VMEM_LIMIT_BYTES = 56 * 1024 * 1024

GATE_COLS = 2 * N_HEADS
M_COLS = 4 * GROUP_W + LANES
H_COLS = 4 * GROUP_W
S_COLS = SWA_Q_HEADS * HEAD_DIM + 2 * SWA_KV_HEADS * HEAD_DIM
IN_COLS = M_COLS + H_COLS + S_COLS
GATE_OFF = 4 * GROUP_W

IN_TM = 1024
TAIL_TM = 512
FF_CHUNK = 1024
GATE_CHUNK = 256
ROPE_TM = 2048
MIX_ROWS = 1024
MLSTM_T = 256
HGRN_L = 64
HGRN_SAFE_LOG2 = 100.0
SWA_T = MIX_ROWS

N_SPLIT = 2
assert HGRN_L == HEAD_DIM


def _params(n_grid_dims):
    return pltpu.CompilerParams(
        dimension_semantics=("arbitrary",) * n_grid_dims,
        vmem_limit_bytes=VMEM_LIMIT_BYTES)


def _dot(a, b):
    return jnp.dot(a, b, preferred_element_type=F32)


def _dot_nt(a, b):
    return lax.dot_general(a, b, (((1,), (1,)), ((), ())), preferred_element_type=F32)


def _dot_tn(a, b):
    return lax.dot_general(a, b, (((0,), (0,)), ((), ())), preferred_element_type=F32)


def _split(x, n=N_SPLIT):
    pieces = []
    r = x
    for _ in range(n):
        p = r.astype(BF16)
        pieces.append(p)
        r = r - p.astype(F32)
    return pieces


def _dot_const_left(m, x):
    acc = None
    for p in _split(x):
        t = _dot(m, p)
        acc = t if acc is None else acc + t
    return acc


def _dot_const_right(x, m):
    acc = None
    for p in _split(x):
        t = _dot(p, m)
        acc = t if acc is None else acc + t
    return acc


def _sigmoid(x):
    return 1.0 / (1.0 + jnp.exp(-x))


def _log_sigmoid(x):
    return jnp.minimum(x, 0.0) - jnp.log1p(jnp.exp(-jnp.abs(x)))


def _rms(x, g):
    return x * lax.rsqrt(jnp.mean(x * x, axis=-1, keepdims=True) + EPS) * g


def _head_rms(x, g, blk_ones):
    ms = _dot(jnp.square(x).astype(BF16), blk_ones) * (1.0 / HEAD_DIM)
    return x * lax.rsqrt(ms + EPS) * g


def _lane_head_id(shape):
    return lax.broadcasted_iota(jnp.int32, shape, len(shape) - 1) // HEAD_DIM


def _blk_ones(width):
    hid = np.arange(width) // HEAD_DIM
    return jnp.asarray((hid[:, None] == hid[None, :]).astype(np.float32), dtype=BF16)


def _hgrn_level_sizes():
    sizes = []
    c = 2
    while c <= HGRN_L:
        sizes.append(c)
        c *= 2
    return sizes


def _hgrn_constants():
    L = HGRN_L
    t = np.arange(L)[:, None]
    u = np.arange(L)[None, :]
    masks = [(t == u)]
    for c in _hgrn_level_sizes():
        mid = (t // c) * c + c // 2 - 1
        same_blk = (t // c) == (u // c)
        u_mid = (u // c) * c + c // 2 - 1
        masks.append(same_blk & (t > mid) & (u <= u_mid))
    masks.append(u <= t)
    tril = (u <= t).astype(np.float32)
    pmask = np.stack([np.tile(m.astype(np.float32), (1, N_HEADS)) for m in masks])
    return jnp.asarray(tril, dtype=BF16), jnp.asarray(pmask, dtype=F32)


ROPE_PER_ROW = LANES // ROT_HALF


def _rope_kernel(pos_ref, freq_ref, sel_ref, one_ref, cos_ref, sin_ref):
    ang = pos_ref[...].astype(F32) * freq_ref[0:1, :]
    cos_ref[...] = _dot_const_right(jnp.cos(ang), sel_ref[0]) + one_ref[0:1, :]
    sin_ref[...] = _dot_const_right(jnp.sin(ang), sel_ref[1])


def _rope_tables(positions):
    n = positions.size
    per_row = ROPE_PER_ROW
    rows = n // per_row
    wide = per_row * LANES
    inv_freq = ROPE_THETA ** (-jnp.arange(0, ROT_DIM, 2, dtype=F32) / ROT_DIM)
    freq = jnp.zeros((SUBLANES, LANES), F32).at[0].set(jnp.tile(inv_freq, per_row))
    pos = jnp.repeat(positions.reshape(rows, per_row), ROT_HALF, axis=1)
    col = np.arange(wide)
    tok, lane = col // LANES, (col % LANES) % HEAD_DIM
    rotated = lane < ROT_DIM
    pick = (np.arange(LANES)[:, None] == (tok * ROT_HALF + lane % ROT_HALF)[None, :]) & rotated[None, :]
    sign = np.where(lane < ROT_HALF, -1.0, 1.0)[None, :]
    sel = jnp.asarray(np.stack([pick.astype(np.float32), pick * sign]), dtype=BF16)
    unrotated = jnp.asarray(np.broadcast_to((~rotated).astype(np.float32)[None, :], (SUBLANES, wide)))
    tm = min(ROPE_TM // per_row, rows)
    out = jax.ShapeDtypeStruct((rows, wide), F32)
    cos_w, sin_w = pl.pallas_call(
        _rope_kernel,
        grid=(rows // tm,),
        in_specs=[pl.BlockSpec((tm, LANES), lambda i: (i, 0)),
                  pl.BlockSpec((SUBLANES, LANES), lambda i: (0, 0)),
                  pl.BlockSpec(sel.shape, lambda i: (0, 0, 0)),
                  pl.BlockSpec((SUBLANES, wide), lambda i: (0, 0))],
        out_specs=[pl.BlockSpec((tm, wide), lambda i: (i, 0))] * 2,
        out_shape=[out, out],
        compiler_params=_params(1),
        name="rope_tables",
    )(pos, freq, sel, unrotated)
    return cos_w.reshape(n, LANES), sin_w.reshape(n, LANES)


def _lb_kernel(lg_ref, la_ref, l1m_ref, oml_ref):
    z = lg_ref[...]
    e = jnp.exp(z - jnp.max(z, axis=0, keepdims=True))
    sm = e / jnp.sum(e, axis=0, keepdims=True)
    depth = z.shape[0]
    run = sm[0:1, :]
    first = run
    for l in range(depth):
        if l > 0:
            run = run + sm[l:l + 1, :]
        lb = run - first
        la_ref[l:l + 1, :] = jnp.log(lb)
        l1m_ref[l:l + 1, :] = jnp.log1p(-lb)
        oml_ref[l:l + 1, :] = 1.0 - lb


def _hgrn_lower_bounds(logits):
    out = jax.ShapeDtypeStruct(logits.shape, F32)
    return pl.pallas_call(
        _lb_kernel, out_shape=[out, out, out], name="hgrn_lower_bounds",
    )(logits.astype(F32))


def _in_proj_kernel(x_ref, g_ref, w_ref, b_ref, cw_ref, cb_ref, la_ref, l1m_ref, oml_ref,
                    cos_ref, sin_ref, qg_ref, kg_ref, ones_ref,
                    ml_ref, gt_ref, hb_ref, lf_ref, sw_ref, xbuf, *, tiles_per_seq):
    TM = x_ref.shape[0]
    W = GROUP_W

    @pl.when(pl.program_id(0) % tiles_per_seq == 0)
    def _():
        xbuf[0:SUBLANES, :] = jnp.zeros((SUBLANES, 2 * W), F32)

    h = _rms(x_ref[...], g_ref[...]).astype(BF16)

    def proj(lo, width):
        return _dot(h, w_ref[:, lo:lo + width]) + b_ref[:, lo:lo + width]

    cos_t = cos_ref[...]
    sin_t = sin_ref[...]
    blk = ones_ref[...]
    lane = lax.broadcasted_iota(jnp.int32, (TM, LANES), 1)
    low_half = lane < HEAD_DIM
    rot_first = (lane % HEAD_DIM) < ROT_HALF
    s0 = M_COLS + H_COLS
    QW = SWA_Q_HEADS * HEAD_DIM

    def head_sumsq(y):
        width = y.shape[-1]
        return _dot(jnp.square(y).astype(BF16), blk[0:width, 0:width])

    def norm_rope(v, sumsq, gain):
        xn = v * lax.rsqrt(sumsq * (1.0 / HEAD_DIM) + EPS) * gain
        partner = jnp.where(rot_first, pltpu.roll(xn, LANES - ROT_HALF, 1), pltpu.roll(xn, ROT_HALF, 1))
        return xn * cos_t + partner * sin_t

    def swa_q(c, y, sumsq):
        for half in range(2):
            sl = slice(half * LANES, (half + 1) * LANES)
            sw_ref[:, (c + half) * LANES:(c + half + 1) * LANES] = (
                norm_rope(y[:, sl], sumsq[:, sl], qg_ref[...]) * (HEAD_DIM ** -0.5 * LOG2_E)).astype(BF16)

    def swa_kv(y, sumsq):
        k = norm_rope(y[:, 0:LANES], sumsq, kg_ref[...])
        v = y[:, LANES:2 * LANES]
        k_sw = pltpu.roll(k, HEAD_DIM, 1)
        v_sw = pltpu.roll(v, HEAD_DIM, 1)
        sw_ref[:, QW:QW + LANES] = jnp.where(low_half, k, k_sw).astype(BF16)
        sw_ref[:, QW + LANES:QW + 2 * LANES] = jnp.where(low_half, k_sw, k).astype(BF16)
        sw_ref[:, QW + 2 * LANES:QW + 3 * LANES] = jnp.where(low_half, v, v_sw).astype(BF16)
        sw_ref[:, QW + 3 * LANES:QW + 4 * LANES] = jnp.where(low_half, v_sw, v).astype(BF16)

    def mlstm_qk(y):
        xbuf[SUBLANES:SUBLANES + TM, :] = y
        acc = cb_ref[...] + cw_ref[CONV_K - 1:CONV_K, :] * xbuf[SUBLANES:SUBLANES + TM, :]
        for j in range(1, CONV_K):
            acc = acc + cw_ref[CONV_K - 1 - j:CONV_K - j, :] * xbuf[SUBLANES - j:SUBLANES - j + TM, :]
        xbuf[0:SUBLANES, :] = xbuf[TM:TM + SUBLANES, :]
        qk = acc * _sigmoid(acc)
        ml_ref[:, 0:W] = qk[:, 0:W].astype(BF16)
        ml_ref[:, W:2 * W] = (qk[:, W:2 * W] * (HEAD_DIM ** -0.5)).astype(BF16)

    def hgrn_qf(y):
        hq = y[:, 0:W]
        hf = y[:, W:2 * W]
        la = la_ref[...]
        u = jnp.exp(-jnp.abs(hf))
        t = l1m_ref[...] + (jnp.minimum(hf, 0.0) - jnp.log1p(u))
        lf_ref[...] = jnp.maximum(la, t) + jnp.log1p(jnp.exp(-jnp.abs(la - t)))
        hb_ref[:, 0:W] = (hq * _sigmoid(hq)).astype(BF16)
        hb_ref[:, W:2 * W] = (oml_ref[...] * jnp.where(hf >= 0.0, u, 1.0) / (1.0 + u)).astype(BF16)

    y_h = proj(M_COLS, 2 * W)
    y_m = proj(0, 2 * W)
    hgrn_qf(y_h)
    y_kv = proj(s0 + QW, 2 * LANES)
    mlstm_qk(y_m)
    y_q0 = proj(s0, 2 * LANES)
    swa_kv(y_kv, head_sumsq(y_kv[:, 0:LANES]))
    y_q1 = proj(s0 + 2 * LANES, 2 * LANES)
    swa_q(0, y_q0, head_sumsq(y_q0))
    y_h2 = proj(M_COLS + 2 * W, 2 * W)
    swa_q(2, y_q1, head_sumsq(y_q1))
    y_m2 = proj(2 * W, 2 * W)
    hg = y_h2[:, W:2 * W]
    hb_ref[:, 2 * W:3 * W] = y_h2[:, 0:W].astype(BF16)
    hb_ref[:, 3 * W:4 * W] = (hg * _sigmoid(hg)).astype(BF16)
    gt_ref[...] = proj(GATE_OFF, LANES)
    ml_ref[:, 2 * W:3 * W] = y_m2[:, 0:W].astype(BF16)
    ml_ref[:, 3 * W:4 * W] = _sigmoid(y_m2[:, W:2 * W]).astype(BF16)


def _in_proj(x, g, w, b, conv_w, conv_b, la, l1m, oml, cos_t, sin_t, q_gain, k_gain, blk_ones,
             layer, seq):
    n = x.shape[0]
    tm = min(IN_TM, n)
    row = lambda i: (i, 0)
    lay = lambda i: (layer, 0, 0)
    W4 = 4 * GROUP_W
    return pl.pallas_call(
        functools.partial(_in_proj_kernel, tiles_per_seq=seq // tm),
        grid=(n // tm,),
        in_specs=[pl.BlockSpec((tm, D_MODEL), row),
                  pl.BlockSpec((None, 1, D_MODEL), lay),
                  pl.BlockSpec((None, D_MODEL, IN_COLS), lay),
                  pl.BlockSpec((None, 1, IN_COLS), lay),
                  pl.BlockSpec((None, CONV_K, 2 * GROUP_W), lay),
                  pl.BlockSpec((None, 1, 2 * GROUP_W), lay),
                  pl.BlockSpec((None, 1, GROUP_W), lay),
                  pl.BlockSpec((None, 1, GROUP_W), lay),
                  pl.BlockSpec((None, 1, GROUP_W), lay),
                  pl.BlockSpec((tm, LANES), row),
                  pl.BlockSpec((tm, LANES), row),
                  pl.BlockSpec((None, 1, LANES), lay),
                  pl.BlockSpec((None, 1, LANES), lay),
                  pl.BlockSpec((GROUP_W, GROUP_W), lambda i: (0, 0))],
        out_specs=[pl.BlockSpec((tm, W4), row),
                   pl.BlockSpec((tm, LANES), row),
                   pl.BlockSpec((tm, W4), row),
                   pl.BlockSpec((tm, GROUP_W), row),
                   pl.BlockSpec((tm, W4), row)],
        out_shape=[jax.ShapeDtypeStruct((n, W4), BF16),
                   jax.ShapeDtypeStruct((n, LANES), F32),
                   jax.ShapeDtypeStruct((n, W4), BF16),
                   jax.ShapeDtypeStruct((n, GROUP_W), F32),
                   jax.ShapeDtypeStruct((n, W4), BF16)],
        scratch_shapes=[pltpu.VMEM((tm + SUBLANES, 2 * GROUP_W), F32)],
        compiler_params=_params(1),
        name="in_proj",
    )(x, g, w, b, conv_w, conv_b, la, l1m, oml, cos_t, sin_t, q_gain, k_gain, blk_ones)


def _mlstm_chunk(r0, y_ref, gt_ref, fb_ref, g_ref, ones_ref, onesf_ref, tril_ref, exp_ref, o_ref,
                 c_st, n_st, m_st):
    T = MLSTM_T
    W = GROUP_W
    qb = y_ref[r0:r0 + T, 0:W]
    kb = y_ref[r0:r0 + T, W:2 * W]
    vb = y_ref[r0:r0 + T, 2 * W:3 * W]
    out_gate = y_ref[r0:r0 + T, 3 * W:4 * W]

    gates = gt_ref[r0:r0 + T, :]
    i_pre = gates
    lf = pltpu.roll(_log_sigmoid(gates + fb_ref[...]), LANES - N_HEADS, 1)
    lf_hi = lf.astype(BF16)
    lf_lo = (lf - lf_hi.astype(F32)).astype(BF16)
    bb = _dot(tril_ref[...], jnp.concatenate([lf_hi, lf_lo], axis=1))
    b = bb[:, 0:LANES] + bb[:, LANES:2 * LANES]
    a = i_pre - b
    row = lax.broadcasted_iota(jnp.int32, (T, LANES), 0)
    cm = a
    sh = 1
    while sh < T:
        cm = jnp.maximum(cm, jnp.where(row >= sh, pltpu.roll(cm, sh, 0), -jnp.inf))
        sh *= 2
    m_prev = m_st[0:1, :]
    m_j = b + jnp.maximum(m_prev, cm)
    m_new = m_j[T - 1:T, :]
    b_last = b[T - 1:T, :]
    w_inter = jnp.exp(b + m_prev - m_j)
    e_den = jnp.exp(-m_j)
    w_s = jnp.exp(b_last + a - m_new)
    decay = jnp.exp(b_last + m_prev - m_new)
    cexp = (b - m_j) * LOG2_E
    a_t = (a * LOG2_E).T

    stacked = jnp.concatenate([w_inter, e_den, w_s], axis=0)
    head_lane = lax.broadcasted_iota(jnp.int32, stacked.shape, 1) < N_HEADS
    ex = _dot(jnp.where(head_lane, stacked, 0.0).astype(BF16), exp_ref[...])
    w_inter_x = ex[0:T]
    e_den_x = ex[T:2 * T]
    w_s_x = ex[2 * T:3 * T]
    decay8 = jnp.where(head_lane[0:SUBLANES], jnp.broadcast_to(decay, (SUBLANES, LANES)), 0.0)
    decay_x = _dot_const_right(decay8, exp_ref[...])[0:1]

    blk = ones_ref[...]
    hid = _lane_head_id((T, W))
    causal = (lax.broadcasted_iota(jnp.int32, (T, T), 1)
              <= lax.broadcasted_iota(jnp.int32, (T, T), 0))
    head_rows = [blk[h * HEAD_DIM:h * HEAD_DIM + 1, :] for h in range(N_HEADS)]
    s_all = _dot_nt(jnp.concatenate([qb * hr for hr in head_rows], axis=0), kb)
    num = None
    den = None
    for h in range(N_HEADS):
        dm = cexp[:, h:h + 1] + a_t[h:h + 1, :]
        s = s_all[h * T:(h + 1) * T] * jnp.exp2(jnp.where(causal, dm, -jnp.inf))
        dn = _dot(s.astype(BF16), vb * head_rows[h])
        dd = jnp.where(hid == h, jnp.sum(s, axis=-1, keepdims=True), 0.0)
        num = dn if num is None else num + dn
        den = dd if den is None else den + dd
    n_row = n_st[0:1, :].astype(BF16)
    num = num + w_inter_x * _dot(qb, c_st[...].astype(BF16))
    den = den + w_inter_x * _dot(qb * n_row, blk)
    hh = num / jnp.maximum(jnp.abs(den), e_den_x)

    o_ref[r0:r0 + T, :] = _head_rms(hh, g_ref[...], blk) * out_gate.astype(F32)

    kw = kb * w_s_x.astype(BF16)
    c_st[...] = decay_x * c_st[...] + onesf_ref[...] * _dot_tn(kw, vb)
    n_st[...] = decay_x * n_st[...] + _dot(jnp.ones((SUBLANES, T), BF16), kw)
    m_st[...] = jnp.broadcast_to(m_new, m_st.shape)


def _hgrn_chunk(r0, y_ref, lf_ref, g_ref, ones_ref, onesf_ref, mst_ref, pm_ref, o_ref, s_st, bounded):
    L = HGRN_L
    W = GROUP_W
    gain = g_ref[...]
    blk = ones_ref[...]
    blk_f = onesf_ref[...]
    tril = mst_ref[...]
    row = lax.broadcasted_iota(jnp.int32, (L, W), 0)

    def level_gap(g, lf2, c):
        if c == 2:
            return jnp.where(row % 2 == 1, lf2, 0.0)
        if c >= SUBLANES:
            mids = [b * c + c // 2 - 1 for b in range(L // c)]
            return g - jnp.concatenate(
                [jnp.broadcast_to(g[m:m + 1, :], (c, W)) for m in mids], axis=0)
        lo = jnp.concatenate([jnp.broadcast_to(g[m:m + 1, :], (SUBLANES, W))
                              for m in range(1, L, SUBLANES)], axis=0)
        hi = jnp.concatenate([jnp.broadcast_to(g[m:m + 1, :], (SUBLANES, W))
                              for m in range(5, L, SUBLANES)], axis=0)
        return g - jnp.where(row % SUBLANES < 4, lo, hi)

    def head_stack(xb):
        return jnp.concatenate([xb] * N_HEADS, axis=0) * blk

    qf = y_ref[r0:r0 + L, 0:W]
    key = y_ref[r0:r0 + L, W:2 * W]
    vb = y_ref[r0:r0 + L, 2 * W:3 * W]
    out_gate = y_ref[r0:r0 + L, 3 * W:4 * W]

    lf2 = lf_ref[r0:r0 + L, :] * LOG2_E
    g = _dot_const_left(tril, lf2)
    e_last = jnp.exp2(g[L - 1:L, :])
    q_hat = qf * jnp.exp2(g).astype(BF16)
    k_hat = key * jnp.exp2(g[L - 1:L, :] - g).astype(BF16)

    if bounded:
        g_mid = g[L // 2 - 1:L // 2, :]
        a = pm_ref[pm_ref.shape[0] - 1] * _dot_nt(
            qf * jnp.exp2(g - g_mid).astype(BF16), head_stack(key * jnp.exp2(g_mid - g).astype(BF16)))
    else:
        a = pm_ref[0] * _dot_nt(qf, head_stack(key))
        for i, c in enumerate(_hgrn_level_sizes()):
            e_i = jnp.exp2(-jnp.abs(level_gap(g, lf2, c))).astype(BF16)
            a = a + pm_ref[1 + i] * _dot_nt(qf * e_i, head_stack(key * e_i))
    o = _dot(a.astype(BF16), head_stack(vb)) + _dot_nt(q_hat, s_st[...].astype(BF16))

    o_ref[r0:r0 + L, :] = _head_rms(o, gain, blk) * out_gate.astype(F32)
    s_st[...] = s_st[...] * e_last + blk_f * _dot_tn(vb, k_hat)


def _swa_blocks(sink_ref, y_ref, o_ref, k_prev, v_prev, layer):
    T = SWA_T
    Wn = WINDOW
    QW = SWA_Q_HEADS * HEAD_DIM
    step = pl.program_id(1)

    k2 = [jnp.concatenate([k_prev[g], y_ref[:, QW + g * LANES:QW + (g + 1) * LANES]], axis=0)
          for g in range(SWA_KV_HEADS)]
    v2 = [jnp.concatenate([v_prev[g], y_ref[:, QW + (2 + g) * LANES:QW + (3 + g) * LANES]], axis=0)
          for g in range(SWA_KV_HEADS)]
    for g in range(SWA_KV_HEADS):
        k_prev[g] = k2[g][T:T + Wn]
        v_prev[g] = v2[g][T:T + Wn]

    qi = lax.broadcasted_iota(jnp.int32, (Wn, 2 * Wn), 0)
    ki = lax.broadcasted_iota(jnp.int32, (Wn, 2 * Wn), 1)
    band = (ki > qi) & (ki <= qi + Wn)
    band_first = band & (ki >= jnp.where(step > 0, 0, Wn))
    lane_w = lax.broadcasted_iota(jnp.int32, (Wn, LANES), 1) < HEAD_DIM
    lane_row = lax.broadcasted_iota(jnp.int32, (1, LANES), 1) < HEAD_DIM
    low_row = jnp.where(lane_row, 1.0, 0.0).astype(BF16)
    high_row = jnp.where(lane_row, 0.0, 1.0).astype(BF16)

    def scores(j, g):
        r0 = j * Wn
        kc = k2[g][r0:r0 + 2 * Wn]
        parts = []
        for c in (2 * g, 2 * g + 1):
            qc = y_ref[r0:r0 + Wn, c * LANES:(c + 1) * LANES]
            parts.append(qc * low_row)
            parts.append(qc * high_row)
        return _dot_nt(jnp.concatenate(parts, axis=0), kc)

    def finish(j, g, s):
        r0 = j * Wn
        mask = band_first if j == 0 else band
        vc = v2[g][r0:r0 + 2 * Wn]
        outs = []
        for r in range(SWA_GROUP):
            sink = sink_ref[layer, SWA_GROUP * g + r] * LOG2_E
            sr = jnp.where(mask, s[r * Wn:(r + 1) * Wn], -jnp.inf)
            m = jnp.maximum(jnp.max(sr, axis=-1, keepdims=True), sink)
            pexp = jnp.exp2(sr - m)
            den = jnp.sum(pexp, axis=-1, keepdims=True) + jnp.exp2(sink - m)
            outs.append(_dot(pexp.astype(BF16), vc) / den)
        c0 = 2 * g * LANES
        o_ref[r0:r0 + Wn, c0:c0 + LANES] = jnp.where(lane_w, outs[0], outs[1])
        o_ref[r0:r0 + Wn, c0 + LANES:c0 + 2 * LANES] = jnp.where(lane_w, outs[2], outs[3])

    return [(functools.partial(scores, j, g), functools.partial(finish, j, g))
            for j in range(T // Wn) for g in range(SWA_KV_HEADS)]


def _mixers_kernel(sink_ref, ml_ref, gt_ref, fb_ref, gm_ref, hb_ref, lf_ref, gh_ref, sw_ref,
                   ones_ref, onesf_ref, tril_ref, exp_ref, tril64_ref, pm_ref,
                   om_ref, oh_ref, os_ref,
                   c_st, n_st, m_st, s_st, k_prev, v_prev, *, layer, bounded):
    @pl.when(pl.program_id(1) == 0)
    def _():
        for ref in (c_st, n_st, m_st, s_st, k_prev, v_prev):
            ref[...] = jnp.zeros_like(ref)

    swa_blocks = _swa_blocks(sink_ref, sw_ref, os_ref, k_prev, v_prev, layer)
    n_sub = MIX_ROWS // HGRN_L
    assert len(swa_blocks) == n_sub and n_sub % (MIX_ROWS // MLSTM_T) == 0
    per_mlstm = n_sub // (MIX_ROWS // MLSTM_T)
    for sub in range(n_sub):
        swa_scores, swa_finish = swa_blocks[sub]
        s = swa_scores()
        _hgrn_chunk(sub * HGRN_L, hb_ref, lf_ref, gh_ref, ones_ref, onesf_ref, tril64_ref, pm_ref, oh_ref,
                    s_st, bounded)
        swa_finish(s)
        if sub % per_mlstm == 0:
            _mlstm_chunk((sub // per_mlstm) * MLSTM_T, ml_ref, gt_ref, fb_ref, gm_ref, ones_ref, onesf_ref,
                         tril_ref, exp_ref, om_ref, c_st, n_st, m_st)


def _mixers(ml, gt, hb, lf, sw, f_bias, mlstm_g, hgrn_g, sinks, layer, batch, consts, bounded):
    n = ml.shape[0]
    seq = n // batch
    T = MIX_ROWS
    steps = seq // T
    W4 = 4 * GROUP_W
    QW = SWA_Q_HEADS * HEAD_DIM
    row = lambda bi, ci: (bi * steps + ci, 0)
    lay = lambda bi, ci: (layer, 0, 0)
    cst = lambda bi, ci: (0, 0)
    blk_ones, tril, expand, tril64, pmask = consts
    return pl.pallas_call(
        functools.partial(_mixers_kernel, layer=layer, bounded=bounded),
        grid=(batch, steps),
        in_specs=[pl.BlockSpec(memory_space=pltpu.SMEM),
                  pl.BlockSpec((T, W4), row),
                  pl.BlockSpec((T, LANES), row),
                  pl.BlockSpec((None, 1, LANES), lay),
                  pl.BlockSpec((None, 1, GROUP_W), lay),
                  pl.BlockSpec((T, W4), row),
                  pl.BlockSpec((T, GROUP_W), row),
                  pl.BlockSpec((None, 1, GROUP_W), lay),
                  pl.BlockSpec((T, W4), row),
                  pl.BlockSpec((GROUP_W, GROUP_W), cst),
                  pl.BlockSpec((GROUP_W, GROUP_W), cst),
                  pl.BlockSpec((MLSTM_T, MLSTM_T), cst),
                  pl.BlockSpec((LANES, GROUP_W), cst),
                  pl.BlockSpec((HGRN_L, HGRN_L), cst),
                  pl.BlockSpec(pmask.shape, lambda bi, ci: (0, 0, 0))],
        out_specs=[pl.BlockSpec((T, GROUP_W), row),
                   pl.BlockSpec((T, GROUP_W), row),
                   pl.BlockSpec((T, QW), row)],
        out_shape=[jax.ShapeDtypeStruct((n, GROUP_W), F32),
                   jax.ShapeDtypeStruct((n, GROUP_W), F32),
                   jax.ShapeDtypeStruct((n, QW), F32)],
        scratch_shapes=[pltpu.VMEM((GROUP_W, GROUP_W), F32),
                        pltpu.VMEM((SUBLANES, GROUP_W), F32),
                        pltpu.VMEM((SUBLANES, LANES), F32),
                        pltpu.VMEM((GROUP_W, GROUP_W), F32),
                        pltpu.VMEM((SWA_KV_HEADS, WINDOW, LANES), BF16),
                        pltpu.VMEM((SWA_KV_HEADS, WINDOW, LANES), BF16)],
        compiler_params=_params(2),
        name="mixers",
    )(sinks, ml, gt, f_bias, mlstm_g, hb, lf, hgrn_g, sw, blk_ones, blk_ones.astype(F32), tril, expand,
      tril64, pmask)


def _tail_kernel(x_ref, mm_ref, mh_ref, ms_ref, p_ref, wo_ref, g2_ref, wu_ref, wd_ref, g3_ref,
                 wg_ref, wp_ref, g4_ref, o_ref):
    W = GROUP_W
    mix = _dot(mm_ref[...].astype(BF16), wo_ref[0:W, :])
    mix = mix + _dot(mh_ref[...].astype(BF16), wo_ref[W:2 * W, :])
    mix = mix + _dot(ms_ref[...].astype(BF16), wo_ref[2 * W:4 * W, :])
    x = x_ref[...] + mix
    emb = _rms(_dot(p_ref[...].astype(BF16), wp_ref[...]), g4_ref[...])

    h2 = _rms(x, g2_ref[...]).astype(BF16)
    mlp = None
    for c in range(D_FF // FF_CHUNK):
        u = _dot(h2, wu_ref[:, c * FF_CHUNK:(c + 1) * FF_CHUNK])
        act = jnp.square(jnp.maximum(u, 0.0)).astype(BF16)
        d = _dot(act, wd_ref[c * FF_CHUNK:(c + 1) * FF_CHUNK, :])
        mlp = d if mlp is None else mlp + d
    x = x + mlp

    h3 = _rms(x, g3_ref[...]).astype(BF16)
    for c in range(D_MODEL // GATE_CHUNK):
        cols = slice(c * GATE_CHUNK, (c + 1) * GATE_CHUNK)
        gate = _sigmoid(_dot(h3, wg_ref[:, cols]))
        o_ref[:, cols] = x[:, cols] + gate * emb[:, cols]


def _tail(x, mm, mh, ms, p, w_out, g2, w_up, w_down, g3, w_gate, w_proj, g4, layer):
    n = x.shape[0]
    tm = min(TAIL_TM, n)
    tiles = n // tm
    row = lambda i: (i, 0)
    lay = lambda i: (layer, 0, 0)
    once = pl.Buffered(1)

    def wspec(shape):
        return pl.BlockSpec((None,) + shape, lay, pipeline_mode=once)

    return pl.pallas_call(
        _tail_kernel,
        grid=(tiles,),
        in_specs=[pl.BlockSpec((tm, D_MODEL), row),
                  pl.BlockSpec((tm, GROUP_W), row),
                  pl.BlockSpec((tm, GROUP_W), row),
                  pl.BlockSpec((tm, 2 * GROUP_W), row),
                  pl.BlockSpec((None, tm, PLE_DIM), lambda i: (layer, i, 0)),
                  wspec((D_MODEL, D_MODEL)),
                  wspec((1, D_MODEL)),
                  wspec((D_MODEL, D_FF)),
                  wspec((D_FF, D_MODEL)),
                  wspec((1, D_MODEL)),
                  wspec((D_MODEL, D_MODEL)),
                  wspec((PLE_DIM, D_MODEL)),
                  wspec((1, D_MODEL))],
        out_specs=pl.BlockSpec((tm, D_MODEL), row),
        out_shape=jax.ShapeDtypeStruct((n, D_MODEL), F32),
        compiler_params=_params(1),
        name="tail",
    )(x, mm, mh, ms, p, w_out, g2, w_up, w_down, g3, w_gate, w_proj, g4)


def kernel(x, p, positions, in_norm_g, w_in, b_in, mlstm_f_bias, mlstm_conv_w, mlstm_conv_b,
           mlstm_norm_g, hgrn_lb_logits, hgrn_norm_g, swa_q_norm_g, swa_k_norm_g, swa_sinks,
           w_out, mlp_norm_g, w_up, w_down, ple_norm_g, w_ple_gate, w_ple_proj, ple_post_norm_g):
    batch, seq, d_model = x.shape
    depth = w_in.shape[0]
    n = batch * seq
    assert d_model == D_MODEL and seq % max(MIX_ROWS, IN_TM) == 0

    n_raw = GATE_OFF + GATE_COLS
    pad = LANES - GATE_COLS
    w_in_p = jnp.concatenate(
        [w_in[..., :n_raw], jnp.zeros((depth, D_MODEL, pad), w_in.dtype), w_in[..., n_raw:]],
        axis=-1).astype(BF16)
    b_in_p = jnp.concatenate(
        [b_in[..., :n_raw], jnp.zeros((depth, pad), b_in.dtype), b_in[..., n_raw:]],
        axis=-1).astype(F32).reshape(depth, 1, IN_COLS)
    f_bias = jnp.zeros((depth, 1, LANES), F32).at[:, 0, N_HEADS:GATE_COLS].set(mlstm_f_bias.astype(F32))
    row3 = lambda a: a.astype(F32).reshape(depth, 1, a.shape[-1])
    q_gain = row3(jnp.tile(swa_q_norm_g, (1, LANES // HEAD_DIM)))
    k_gain = row3(jnp.tile(swa_k_norm_g, (1, LANES // HEAD_DIM)))
    w_out_b, w_up_b, w_down_b = w_out.astype(BF16), w_up.astype(BF16), w_down.astype(BF16)
    w_gate_b, w_proj_b = w_ple_gate.astype(BF16), w_ple_proj.astype(BF16)

    blk256 = _blk_ones(GROUP_W)
    tril = jnp.asarray(np.tril(np.ones((MLSTM_T, MLSTM_T), np.float32)), dtype=BF16)
    expand = jnp.asarray(
        (np.arange(LANES)[:, None] == (np.arange(GROUP_W) // HEAD_DIM)[None, :]).astype(np.float32),
        dtype=BF16)
    hgrn_tril, hgrn_pmask = _hgrn_constants()

    cos_t, sin_t = _rope_tables(positions)
    la, l1m, oml = [row3(a) for a in _hgrn_lower_bounds(hgrn_lb_logits)]
    worst_log2 = (HGRN_L // 2) * LOG2_E * jnp.min(la, axis=(1, 2))
    hgrn_bounded = worst_log2 > -HGRN_SAFE_LOG2
    sinks = swa_sinks.astype(F32)

    xf = x.reshape(n, D_MODEL)
    for l in range(depth):
        ml, gt, hb, lf, sw = _in_proj(
            xf, row3(in_norm_g), w_in_p, b_in_p, mlstm_conv_w.astype(F32), row3(mlstm_conv_b),
            la, l1m, oml, cos_t, sin_t, q_gain, k_gain, blk256, l, seq)
        mix_consts = (blk256, tril, expand, hgrn_tril, hgrn_pmask)
        mm, mh, ms = lax.cond(
            hgrn_bounded[l],
            lambda *a: _mixers(*a, l, batch, mix_consts, True),
            lambda *a: _mixers(*a, l, batch, mix_consts, False),
            ml, gt, hb, lf, sw, f_bias, row3(mlstm_norm_g), row3(hgrn_norm_g), sinks)
        xf = _tail(xf, mm, mh, ms, p.reshape(depth, n, PLE_DIM), w_out_b, row3(mlp_norm_g), w_up_b,
                   w_down_b, row3(ple_norm_g), w_gate_b, w_proj_b, row3(ple_post_norm_g), l)
    return xf.reshape(batch, seq, D_MODEL)
```

```python
import functools

import numpy as np
import jax
import jax.numpy as jnp
from jax import lax
from jax.experimental import pallas as pl
from jax.experimental.pallas import tpu as pltpu

F32 = jnp.float32
BF16 = jnp.bfloat16

D_MODEL = 1024
N_HEADS = 4
HEAD_DIM = 64
GROUP_W = N_HEADS * HEAD_DIM
CONV_K = 4
SWA_Q_HEADS = 8
SWA_KV_HEADS = 2
SWA_GROUP = SWA_Q_HEADS // SWA_KV_HEADS
WINDOW = 128
ROPE_THETA = 500000.0
ROT_DIM = HEAD_DIM // 4
ROT_HALF = ROT_DIM // 2
D_FF = 4 * D_MODEL
PLE_DIM = 256
EPS = 1e-6
LOG2_E = 1.4426950408889634

LANES = 128
SUBLANES = 8
VMEM_LIMIT_BYTES = 56 * 1024 * 1024

GATE_COLS = 2 * N_HEADS
M_COLS = 4 * GROUP_W + LANES
H_COLS = 4 * GROUP_W
S_COLS = SWA_Q_HEADS * HEAD_DIM + 2 * SWA_KV_HEADS * HEAD_DIM
IN_COLS = M_COLS + H_COLS + S_COLS
GATE_OFF = 4 * GROUP_W

IN_TM = 1024
TAIL_TM = 512
FF_CHUNK = 1024
GATE_CHUNK = 256
ROPE_TM = 2048
MIX_ROWS = 1024
MLSTM_T = 256
HGRN_L = 64
HGRN_SAFE_LOG2 = 100.0
SWA_T = MIX_ROWS

N_SPLIT = 2
assert HGRN_L == HEAD_DIM


def _params(n_grid_dims):
    return pltpu.CompilerParams(
        dimension_semantics=("arbitrary",) * n_grid_dims,
        vmem_limit_bytes=VMEM_LIMIT_BYTES)


def _dot(a, b):
    return jnp.dot(a, b, preferred_element_type=F32)


def _dot_nt(a, b):
    return lax.dot_general(a, b, (((1,), (1,)), ((), ())), preferred_element_type=F32)


def _dot_tn(a, b):
    return lax.dot_general(a, b, (((0,), (0,)), ((), ())), preferred_element_type=F32)


def _split(x, n=N_SPLIT):
    pieces = []
    r = x
    for _ in range(n):
        p = r.astype(BF16)
        pieces.append(p)
        r = r - p.astype(F32)
    return pieces


def _dot_const_left(m, x):
    acc = None
    for p in _split(x):
        t = _dot(m, p)
        acc = t if acc is None else acc + t
    return acc


def _dot_const_right(x, m):
    acc = None
    for p in _split(x):
        t = _dot(p, m)
        acc = t if acc is None else acc + t
    return acc


def _sigmoid(x):
    return 1.0 / (1.0 + jnp.exp(-x))


def _log_sigmoid(x):
    return jnp.minimum(x, 0.0) - jnp.log1p(jnp.exp(-jnp.abs(x)))


def _rms(x, g):
    return x * lax.rsqrt(jnp.mean(x * x, axis=-1, keepdims=True) + EPS) * g


def _head_rms(x, g, blk_ones):
    ms = _dot(jnp.square(x).astype(BF16), blk_ones) * (1.0 / HEAD_DIM)
    return x * lax.rsqrt(ms + EPS) * g


def _lane_head_id(shape):
    return lax.broadcasted_iota(jnp.int32, shape, len(shape) - 1) // HEAD_DIM


def _blk_ones(width):
    hid = np.arange(width) // HEAD_DIM
    return jnp.asarray((hid[:, None] == hid[None, :]).astype(np.float32), dtype=BF16)


def _hgrn_level_sizes():
    sizes = []
    c = 2
    while c <= HGRN_L:
        sizes.append(c)
        c *= 2
    return sizes


def _hgrn_constants():
    L = HGRN_L
    t = np.arange(L)[:, None]
    u = np.arange(L)[None, :]
    masks = [(t == u)]
    for c in _hgrn_level_sizes():
        mid = (t // c) * c + c // 2 - 1
        same_blk = (t // c) == (u // c)
        u_mid = (u // c) * c + c // 2 - 1
        masks.append(same_blk & (t > mid) & (u <= u_mid))
    masks.append(u <= t)
    tril = (u <= t).astype(np.float32)
    pmask = np.stack([np.tile(m.astype(np.float32), (1, N_HEADS)) for m in masks])
    return jnp.asarray(tril, dtype=BF16), jnp.asarray(pmask, dtype=F32)


ROPE_PER_ROW = LANES // ROT_HALF


def _rope_kernel(pos_ref, freq_ref, sel_ref, one_ref, cos_ref, sin_ref):
    ang = pos_ref[...].astype(F32) * freq_ref[0:1, :]
    cos_ref[...] = _dot_const_right(jnp.cos(ang), sel_ref[0]) + one_ref[0:1, :]
    sin_ref[...] = _dot_const_right(jnp.sin(ang), sel_ref[1])


def _rope_tables(positions):
    n = positions.size
    per_row = ROPE_PER_ROW
    rows = n // per_row
    wide = per_row * LANES
    inv_freq = ROPE_THETA ** (-jnp.arange(0, ROT_DIM, 2, dtype=F32) / ROT_DIM)
    freq = jnp.zeros((SUBLANES, LANES), F32).at[0].set(jnp.tile(inv_freq, per_row))
    pos = jnp.repeat(positions.reshape(rows, per_row), ROT_HALF, axis=1)
    col = np.arange(wide)
    tok, lane = col // LANES, (col % LANES) % HEAD_DIM
    rotated = lane < ROT_DIM
    pick = (np.arange(LANES)[:, None] == (tok * ROT_HALF + lane % ROT_HALF)[None, :]) & rotated[None, :]
    sign = np.where(lane < ROT_HALF, -1.0, 1.0)[None, :]
    sel = jnp.asarray(np.stack([pick.astype(np.float32), pick * sign]), dtype=BF16)
    unrotated = jnp.asarray(np.broadcast_to((~rotated).astype(np.float32)[None, :], (SUBLANES, wide)))
    tm = min(ROPE_TM // per_row, rows)
    out = jax.ShapeDtypeStruct((rows, wide), F32)
    cos_w, sin_w = pl.pallas_call(
        _rope_kernel,
        grid=(rows // tm,),
        in_specs=[pl.BlockSpec((tm, LANES), lambda i: (i, 0)),
                  pl.BlockSpec((SUBLANES, LANES), lambda i: (0, 0)),
                  pl.BlockSpec(sel.shape, lambda i: (0, 0, 0)),
                  pl.BlockSpec((SUBLANES, wide), lambda i: (0, 0))],
        out_specs=[pl.BlockSpec((tm, wide), lambda i: (i, 0))] * 2,
        out_shape=[out, out],
        compiler_params=_params(1),
        name="rope_tables",
    )(pos, freq, sel, unrotated)
    return cos_w.reshape(n, LANES), sin_w.reshape(n, LANES)


def _lb_kernel(lg_ref, la_ref, l1m_ref, oml_ref):
    z = lg_ref[...]
    e = jnp.exp(z - jnp.max(z, axis=0, keepdims=True))
    sm = e / jnp.sum(e, axis=0, keepdims=True)
    depth = z.shape[0]
    run = sm[0:1, :]
    first = run
    for l in range(depth):
        if l > 0:
            run = run + sm[l:l + 1, :]
        lb = run - first
        la_ref[l:l + 1, :] = jnp.log(lb)
        l1m_ref[l:l + 1, :] = jnp.log1p(-lb)
        oml_ref[l:l + 1, :] = 1.0 - lb


def _hgrn_lower_bounds(logits):
    out = jax.ShapeDtypeStruct(logits.shape, F32)
    return pl.pallas_call(
        _lb_kernel, out_shape=[out, out, out], name="hgrn_lower_bounds",
    )(logits.astype(F32))


def _in_proj_kernel(x_ref, g_ref, w_ref, b_ref, cw_ref, cb_ref, la_ref, l1m_ref, oml_ref,
                    cos_ref, sin_ref, qg_ref, kg_ref, ones_ref,
                    ml_ref, gt_ref, hb_ref, lf_ref, sw_ref, xbuf, *, tiles_per_seq):
    TM = x_ref.shape[0]
    W = GROUP_W

    @pl.when(pl.program_id(0) % tiles_per_seq == 0)
    def _():
        xbuf[0:SUBLANES, :] = jnp.zeros((SUBLANES, 2 * W), F32)

    h = _rms(x_ref[...], g_ref[...]).astype(BF16)

    def proj(lo, width):
        return _dot(h, w_ref[:, lo:lo + width]) + b_ref[:, lo:lo + width]

    cos_t = cos_ref[...]
    sin_t = sin_ref[...]
    blk = ones_ref[...]
    lane = lax.broadcasted_iota(jnp.int32, (TM, LANES), 1)
    low_half = lane < HEAD_DIM
    rot_first = (lane % HEAD_DIM) < ROT_HALF
    s0 = M_COLS + H_COLS
    QW = SWA_Q_HEADS * HEAD_DIM

    def head_sumsq(y):
        width = y.shape[-1]
        return _dot(jnp.square(y).astype(BF16), blk[0:width, 0:width])

    def norm_rope(v, sumsq, gain):
        xn = v * lax.rsqrt(sumsq * (1.0 / HEAD_DIM) + EPS) * gain
        partner = jnp.where(rot_first, pltpu.roll(xn, LANES - ROT_HALF, 1), pltpu.roll(xn, ROT_HALF, 1))
        return xn * cos_t + partner * sin_t

    def swa_q(c, y, sumsq):
        for half in range(2):
            sl = slice(half * LANES, (half + 1) * LANES)
            sw_ref[:, (c + half) * LANES:(c + half + 1) * LANES] = (
                norm_rope(y[:, sl], sumsq[:, sl], qg_ref[...]) * (HEAD_DIM ** -0.5 * LOG2_E)).astype(BF16)

    def swa_kv(y, sumsq):
        k = norm_rope(y[:, 0:LANES], sumsq, kg_ref[...])
        v = y[:, LANES:2 * LANES]
        k_sw = pltpu.roll(k, HEAD_DIM, 1)
        v_sw = pltpu.roll(v, HEAD_DIM, 1)
        sw_ref[:, QW:QW + LANES] = jnp.where(low_half, k, k_sw).astype(BF16)
        sw_ref[:, QW + LANES:QW + 2 * LANES] = jnp.where(low_half, k_sw, k).astype(BF16)
        sw_ref[:, QW + 2 * LANES:QW + 3 * LANES] = jnp.where(low_half, v, v_sw).astype(BF16)
        sw_ref[:, QW + 3 * LANES:QW + 4 * LANES] = jnp.where(low_half, v_sw, v).astype(BF16)

    def mlstm_qk(y):
        xbuf[SUBLANES:SUBLANES + TM, :] = y
        acc = cb_ref[...] + cw_ref[CONV_K - 1:CONV_K, :] * xbuf[SUBLANES:SUBLANES + TM, :]
        for j in range(1, CONV_K):
            acc = acc + cw_ref[CONV_K - 1 - j:CONV_K - j, :] * xbuf[SUBLANES - j:SUBLANES - j + TM, :]
        xbuf[0:SUBLANES, :] = xbuf[TM:TM + SUBLANES, :]
        qk = acc * _sigmoid(acc)
        ml_ref[:, 0:W] = qk[:, 0:W].astype(BF16)
        ml_ref[:, W:2 * W] = (qk[:, W:2 * W] * (HEAD_DIM ** -0.5)).astype(BF16)

    def hgrn_qf(y):
        hq = y[:, 0:W]
        hf = y[:, W:2 * W]
        la = la_ref[...]
        u = jnp.exp(-jnp.abs(hf))
        t = l1m_ref[...] + (jnp.minimum(hf, 0.0) - jnp.log1p(u))
        lf_ref[...] = jnp.maximum(la, t) + jnp.log1p(jnp.exp(-jnp.abs(la - t)))
        hb_ref[:, 0:W] = (hq * _sigmoid(hq)).astype(BF16)
        hb_ref[:, W:2 * W] = (oml_ref[...] * jnp.where(hf >= 0.0, u, 1.0) / (1.0 + u)).astype(BF16)

    y_h = proj(M_COLS, 2 * W)
    y_m = proj(0, 2 * W)
    hgrn_qf(y_h)
    y_kv = proj(s0 + QW, 2 * LANES)
    mlstm_qk(y_m)
    y_q0 = proj(s0, 2 * LANES)
    swa_kv(y_kv, head_sumsq(y_kv[:, 0:LANES]))
    y_q1 = proj(s0 + 2 * LANES, 2 * LANES)
    swa_q(0, y_q0, head_sumsq(y_q0))
    y_h2 = proj(M_COLS + 2 * W, 2 * W)
    swa_q(2, y_q1, head_sumsq(y_q1))
    y_m2 = proj(2 * W, 2 * W)
    hg = y_h2[:, W:2 * W]
    hb_ref[:, 2 * W:3 * W] = y_h2[:, 0:W].astype(BF16)
    hb_ref[:, 3 * W:4 * W] = (hg * _sigmoid(hg)).astype(BF16)
    gt_ref[...] = proj(GATE_OFF, LANES)
    ml_ref[:, 2 * W:3 * W] = y_m2[:, 0:W].astype(BF16)
    ml_ref[:, 3 * W:4 * W] = _sigmoid(y_m2[:, W:2 * W]).astype(BF16)


def _in_proj(x, g, w, b, conv_w, conv_b, la, l1m, oml, cos_t, sin_t, q_gain, k_gain, blk_ones,
             layer, seq):
    n = x.shape[0]
    tm = min(IN_TM, n)
    row = lambda i: (i, 0)
    lay = lambda i: (layer, 0, 0)
    W4 = 4 * GROUP_W
    return pl.pallas_call(
        functools.partial(_in_proj_kernel, tiles_per_seq=seq // tm),
        grid=(n // tm,),
        in_specs=[pl.BlockSpec((tm, D_MODEL), row),
                  pl.BlockSpec((None, 1, D_MODEL), lay),
                  pl.BlockSpec((None, D_MODEL, IN_COLS), lay),
                  pl.BlockSpec((None, 1, IN_COLS), lay),
                  pl.BlockSpec((None, CONV_K, 2 * GROUP_W), lay),
                  pl.BlockSpec((None, 1, 2 * GROUP_W), lay),
                  pl.BlockSpec((None, 1, GROUP_W), lay),
                  pl.BlockSpec((None, 1, GROUP_W), lay),
                  pl.BlockSpec((None, 1, GROUP_W), lay),
                  pl.BlockSpec((tm, LANES), row),
                  pl.BlockSpec((tm, LANES), row),
                  pl.BlockSpec((None, 1, LANES), lay),
                  pl.BlockSpec((None, 1, LANES), lay),
                  pl.BlockSpec((GROUP_W, GROUP_W), lambda i: (0, 0))],
        out_specs=[pl.BlockSpec((tm, W4), row),
                   pl.BlockSpec((tm, LANES), row),
                   pl.BlockSpec((tm, W4), row),
                   pl.BlockSpec((tm, GROUP_W), row),
                   pl.BlockSpec((tm, W4), row)],
        out_shape=[jax.ShapeDtypeStruct((n, W4), BF16),
                   jax.ShapeDtypeStruct((n, LANES), F32),
                   jax.ShapeDtypeStruct((n, W4), BF16),
                   jax.ShapeDtypeStruct((n, GROUP_W), F32),
                   jax.ShapeDtypeStruct((n, W4), BF16)],
        scratch_shapes=[pltpu.VMEM((tm + SUBLANES, 2 * GROUP_W), F32)],
        compiler_params=_params(1),
        name="in_proj",
    )(x, g, w, b, conv_w, conv_b, la, l1m, oml, cos_t, sin_t, q_gain, k_gain, blk_ones)


def _mlstm_chunk(r0, y_ref, gt_ref, fb_ref, g_ref, ones_ref, onesf_ref, tril_ref, exp_ref, o_ref,
                 c_st, n_st, m_st):
    T = MLSTM_T
    W = GROUP_W
    qb = y_ref[r0:r0 + T, 0:W]
    kb = y_ref[r0:r0 + T, W:2 * W]
    vb = y_ref[r0:r0 + T, 2 * W:3 * W]
    out_gate = y_ref[r0:r0 + T, 3 * W:4 * W]

    gates = gt_ref[r0:r0 + T, :]
    i_pre = gates
    lf = pltpu.roll(_log_sigmoid(gates + fb_ref[...]), LANES - N_HEADS, 1)
    lf_hi = lf.astype(BF16)
    lf_lo = (lf - lf_hi.astype(F32)).astype(BF16)
    bb = _dot(tril_ref[...], jnp.concatenate([lf_hi, lf_lo], axis=1))
    b = bb[:, 0:LANES] + bb[:, LANES:2 * LANES]
    a = i_pre - b
    row = lax.broadcasted_iota(jnp.int32, (T, LANES), 0)
    cm = a
    sh = 1
    while sh < T:
        cm = jnp.maximum(cm, jnp.where(row >= sh, pltpu.roll(cm, sh, 0), -jnp.inf))
        sh *= 2
    m_prev = m_st[0:1, :]
    m_j = b + jnp.maximum(m_prev, cm)
    m_new = m_j[T - 1:T, :]
    b_last = b[T - 1:T, :]
    w_inter = jnp.exp(b + m_prev - m_j)
    e_den = jnp.exp(-m_j)
    w_s = jnp.exp(b_last + a - m_new)
    decay = jnp.exp(b_last + m_prev - m_new)
    cexp = (b - m_j) * LOG2_E
    a_t = (a * LOG2_E).T

    stacked = jnp.concatenate([w_inter, e_den, w_s], axis=0)
    head_lane = lax.broadcasted_iota(jnp.int32, stacked.shape, 1) < N_HEADS
    ex = _dot(jnp.where(head_lane, stacked, 0.0).astype(BF16), exp_ref[...])
    w_inter_x = ex[0:T]
    e_den_x = ex[T:2 * T]
    w_s_x = ex[2 * T:3 * T]
    decay8 = jnp.where(head_lane[0:SUBLANES], jnp.broadcast_to(decay, (SUBLANES, LANES)), 0.0)
    decay_x = _dot_const_right(decay8, exp_ref[...])[0:1]

    blk = ones_ref[...]
    hid = _lane_head_id((T, W))
    causal = (lax.broadcasted_iota(jnp.int32, (T, T), 1)
              <= lax.broadcasted_iota(jnp.int32, (T, T), 0))
    head_rows = [blk[h * HEAD_DIM:h * HEAD_DIM + 1, :] for h in range(N_HEADS)]
    s_all = _dot_nt(jnp.concatenate([qb * hr for hr in head_rows], axis=0), kb)
    num = None
    den = None
    for h in range(N_HEADS):
        dm = cexp[:, h:h + 1] + a_t[h:h + 1, :]
        s = s_all[h * T:(h + 1) * T] * jnp.exp2(jnp.where(causal, dm, -jnp.inf))
        dn = _dot(s.astype(BF16), vb * head_rows[h])
        dd = jnp.where(hid == h, jnp.sum(s, axis=-1, keepdims=True), 0.0)
        num = dn if num is None else num + dn
        den = dd if den is None else den + dd
    n_row = n_st[0:1, :].astype(BF16)
    num = num + w_inter_x * _dot(qb, c_st[...].astype(BF16))
    den = den + w_inter_x * _dot(qb * n_row, blk)
    hh = num / jnp.maximum(jnp.abs(den), e_den_x)

    o_ref[r0:r0 + T, :] = _head_rms(hh, g_ref[...], blk) * out_gate.astype(F32)

    kw = kb * w_s_x.astype(BF16)
    c_st[...] = decay_x * c_st[...] + onesf_ref[...] * _dot_tn(kw, vb)
    n_st[...] = decay_x * n_st[...] + _dot(jnp.ones((SUBLANES, T), BF16), kw)
    m_st[...] = jnp.broadcast_to(m_new, m_st.shape)


def _hgrn_chunk(r0, y_ref, lf_ref, g_ref, ones_ref, onesf_ref, mst_ref, pm_ref, o_ref, s_st, bounded):
    L = HGRN_L
    W = GROUP_W
    gain = g_ref[...]
    blk = ones_ref[...]
    blk_f = onesf_ref[...]
    tril = mst_ref[...]
    row = lax.broadcasted_iota(jnp.int32, (L, W), 0)

    def level_gap(g, lf2, c):
        if c == 2:
            return jnp.where(row % 2 == 1, lf2, 0.0)
        if c >= SUBLANES:
            mids = [b * c + c // 2 - 1 for b in range(L // c)]
            return g - jnp.concatenate(
                [jnp.broadcast_to(g[m:m + 1, :], (c, W)) for m in mids], axis=0)
        lo = jnp.concatenate([jnp.broadcast_to(g[m:m + 1, :], (SUBLANES, W))
                              for m in range(1, L, SUBLANES)], axis=0)
        hi = jnp.concatenate([jnp.broadcast_to(g[m:m + 1, :], (SUBLANES, W))
                              for m in range(5, L, SUBLANES)], axis=0)
        return g - jnp.where(row % SUBLANES < 4, lo, hi)

    def head_stack(xb):
        return jnp.concatenate([xb] * N_HEADS, axis=0) * blk

    qf = y_ref[r0:r0 + L, 0:W]
    key = y_ref[r0:r0 + L, W:2 * W]
    vb = y_ref[r0:r0 + L, 2 * W:3 * W]
    out_gate = y_ref[r0:r0 + L, 3 * W:4 * W]

    lf2 = lf_ref[r0:r0 + L, :] * LOG2_E
    g = _dot_const_left(tril, lf2)
    g_last = g[L - 1:L, :]
    e_last = jnp.exp2(g_last)

    if bounded:
        g_mid = g[L // 2 - 1:L // 2, :]
        xq = qf * jnp.exp2(g - g_mid).astype(BF16)
        xk = key * jnp.exp2(g_mid - g).astype(BF16)
        a = pm_ref[pm_ref.shape[0] - 1] * _dot_nt(xq, head_stack(xk))
        q_hat = xq * jnp.exp2(g_mid).astype(BF16)
        k_hat = xk * jnp.exp2(g_last - g_mid).astype(BF16)
    else:
        q_hat = qf * jnp.exp2(g).astype(BF16)
        k_hat = key * jnp.exp2(g_last - g).astype(BF16)
        a = pm_ref[0] * _dot_nt(qf, head_stack(key))
        for i, c in enumerate(_hgrn_level_sizes()):
            e_i = jnp.exp2(-jnp.abs(level_gap(g, lf2, c))).astype(BF16)
            a = a + pm_ref[1 + i] * _dot_nt(qf * e_i, head_stack(key * e_i))
    o = _dot(a.astype(BF16), head_stack(vb)) + _dot_nt(q_hat, s_st[...].astype(BF16))

    o_ref[r0:r0 + L, :] = _head_rms(o, gain, blk) * out_gate.astype(F32)
    s_st[...] = s_st[...] * e_last + blk_f * _dot_tn(vb, k_hat)


def _swa_blocks(sink_ref, y_ref, o_ref, k_prev, v_prev, layer):
    T = SWA_T
    Wn = WINDOW
    QW = SWA_Q_HEADS * HEAD_DIM
    step = pl.program_id(1)

    k2 = [jnp.concatenate([k_prev[g], y_ref[:, QW + g * LANES:QW + (g + 1) * LANES]], axis=0)
          for g in range(SWA_KV_HEADS)]
    v2 = [jnp.concatenate([v_prev[g], y_ref[:, QW + (2 + g) * LANES:QW + (3 + g) * LANES]], axis=0)
          for g in range(SWA_KV_HEADS)]
    for g in range(SWA_KV_HEADS):
        k_prev[g] = k2[g][T:T + Wn]
        v_prev[g] = v2[g][T:T + Wn]

    qi = lax.broadcasted_iota(jnp.int32, (Wn, 2 * Wn), 0)
    ki = lax.broadcasted_iota(jnp.int32, (Wn, 2 * Wn), 1)
    band = (ki > qi) & (ki <= qi + Wn)
    band_first = band & (ki >= jnp.where(step > 0, 0, Wn))
    lane_w = lax.broadcasted_iota(jnp.int32, (Wn, LANES), 1) < HEAD_DIM
    lane_row = lax.broadcasted_iota(jnp.int32, (1, LANES), 1) < HEAD_DIM
    low_row = jnp.where(lane_row, 1.0, 0.0).astype(BF16)
    high_row = jnp.where(lane_row, 0.0, 1.0).astype(BF16)

    def scores(j, g):
        r0 = j * Wn
        kc = k2[g][r0:r0 + 2 * Wn]
        parts = []
        for c in (2 * g, 2 * g + 1):
            qc = y_ref[r0:r0 + Wn, c * LANES:(c + 1) * LANES]
            parts.append(qc * low_row)
            parts.append(qc * high_row)
        return _dot_nt(jnp.concatenate(parts, axis=0), kc)

    def finish(j, g, s):
        r0 = j * Wn
        mask = band_first if j == 0 else band
        vc = v2[g][r0:r0 + 2 * Wn]
        outs = []
        for r in range(SWA_GROUP):
            sink = sink_ref[layer, SWA_GROUP * g + r] * LOG2_E
            sr = jnp.where(mask, s[r * Wn:(r + 1) * Wn], -jnp.inf)
            m = jnp.maximum(jnp.max(sr, axis=-1, keepdims=True), sink)
            pexp = jnp.exp2(sr - m)
            den = jnp.sum(pexp, axis=-1, keepdims=True) + jnp.exp2(sink - m)
            outs.append(_dot(pexp.astype(BF16), vc) / den)
        c0 = 2 * g * LANES
        o_ref[r0:r0 + Wn, c0:c0 + LANES] = jnp.where(lane_w, outs[0], outs[1])
        o_ref[r0:r0 + Wn, c0 + LANES:c0 + 2 * LANES] = jnp.where(lane_w, outs[2], outs[3])

    return [(functools.partial(scores, j, g), functools.partial(finish, j, g))
            for j in range(T // Wn) for g in range(SWA_KV_HEADS)]


def _mixers_kernel(sink_ref, ml_ref, gt_ref, fb_ref, gm_ref, hb_ref, lf_ref, gh_ref, sw_ref,
                   ones_ref, onesf_ref, tril_ref, exp_ref, tril64_ref, pm_ref,
                   om_ref, oh_ref, os_ref,
                   c_st, n_st, m_st, s_st, k_prev, v_prev, *, layer, bounded):
    @pl.when(pl.program_id(1) == 0)
    def _():
        for ref in (c_st, n_st, m_st, s_st, k_prev, v_prev):
            ref[...] = jnp.zeros_like(ref)

    swa_blocks = _swa_blocks(sink_ref, sw_ref, os_ref, k_prev, v_prev, layer)
    n_sub = MIX_ROWS // HGRN_L
    assert len(swa_blocks) == n_sub and n_sub % (MIX_ROWS // MLSTM_T) == 0
    per_mlstm = n_sub // (MIX_ROWS // MLSTM_T)
    for sub in range(n_sub):
        swa_scores, swa_finish = swa_blocks[sub]
        s = swa_scores()
        _hgrn_chunk(sub * HGRN_L, hb_ref, lf_ref, gh_ref, ones_ref, onesf_ref, tril64_ref, pm_ref, oh_ref,
                    s_st, bounded)
        swa_finish(s)
        if sub % per_mlstm == 0:
            _mlstm_chunk((sub // per_mlstm) * MLSTM_T, ml_ref, gt_ref, fb_ref, gm_ref, ones_ref, onesf_ref,
                         tril_ref, exp_ref, om_ref, c_st, n_st, m_st)


def _mixers(ml, gt, hb, lf, sw, f_bias, mlstm_g, hgrn_g, sinks, layer, batch, consts, bounded):
    n = ml.shape[0]
    seq = n // batch
    T = MIX_ROWS
    steps = seq // T
    W4 = 4 * GROUP_W
    QW = SWA_Q_HEADS * HEAD_DIM
    row = lambda bi, ci: (bi * steps + ci, 0)
    lay = lambda bi, ci: (layer, 0, 0)
    cst = lambda bi, ci: (0, 0)
    blk_ones, tril, expand, tril64, pmask = consts
    return pl.pallas_call(
        functools.partial(_mixers_kernel, layer=layer, bounded=bounded),
        grid=(batch, steps),
        in_specs=[pl.BlockSpec(memory_space=pltpu.SMEM),
                  pl.BlockSpec((T, W4), row),
                  pl.BlockSpec((T, LANES), row),
                  pl.BlockSpec((None, 1, LANES), lay),
                  pl.BlockSpec((None, 1, GROUP_W), lay),
                  pl.BlockSpec((T, W4), row),
                  pl.BlockSpec((T, GROUP_W), row),
                  pl.BlockSpec((None, 1, GROUP_W), lay),
                  pl.BlockSpec((T, W4), row),
                  pl.BlockSpec((GROUP_W, GROUP_W), cst),
                  pl.BlockSpec((GROUP_W, GROUP_W), cst),
                  pl.BlockSpec((MLSTM_T, MLSTM_T), cst),
                  pl.BlockSpec((LANES, GROUP_W), cst),
                  pl.BlockSpec((HGRN_L, HGRN_L), cst),
                  pl.BlockSpec(pmask.shape, lambda bi, ci: (0, 0, 0))],
        out_specs=[pl.BlockSpec((T, GROUP_W), row),
                   pl.BlockSpec((T, GROUP_W), row),
                   pl.BlockSpec((T, QW), row)],
        out_shape=[jax.ShapeDtypeStruct((n, GROUP_W), F32),
                   jax.ShapeDtypeStruct((n, GROUP_W), F32),
                   jax.ShapeDtypeStruct((n, QW), F32)],
        scratch_shapes=[pltpu.VMEM((GROUP_W, GROUP_W), F32),
                        pltpu.VMEM((SUBLANES, GROUP_W), F32),
                        pltpu.VMEM((SUBLANES, LANES), F32),
                        pltpu.VMEM((GROUP_W, GROUP_W), F32),
                        pltpu.VMEM((SWA_KV_HEADS, WINDOW, LANES), BF16),
                        pltpu.VMEM((SWA_KV_HEADS, WINDOW, LANES), BF16)],
        compiler_params=_params(2),
        name="mixers",
    )(sinks, ml, gt, f_bias, mlstm_g, hb, lf, hgrn_g, sw, blk_ones, blk_ones.astype(F32), tril, expand,
      tril64, pmask)


def _tail_kernel(x_ref, mm_ref, mh_ref, ms_ref, p_ref, wo_ref, g2_ref, wu_ref, wd_ref, g3_ref,
                 wg_ref, wp_ref, g4_ref, o_ref):
    W = GROUP_W
    mix = _dot(mm_ref[...].astype(BF16), wo_ref[0:W, :])
    mix = mix + _dot(mh_ref[...].astype(BF16), wo_ref[W:2 * W, :])
    mix = mix + _dot(ms_ref[...].astype(BF16), wo_ref[2 * W:4 * W, :])
    x = x_ref[...] + mix
    emb = _rms(_dot(p_ref[...].astype(BF16), wp_ref[...]), g4_ref[...])

    h2 = _rms(x, g2_ref[...]).astype(BF16)
    mlp = None
    for c in range(D_FF // FF_CHUNK):
        u = _dot(h2, wu_ref[:, c * FF_CHUNK:(c + 1) * FF_CHUNK])
        act = jnp.square(jnp.maximum(u, 0.0)).astype(BF16)
        d = _dot(act, wd_ref[c * FF_CHUNK:(c + 1) * FF_CHUNK, :])
        mlp = d if mlp is None else mlp + d
    x = x + mlp

    h3 = _rms(x, g3_ref[...]).astype(BF16)
    for c in range(D_MODEL // GATE_CHUNK):
        cols = slice(c * GATE_CHUNK, (c + 1) * GATE_CHUNK)
        gate = _sigmoid(_dot(h3, wg_ref[:, cols]))
        o_ref[:, cols] = x[:, cols] + gate * emb[:, cols]


def _tail(x, mm, mh, ms, p, w_out, g2, w_up, w_down, g3, w_gate, w_proj, g4, layer):
    n = x.shape[0]
    tm = min(TAIL_TM, n)
    tiles = n // tm
    row = lambda i: (i, 0)
    lay = lambda i: (layer, 0, 0)
    once = pl.Buffered(1)

    def wspec(shape):
        return pl.BlockSpec((None,) + shape, lay, pipeline_mode=once)

    return pl.pallas_call(
        _tail_kernel,
        grid=(tiles,),
        in_specs=[pl.BlockSpec((tm, D_MODEL), row),
                  pl.BlockSpec((tm, GROUP_W), row),
                  pl.BlockSpec((tm, GROUP_W), row),
                  pl.BlockSpec((tm, 2 * GROUP_W), row),
                  pl.BlockSpec((None, tm, PLE_DIM), lambda i: (layer, i, 0)),
                  wspec((D_MODEL, D_MODEL)),
                  wspec((1, D_MODEL)),
                  wspec((D_MODEL, D_FF)),
                  wspec((D_FF, D_MODEL)),
                  wspec((1, D_MODEL)),
                  wspec((D_MODEL, D_MODEL)),
                  wspec((PLE_DIM, D_MODEL)),
                  wspec((1, D_MODEL))],
        out_specs=pl.BlockSpec((tm, D_MODEL), row),
        out_shape=jax.ShapeDtypeStruct((n, D_MODEL), F32),
        compiler_params=_params(1),
        name="tail",
    )(x, mm, mh, ms, p, w_out, g2, w_up, w_down, g3, w_gate, w_proj, g4)


def kernel(x, p, positions, in_norm_g, w_in, b_in, mlstm_f_bias, mlstm_conv_w, mlstm_conv_b,
           mlstm_norm_g, hgrn_lb_logits, hgrn_norm_g, swa_q_norm_g, swa_k_norm_g, swa_sinks,
           w_out, mlp_norm_g, w_up, w_down, ple_norm_g, w_ple_gate, w_ple_proj, ple_post_norm_g):
    batch, seq, d_model = x.shape
    depth = w_in.shape[0]
    n = batch * seq
    assert d_model == D_MODEL and seq % max(MIX_ROWS, IN_TM) == 0

    n_raw = GATE_OFF + GATE_COLS
    pad = LANES - GATE_COLS
    w_in_p = jnp.concatenate(
        [w_in[..., :n_raw], jnp.zeros((depth, D_MODEL, pad), w_in.dtype), w_in[..., n_raw:]],
        axis=-1).astype(BF16)
    b_in_p = jnp.concatenate(
        [b_in[..., :n_raw], jnp.zeros((depth, pad), b_in.dtype), b_in[..., n_raw:]],
        axis=-1).astype(F32).reshape(depth, 1, IN_COLS)
    f_bias = jnp.zeros((depth, 1, LANES), F32).at[:, 0, N_HEADS:GATE_COLS].set(mlstm_f_bias.astype(F32))
    row3 = lambda a: a.astype(F32).reshape(depth, 1, a.shape[-1])
    q_gain = row3(jnp.tile(swa_q_norm_g, (1, LANES // HEAD_DIM)))
    k_gain = row3(jnp.tile(swa_k_norm_g, (1, LANES // HEAD_DIM)))
    w_out_b, w_up_b, w_down_b = w_out.astype(BF16), w_up.astype(BF16), w_down.astype(BF16)
    w_gate_b, w_proj_b = w_ple_gate.astype(BF16), w_ple_proj.astype(BF16)

    blk256 = _blk_ones(GROUP_W)
    tril = jnp.asarray(np.tril(np.ones((MLSTM_T, MLSTM_T), np.float32)), dtype=BF16)
    expand = jnp.asarray(
        (np.arange(LANES)[:, None] == (np.arange(GROUP_W) // HEAD_DIM)[None, :]).astype(np.float32),
        dtype=BF16)
    hgrn_tril, hgrn_pmask = _hgrn_constants()

    cos_t, sin_t = _rope_tables(positions)
    la, l1m, oml = [row3(a) for a in _hgrn_lower_bounds(hgrn_lb_logits)]
    worst_log2 = (HGRN_L // 2) * LOG2_E * jnp.min(la, axis=(1, 2))
    hgrn_bounded = worst_log2 > -HGRN_SAFE_LOG2
    sinks = swa_sinks.astype(F32)

    xf = x.reshape(n, D_MODEL)
    for l in range(depth):
        ml, gt, hb, lf, sw = _in_proj(
            xf, row3(in_norm_g), w_in_p, b_in_p, mlstm_conv_w.astype(F32), row3(mlstm_conv_b),
            la, l1m, oml, cos_t, sin_t, q_gain, k_gain, blk256, l, seq)
        mix_consts = (blk256, tril, expand, hgrn_tril, hgrn_pmask)
        mm, mh, ms = lax.cond(
            hgrn_bounded[l],
            lambda *a: _mixers(*a, l, batch, mix_consts, True),
            lambda *a: _mixers(*a, l, batch, mix_consts, False),
            ml, gt, hb, lf, sw, f_bias, row3(mlstm_norm_g), row3(hgrn_norm_g), sinks)
        xf = _tail(xf, mm, mh, ms, p.reshape(depth, n, PLE_DIM), w_out_b, row3(mlp_norm_g), w_up_b,
                   w_down_b, row3(ple_norm_g), w_gate_b, w_proj_b, row3(ple_post_norm_g), l)
    return xf.reshape(batch, seq, D_MODEL)
```

```python
import functools

import numpy as np
import jax
import jax.numpy as jnp
from jax import lax
from jax.experimental import pallas as pl
from jax.experimental.pallas import tpu as pltpu

F32 = jnp.float32
BF16 = jnp.bfloat16

D_MODEL = 1024
N_HEADS = 4
HEAD_DIM = 64
GROUP_W = N_HEADS * HEAD_DIM
CONV_K = 4
SWA_Q_HEADS = 8
SWA_KV_HEADS = 2
SWA_GROUP = SWA_Q_HEADS // SWA_KV_HEADS
WINDOW = 128
ROPE_THETA = 500000.0
ROT_DIM = HEAD_DIM // 4
ROT_HALF = ROT_DIM // 2
D_FF = 4 * D_MODEL
PLE_DIM = 256
EPS = 1e-6
LOG2_E = 1.4426950408889634

LANES = 128
SUBLANES = 8
VMEM_LIMIT_BYTES = 56 * 1024 * 1024

GATE_COLS = 2 * N_HEADS
M_COLS = 4 * GROUP_W + LANES
H_COLS = 4 * GROUP_W
S_COLS = SWA_Q_HEADS * HEAD_DIM + 2 * SWA_KV_HEADS * HEAD_DIM
IN_COLS = M_COLS + H_COLS + S_COLS
GATE_OFF = 4 * GROUP_W

IN_TM = 1024
TAIL_TM = 512
FF_CHUNK = 1024
GATE_CHUNK = 256
ROPE_TM = 2048
MIX_ROWS = 1024
MLSTM_T = 256
HGRN_L = 64
HGRN_SAFE_LOG2 = 100.0
SWA_T = MIX_ROWS

N_SPLIT = 2
assert HGRN_L == HEAD_DIM


def _params(n_grid_dims):
    return pltpu.CompilerParams(
        dimension_semantics=("arbitrary",) * n_grid_dims,
        vmem_limit_bytes=VMEM_LIMIT_BYTES)


def _dot(a, b):
    return jnp.dot(a, b, preferred_element_type=F32)


def _dot_nt(a, b):
    return lax.dot_general(a, b, (((1,), (1,)), ((), ())), preferred_element_type=F32)


def _dot_tn(a, b):
    return lax.dot_general(a, b, (((0,), (0,)), ((), ())), preferred_element_type=F32)


def _split(x, n=N_SPLIT):
    pieces = []
    r = x
    for _ in range(n):
        p = r.astype(BF16)
        pieces.append(p)
        r = r - p.astype(F32)
    return pieces


def _dot_const_left(m, x):
    acc = None
    for p in _split(x):
        t = _dot(m, p)
        acc = t if acc is None else acc + t
    return acc


def _dot_const_right(x, m):
    acc = None
    for p in _split(x):
        t = _dot(p, m)
        acc = t if acc is None else acc + t
    return acc


def _sigmoid(x):
    return 1.0 / (1.0 + jnp.exp(-x))


def _log_sigmoid(x):
    return jnp.minimum(x, 0.0) - jnp.log1p(jnp.exp(-jnp.abs(x)))


def _rms(x, g):
    return x * lax.rsqrt(jnp.mean(x * x, axis=-1, keepdims=True) + EPS) * g


def _head_rms(x, g, blk_ones):
    ms = _dot(jnp.square(x).astype(BF16), blk_ones) * (1.0 / HEAD_DIM)
    return x * lax.rsqrt(ms + EPS) * g


def _lane_head_id(shape):
    return lax.broadcasted_iota(jnp.int32, shape, len(shape) - 1) // HEAD_DIM


def _blk_ones(width):
    hid = np.arange(width) // HEAD_DIM
    return jnp.asarray((hid[:, None] == hid[None, :]).astype(np.float32), dtype=BF16)


def _hgrn_level_sizes():
    sizes = []
    c = 2
    while c <= HGRN_L:
        sizes.append(c)
        c *= 2
    return sizes


def _hgrn_constants():
    L = HGRN_L
    t = np.arange(L)[:, None]
    u = np.arange(L)[None, :]
    masks = [(t == u)]
    for c in _hgrn_level_sizes():
        mid = (t // c) * c + c // 2 - 1
        same_blk = (t // c) == (u // c)
        u_mid = (u // c) * c + c // 2 - 1
        masks.append(same_blk & (t > mid) & (u <= u_mid))
    masks.append(u <= t)
    tril = (u <= t).astype(np.float32)
    pmask = np.stack([np.tile(m.astype(np.float32), (1, N_HEADS)) for m in masks])
    return jnp.asarray(tril, dtype=BF16), jnp.asarray(pmask, dtype=F32)


ROPE_PER_ROW = LANES // ROT_HALF


def _rope_kernel(pos_ref, freq_ref, sel_ref, one_ref, cos_ref, sin_ref):
    ang = pos_ref[...].astype(F32) * freq_ref[0:1, :]
    cos_ref[...] = _dot_const_right(jnp.cos(ang), sel_ref[0]) + one_ref[0:1, :]
    sin_ref[...] = _dot_const_right(jnp.sin(ang), sel_ref[1])


def _rope_tables(positions):
    n = positions.size
    per_row = ROPE_PER_ROW
    rows = n // per_row
    wide = per_row * LANES
    inv_freq = ROPE_THETA ** (-jnp.arange(0, ROT_DIM, 2, dtype=F32) / ROT_DIM)
    freq = jnp.zeros((SUBLANES, LANES), F32).at[0].set(jnp.tile(inv_freq, per_row))
    pos = jnp.repeat(positions.reshape(rows, per_row), ROT_HALF, axis=1)
    col = np.arange(wide)
    tok, lane = col // LANES, (col % LANES) % HEAD_DIM
    rotated = lane < ROT_DIM
    pick = (np.arange(LANES)[:, None] == (tok * ROT_HALF + lane % ROT_HALF)[None, :]) & rotated[None, :]
    sign = np.where(lane < ROT_HALF, -1.0, 1.0)[None, :]
    sel = jnp.asarray(np.stack([pick.astype(np.float32), pick * sign]), dtype=BF16)
    unrotated = jnp.asarray(np.broadcast_to((~rotated).astype(np.float32)[None, :], (SUBLANES, wide)))
    tm = min(ROPE_TM // per_row, rows)
    out = jax.ShapeDtypeStruct((rows, wide), F32)
    cos_w, sin_w = pl.pallas_call(
        _rope_kernel,
        grid=(rows // tm,),
        in_specs=[pl.BlockSpec((tm, LANES), lambda i: (i, 0)),
                  pl.BlockSpec((SUBLANES, LANES), lambda i: (0, 0)),
                  pl.BlockSpec(sel.shape, lambda i: (0, 0, 0)),
                  pl.BlockSpec((SUBLANES, wide), lambda i: (0, 0))],
        out_specs=[pl.BlockSpec((tm, wide), lambda i: (i, 0))] * 2,
        out_shape=[out, out],
        compiler_params=_params(1),
        name="rope_tables",
    )(pos, freq, sel, unrotated)
    return cos_w.reshape(n, LANES), sin_w.reshape(n, LANES)


def _lb_kernel(lg_ref, la_ref, l1m_ref, oml_ref):
    z = lg_ref[...]
    e = jnp.exp(z - jnp.max(z, axis=0, keepdims=True))
    sm = e / jnp.sum(e, axis=0, keepdims=True)
    depth = z.shape[0]
    run = sm[0:1, :]
    first = run
    for l in range(depth):
        if l > 0:
            run = run + sm[l:l + 1, :]
        lb = run - first
        la_ref[l:l + 1, :] = jnp.log(lb)
        l1m_ref[l:l + 1, :] = jnp.log1p(-lb)
        oml_ref[l:l + 1, :] = 1.0 - lb


def _hgrn_lower_bounds(logits):
    out = jax.ShapeDtypeStruct(logits.shape, F32)
    return pl.pallas_call(
        _lb_kernel, out_shape=[out, out, out], name="hgrn_lower_bounds",
    )(logits.astype(F32))


def _in_proj_kernel(x_ref, g_ref, w_ref, b_ref, cw_ref, cb_ref, la_ref, l1m_ref, oml_ref,
                    cos_ref, sin_ref, qg_ref, kg_ref, ones_ref,
                    ml_ref, gt_ref, hb_ref, lf_ref, sw_ref, hd_ref, xbuf, *, tiles_per_seq):
    TM = x_ref.shape[0]
    W = GROUP_W

    @pl.when(pl.program_id(0) % tiles_per_seq == 0)
    def _():
        xbuf[0:SUBLANES, :] = jnp.zeros((SUBLANES, 2 * W), F32)

    h = _rms(x_ref[...], g_ref[...]).astype(BF16)

    def proj(lo, width):
        return _dot(h, w_ref[:, lo:lo + width]) + b_ref[:, lo:lo + width]

    cos_t = cos_ref[...]
    sin_t = sin_ref[...]
    blk = ones_ref[...]
    lane = lax.broadcasted_iota(jnp.int32, (TM, LANES), 1)
    low_half = lane < HEAD_DIM
    rot_first = (lane % HEAD_DIM) < ROT_HALF
    s0 = M_COLS + H_COLS
    QW = SWA_Q_HEADS * HEAD_DIM

    def head_sumsq(y):
        width = y.shape[-1]
        return _dot(jnp.square(y).astype(BF16), blk[0:width, 0:width])

    def norm_rope(v, sumsq, gain):
        xn = v * lax.rsqrt(sumsq * (1.0 / HEAD_DIM) + EPS) * gain
        partner = jnp.where(rot_first, pltpu.roll(xn, LANES - ROT_HALF, 1), pltpu.roll(xn, ROT_HALF, 1))
        return xn * cos_t + partner * sin_t

    def swa_q(c, y, sumsq):
        for half in range(2):
            sl = slice(half * LANES, (half + 1) * LANES)
            sw_ref[:, (c + half) * LANES:(c + half + 1) * LANES] = (
                norm_rope(y[:, sl], sumsq[:, sl], qg_ref[...]) * (HEAD_DIM ** -0.5 * LOG2_E)).astype(BF16)

    def swa_kv(y, sumsq):
        k = norm_rope(y[:, 0:LANES], sumsq, kg_ref[...])
        v = y[:, LANES:2 * LANES]
        k_sw = pltpu.roll(k, HEAD_DIM, 1)
        v_sw = pltpu.roll(v, HEAD_DIM, 1)
        sw_ref[:, QW:QW + LANES] = jnp.where(low_half, k, k_sw).astype(BF16)
        sw_ref[:, QW + LANES:QW + 2 * LANES] = jnp.where(low_half, k_sw, k).astype(BF16)
        sw_ref[:, QW + 2 * LANES:QW + 3 * LANES] = jnp.where(low_half, v, v_sw).astype(BF16)
        sw_ref[:, QW + 3 * LANES:QW + 4 * LANES] = jnp.where(low_half, v_sw, v).astype(BF16)

    def mlstm_qk(y):
        xbuf[SUBLANES:SUBLANES + TM, :] = y
        acc = cb_ref[...] + cw_ref[CONV_K - 1:CONV_K, :] * xbuf[SUBLANES:SUBLANES + TM, :]
        for j in range(1, CONV_K):
            acc = acc + cw_ref[CONV_K - 1 - j:CONV_K - j, :] * xbuf[SUBLANES - j:SUBLANES - j + TM, :]
        xbuf[0:SUBLANES, :] = xbuf[TM:TM + SUBLANES, :]
        qk = acc * _sigmoid(acc)
        ml_ref[:, 0:W] = qk[:, 0:W].astype(BF16)
        ml_ref[:, W:2 * W] = (qk[:, W:2 * W] * (HEAD_DIM ** -0.5)).astype(BF16)

    def hgrn_qf(y):
        hq = y[:, 0:W]
        hf = y[:, W:2 * W]
        la = la_ref[...]
        u = jnp.exp(-jnp.abs(hf))
        t = l1m_ref[...] + (jnp.minimum(hf, 0.0) - jnp.log1p(u))
        lf = jnp.maximum(la, t) + jnp.log1p(jnp.exp(-jnp.abs(la - t)))
        lf_ref[...] = lf
        half = HGRN_L // 2
        decay = jnp.sum((lf * (-LOG2_E)).reshape(TM // half, half, W), axis=1)
        hd_ref[...] = jnp.broadcast_to(jnp.max(decay, keepdims=True), hd_ref.shape)
        hb_ref[:, 0:W] = (hq * _sigmoid(hq)).astype(BF16)
        hb_ref[:, W:2 * W] = (oml_ref[...] * jnp.where(hf >= 0.0, u, 1.0) / (1.0 + u)).astype(BF16)

    y_h = proj(M_COLS, 2 * W)
    y_m = proj(0, 2 * W)
    hgrn_qf(y_h)
    y_kv = proj(s0 + QW, 2 * LANES)
    mlstm_qk(y_m)
    y_q0 = proj(s0, 2 * LANES)
    swa_kv(y_kv, head_sumsq(y_kv[:, 0:LANES]))
    y_q1 = proj(s0 + 2 * LANES, 2 * LANES)
    swa_q(0, y_q0, head_sumsq(y_q0))
    y_h2 = proj(M_COLS + 2 * W, 2 * W)
    swa_q(2, y_q1, head_sumsq(y_q1))
    y_m2 = proj(2 * W, 2 * W)
    hg = y_h2[:, W:2 * W]
    hb_ref[:, 2 * W:3 * W] = y_h2[:, 0:W].astype(BF16)
    hb_ref[:, 3 * W:4 * W] = (hg * _sigmoid(hg)).astype(BF16)
    gt_ref[...] = proj(GATE_OFF, LANES)
    ml_ref[:, 2 * W:3 * W] = y_m2[:, 0:W].astype(BF16)
    ml_ref[:, 3 * W:4 * W] = _sigmoid(y_m2[:, W:2 * W]).astype(BF16)


def _in_proj(x, g, w, b, conv_w, conv_b, la, l1m, oml, cos_t, sin_t, q_gain, k_gain, blk_ones,
             layer, seq):
    n = x.shape[0]
    tm = min(IN_TM, n)
    row = lambda i: (i, 0)
    lay = lambda i: (layer, 0, 0)
    W4 = 4 * GROUP_W
    return pl.pallas_call(
        functools.partial(_in_proj_kernel, tiles_per_seq=seq // tm),
        grid=(n // tm,),
        in_specs=[pl.BlockSpec((tm, D_MODEL), row),
                  pl.BlockSpec((None, 1, D_MODEL), lay),
                  pl.BlockSpec((None, D_MODEL, IN_COLS), lay),
                  pl.BlockSpec((None, 1, IN_COLS), lay),
                  pl.BlockSpec((None, CONV_K, 2 * GROUP_W), lay),
                  pl.BlockSpec((None, 1, 2 * GROUP_W), lay),
                  pl.BlockSpec((None, 1, GROUP_W), lay),
                  pl.BlockSpec((None, 1, GROUP_W), lay),
                  pl.BlockSpec((None, 1, GROUP_W), lay),
                  pl.BlockSpec((tm, LANES), row),
                  pl.BlockSpec((tm, LANES), row),
                  pl.BlockSpec((None, 1, LANES), lay),
                  pl.BlockSpec((None, 1, LANES), lay),
                  pl.BlockSpec((GROUP_W, GROUP_W), lambda i: (0, 0))],
        out_specs=[pl.BlockSpec((tm, W4), row),
                   pl.BlockSpec((tm, LANES), row),
                   pl.BlockSpec((tm, W4), row),
                   pl.BlockSpec((tm, GROUP_W), row),
                   pl.BlockSpec((tm, W4), row),
                   pl.BlockSpec((SUBLANES, LANES), row)],
        out_shape=[jax.ShapeDtypeStruct((n, W4), BF16),
                   jax.ShapeDtypeStruct((n, LANES), F32),
                   jax.ShapeDtypeStruct((n, W4), BF16),
                   jax.ShapeDtypeStruct((n, GROUP_W), F32),
                   jax.ShapeDtypeStruct((n, W4), BF16),
                   jax.ShapeDtypeStruct((n // tm * SUBLANES, LANES), F32)],
        scratch_shapes=[pltpu.VMEM((tm + SUBLANES, 2 * GROUP_W), F32)],
        compiler_params=_params(1),
        name="in_proj",
    )(x, g, w, b, conv_w, conv_b, la, l1m, oml, cos_t, sin_t, q_gain, k_gain, blk_ones)


def _mlstm_chunk(r0, y_ref, gt_ref, fb_ref, g_ref, ones_ref, onesf_ref, tril_ref, exp_ref, o_ref,
                 c_st, n_st, m_st):
    T = MLSTM_T
    W = GROUP_W
    qb = y_ref[r0:r0 + T, 0:W]
    kb = y_ref[r0:r0 + T, W:2 * W]
    vb = y_ref[r0:r0 + T, 2 * W:3 * W]
    out_gate = y_ref[r0:r0 + T, 3 * W:4 * W]

    gates = gt_ref[r0:r0 + T, :]
    i_pre = gates
    lf = pltpu.roll(_log_sigmoid(gates + fb_ref[...]), LANES - N_HEADS, 1)
    lf_hi = lf.astype(BF16)
    lf_lo = (lf - lf_hi.astype(F32)).astype(BF16)
    bb = _dot(tril_ref[...], jnp.concatenate([lf_hi, lf_lo], axis=1))
    b = bb[:, 0:LANES] + bb[:, LANES:2 * LANES]
    a = i_pre - b
    row = lax.broadcasted_iota(jnp.int32, (T, LANES), 0)
    cm = a
    sh = 1
    while sh < T:
        cm = jnp.maximum(cm, jnp.where(row >= sh, pltpu.roll(cm, sh, 0), -jnp.inf))
        sh *= 2
    m_prev = m_st[0:1, :]
    m_j = b + jnp.maximum(m_prev, cm)
    m_new = m_j[T - 1:T, :]
    b_last = b[T - 1:T, :]
    w_inter = jnp.exp(b + m_prev - m_j)
    e_den = jnp.exp(-m_j)
    w_s = jnp.exp(b_last + a - m_new)
    decay = jnp.exp(b_last + m_prev - m_new)
    cexp = (b - m_j) * LOG2_E
    a_t = (a * LOG2_E).T

    stacked = jnp.concatenate([w_inter, e_den, w_s], axis=0)
    head_lane = lax.broadcasted_iota(jnp.int32, stacked.shape, 1) < N_HEADS
    ex = _dot(jnp.where(head_lane, stacked, 0.0).astype(BF16), exp_ref[...])
    w_inter_x = ex[0:T]
    e_den_x = ex[T:2 * T]
    w_s_x = ex[2 * T:3 * T]
    decay8 = jnp.where(head_lane[0:SUBLANES], jnp.broadcast_to(decay, (SUBLANES, LANES)), 0.0)
    decay_x = _dot_const_right(decay8, exp_ref[...])[0:1]

    blk = ones_ref[...]
    hid = _lane_head_id((T, W))
    causal = (lax.broadcasted_iota(jnp.int32, (T, T), 1)
              <= lax.broadcasted_iota(jnp.int32, (T, T), 0))
    head_rows = [blk[h * HEAD_DIM:h * HEAD_DIM + 1, :] for h in range(N_HEADS)]
    s_all = _dot_nt(jnp.concatenate([qb * hr for hr in head_rows], axis=0), kb)
    num = None
    den = None
    for h in range(N_HEADS):
        dm = cexp[:, h:h + 1] + a_t[h:h + 1, :]
        s = s_all[h * T:(h + 1) * T] * jnp.exp2(jnp.where(causal, dm, -jnp.inf))
        dn = _dot(s.astype(BF16), vb * head_rows[h])
        dd = jnp.where(hid == h, jnp.sum(s, axis=-1, keepdims=True), 0.0)
        num = dn if num is None else num + dn
        den = dd if den is None else den + dd
    n_row = n_st[0:1, :].astype(BF16)
    num = num + w_inter_x * _dot(qb, c_st[...].astype(BF16))
    den = den + w_inter_x * _dot(qb * n_row, blk)
    hh = num / jnp.maximum(jnp.abs(den), e_den_x)

    o_ref[r0:r0 + T, :] = _head_rms(hh, g_ref[...], blk) * out_gate.astype(F32)

    kw = kb * w_s_x.astype(BF16)
    c_st[...] = decay_x * c_st[...] + onesf_ref[...] * _dot_tn(kw, vb)
    n_st[...] = decay_x * n_st[...] + _dot(jnp.ones((SUBLANES, T), BF16), kw)
    m_st[...] = jnp.broadcast_to(m_new, m_st.shape)


def _hgrn_chunk(r0, y_ref, lf_ref, g_ref, ones_ref, onesf_ref, mst_ref, pm_ref, o_ref, s_st, bounded):
    L = HGRN_L
    W = GROUP_W
    gain = g_ref[...]
    blk = ones_ref[...]
    blk_f = onesf_ref[...]
    tril = mst_ref[...]
    row = lax.broadcasted_iota(jnp.int32, (L, W), 0)

    def level_gap(g, lf2, c):
        if c == 2:
            return jnp.where(row % 2 == 1, lf2, 0.0)
        if c >= SUBLANES:
            mids = [b * c + c // 2 - 1 for b in range(L // c)]
            return g - jnp.concatenate(
                [jnp.broadcast_to(g[m:m + 1, :], (c, W)) for m in mids], axis=0)
        lo = jnp.concatenate([jnp.broadcast_to(g[m:m + 1, :], (SUBLANES, W))
                              for m in range(1, L, SUBLANES)], axis=0)
        hi = jnp.concatenate([jnp.broadcast_to(g[m:m + 1, :], (SUBLANES, W))
                              for m in range(5, L, SUBLANES)], axis=0)
        return g - jnp.where(row % SUBLANES < 4, lo, hi)

    def head_stack(xb):
        return jnp.concatenate([xb] * N_HEADS, axis=0) * blk

    qf = y_ref[r0:r0 + L, 0:W]
    key = y_ref[r0:r0 + L, W:2 * W]
    vb = y_ref[r0:r0 + L, 2 * W:3 * W]
    out_gate = y_ref[r0:r0 + L, 3 * W:4 * W]

    lf2 = lf_ref[r0:r0 + L, :] * LOG2_E
    g = _dot_const_left(tril, lf2)
    g_last = g[L - 1:L, :]
    e_last = jnp.exp2(g_last)

    if bounded:
        g_mid = g[L // 2 - 1:L // 2, :]
        xq = qf * jnp.exp2(g - g_mid).astype(BF16)
        xk = key * jnp.exp2(g_mid - g).astype(BF16)
        a = pm_ref[pm_ref.shape[0] - 1] * _dot_nt(xq, head_stack(xk))
        q_hat = xq * jnp.exp2(g_mid).astype(BF16)
        k_hat = xk * jnp.exp2(g_last - g_mid).astype(BF16)
    else:
        q_hat = qf * jnp.exp2(g).astype(BF16)
        k_hat = key * jnp.exp2(g_last - g).astype(BF16)
        a = pm_ref[0] * _dot_nt(qf, head_stack(key))
        for i, c in enumerate(_hgrn_level_sizes()):
            e_i = jnp.exp2(-jnp.abs(level_gap(g, lf2, c))).astype(BF16)
            a = a + pm_ref[1 + i] * _dot_nt(qf * e_i, head_stack(key * e_i))
    o = _dot(a.astype(BF16), head_stack(vb)) + _dot_nt(q_hat, s_st[...].astype(BF16))

    o_ref[r0:r0 + L, :] = _head_rms(o, gain, blk) * out_gate.astype(F32)
    s_st[...] = s_st[...] * e_last + blk_f * _dot_tn(vb, k_hat)


def _swa_blocks(sink_ref, y_ref, o_ref, k_prev, v_prev, layer):
    T = SWA_T
    Wn = WINDOW
    QW = SWA_Q_HEADS * HEAD_DIM
    step = pl.program_id(1)

    k2 = [jnp.concatenate([k_prev[g], y_ref[:, QW + g * LANES:QW + (g + 1) * LANES]], axis=0)
          for g in range(SWA_KV_HEADS)]
    v2 = [jnp.concatenate([v_prev[g], y_ref[:, QW + (2 + g) * LANES:QW + (3 + g) * LANES]], axis=0)
          for g in range(SWA_KV_HEADS)]
    for g in range(SWA_KV_HEADS):
        k_prev[g] = k2[g][T:T + Wn]
        v_prev[g] = v2[g][T:T + Wn]

    qi = lax.broadcasted_iota(jnp.int32, (Wn, 2 * Wn), 0)
    ki = lax.broadcasted_iota(jnp.int32, (Wn, 2 * Wn), 1)
    band = (ki > qi) & (ki <= qi + Wn)
    band_first = band & (ki >= jnp.where(step > 0, 0, Wn))
    lane_w = lax.broadcasted_iota(jnp.int32, (Wn, LANES), 1) < HEAD_DIM
    lane_row = lax.broadcasted_iota(jnp.int32, (1, LANES), 1) < HEAD_DIM
    low_row = jnp.where(lane_row, 1.0, 0.0).astype(BF16)
    high_row = jnp.where(lane_row, 0.0, 1.0).astype(BF16)

    def scores(j, g):
        r0 = j * Wn
        kc = k2[g][r0:r0 + 2 * Wn]
        parts = []
        for c in (2 * g, 2 * g + 1):
            qc = y_ref[r0:r0 + Wn, c * LANES:(c + 1) * LANES]
            parts.append(qc * low_row)
            parts.append(qc * high_row)
        return _dot_nt(jnp.concatenate(parts, axis=0), kc)

    def finish(j, g, s):
        r0 = j * Wn
        mask = band_first if j == 0 else band
        vc = v2[g][r0:r0 + 2 * Wn]
        outs = []
        for r in range(SWA_GROUP):
            sink = sink_ref[layer, SWA_GROUP * g + r] * LOG2_E
            sr = jnp.where(mask, s[r * Wn:(r + 1) * Wn], -jnp.inf)
            m = jnp.maximum(jnp.max(sr, axis=-1, keepdims=True), sink)
            pexp = jnp.exp2(sr - m)
            den = jnp.sum(pexp, axis=-1, keepdims=True) + jnp.exp2(sink - m)
            outs.append(_dot(pexp.astype(BF16), vc) / den)
        c0 = 2 * g * LANES
        o_ref[r0:r0 + Wn, c0:c0 + LANES] = jnp.where(lane_w, outs[0], outs[1])
        o_ref[r0:r0 + Wn, c0 + LANES:c0 + 2 * LANES] = jnp.where(lane_w, outs[2], outs[3])

    return [(functools.partial(scores, j, g), functools.partial(finish, j, g))
            for j in range(T // Wn) for g in range(SWA_KV_HEADS)]


def _mixers_kernel(sink_ref, ml_ref, gt_ref, fb_ref, gm_ref, hb_ref, lf_ref, gh_ref, sw_ref,
                   ones_ref, onesf_ref, tril_ref, exp_ref, tril64_ref, pm_ref,
                   om_ref, oh_ref, os_ref,
                   c_st, n_st, m_st, s_st, k_prev, v_prev, *, layer, bounded):
    @pl.when(pl.program_id(1) == 0)
    def _():
        for ref in (c_st, n_st, m_st, s_st, k_prev, v_prev):
            ref[...] = jnp.zeros_like(ref)

    swa_blocks = _swa_blocks(sink_ref, sw_ref, os_ref, k_prev, v_prev, layer)
    n_sub = MIX_ROWS // HGRN_L
    assert len(swa_blocks) == n_sub and n_sub % (MIX_ROWS // MLSTM_T) == 0
    per_mlstm = n_sub // (MIX_ROWS // MLSTM_T)
    for sub in range(n_sub):
        swa_scores, swa_finish = swa_blocks[sub]
        s = swa_scores()
        _hgrn_chunk(sub * HGRN_L, hb_ref, lf_ref, gh_ref, ones_ref, onesf_ref, tril64_ref, pm_ref, oh_ref,
                    s_st, bounded)
        swa_finish(s)
        if sub % per_mlstm == 0:
            _mlstm_chunk((sub // per_mlstm) * MLSTM_T, ml_ref, gt_ref, fb_ref, gm_ref, ones_ref, onesf_ref,
                         tril_ref, exp_ref, om_ref, c_st, n_st, m_st)


def _mixers(ml, gt, hb, lf, sw, f_bias, mlstm_g, hgrn_g, sinks, layer, batch, consts, bounded):
    n = ml.shape[0]
    seq = n // batch
    T = MIX_ROWS
    steps = seq // T
    W4 = 4 * GROUP_W
    QW = SWA_Q_HEADS * HEAD_DIM
    row = lambda bi, ci: (bi * steps + ci, 0)
    lay = lambda bi, ci: (layer, 0, 0)
    cst = lambda bi, ci: (0, 0)
    blk_ones, tril, expand, tril64, pmask = consts
    return pl.pallas_call(
        functools.partial(_mixers_kernel, layer=layer, bounded=bounded),
        grid=(batch, steps),
        in_specs=[pl.BlockSpec(memory_space=pltpu.SMEM),
                  pl.BlockSpec((T, W4), row),
                  pl.BlockSpec((T, LANES), row),
                  pl.BlockSpec((None, 1, LANES), lay),
                  pl.BlockSpec((None, 1, GROUP_W), lay),
                  pl.BlockSpec((T, W4), row),
                  pl.BlockSpec((T, GROUP_W), row),
                  pl.BlockSpec((None, 1, GROUP_W), lay),
                  pl.BlockSpec((T, W4), row),
                  pl.BlockSpec((GROUP_W, GROUP_W), cst),
                  pl.BlockSpec((GROUP_W, GROUP_W), cst),
                  pl.BlockSpec((MLSTM_T, MLSTM_T), cst),
                  pl.BlockSpec((LANES, GROUP_W), cst),
                  pl.BlockSpec((HGRN_L, HGRN_L), cst),
                  pl.BlockSpec(pmask.shape, lambda bi, ci: (0, 0, 0))],
        out_specs=[pl.BlockSpec((T, GROUP_W), row),
                   pl.BlockSpec((T, GROUP_W), row),
                   pl.BlockSpec((T, QW), row)],
        out_shape=[jax.ShapeDtypeStruct((n, GROUP_W), F32),
                   jax.ShapeDtypeStruct((n, GROUP_W), F32),
                   jax.ShapeDtypeStruct((n, QW), F32)],
        scratch_shapes=[pltpu.VMEM((GROUP_W, GROUP_W), F32),
                        pltpu.VMEM((SUBLANES, GROUP_W), F32),
                        pltpu.VMEM((SUBLANES, LANES), F32),
                        pltpu.VMEM((GROUP_W, GROUP_W), F32),
                        pltpu.VMEM((SWA_KV_HEADS, WINDOW, LANES), BF16),
                        pltpu.VMEM((SWA_KV_HEADS, WINDOW, LANES), BF16)],
        compiler_params=_params(2),
        name="mixers",
    )(sinks, ml, gt, f_bias, mlstm_g, hb, lf, hgrn_g, sw, blk_ones, blk_ones.astype(F32), tril, expand,
      tril64, pmask)


def _tail_kernel(x_ref, mm_ref, mh_ref, ms_ref, p_ref, wo_ref, g2_ref, wu_ref, wd_ref, g3_ref,
                 wg_ref, wp_ref, g4_ref, o_ref):
    W = GROUP_W
    mix = _dot(mm_ref[...].astype(BF16), wo_ref[0:W, :])
    mix = mix + _dot(mh_ref[...].astype(BF16), wo_ref[W:2 * W, :])
    mix = mix + _dot(ms_ref[...].astype(BF16), wo_ref[2 * W:4 * W, :])
    x = x_ref[...] + mix
    emb = _rms(_dot(p_ref[...].astype(BF16), wp_ref[...]), g4_ref[...])

    h2 = _rms(x, g2_ref[...]).astype(BF16)
    mlp = None
    for c in range(D_FF // FF_CHUNK):
        u = _dot(h2, wu_ref[:, c * FF_CHUNK:(c + 1) * FF_CHUNK])
        act = jnp.square(jnp.maximum(u, 0.0)).astype(BF16)
        d = _dot(act, wd_ref[c * FF_CHUNK:(c + 1) * FF_CHUNK, :])
        mlp = d if mlp is None else mlp + d
    x = x + mlp

    h3 = _rms(x, g3_ref[...]).astype(BF16)
    for c in range(D_MODEL // GATE_CHUNK):
        cols = slice(c * GATE_CHUNK, (c + 1) * GATE_CHUNK)
        gate = _sigmoid(_dot(h3, wg_ref[:, cols]))
        o_ref[:, cols] = x[:, cols] + gate * emb[:, cols]


def _tail(x, mm, mh, ms, p, w_out, g2, w_up, w_down, g3, w_gate, w_proj, g4, layer):
    n = x.shape[0]
    tm = min(TAIL_TM, n)
    tiles = n // tm
    row = lambda i: (i, 0)
    lay = lambda i: (layer, 0, 0)
    once = pl.Buffered(1)

    def wspec(shape):
        return pl.BlockSpec((None,) + shape, lay, pipeline_mode=once)

    return pl.pallas_call(
        _tail_kernel,
        grid=(tiles,),
        in_specs=[pl.BlockSpec((tm, D_MODEL), row),
                  pl.BlockSpec((tm, GROUP_W), row),
                  pl.BlockSpec((tm, GROUP_W), row),
                  pl.BlockSpec((tm, 2 * GROUP_W), row),
                  pl.BlockSpec((None, tm, PLE_DIM), lambda i: (layer, i, 0)),
                  wspec((D_MODEL, D_MODEL)),
                  wspec((1, D_MODEL)),
                  wspec((D_MODEL, D_FF)),
                  wspec((D_FF, D_MODEL)),
                  wspec((1, D_MODEL)),
                  wspec((D_MODEL, D_MODEL)),
                  wspec((PLE_DIM, D_MODEL)),
                  wspec((1, D_MODEL))],
        out_specs=pl.BlockSpec((tm, D_MODEL), row),
        out_shape=jax.ShapeDtypeStruct((n, D_MODEL), F32),
        compiler_params=_params(1),
        name="tail",
    )(x, mm, mh, ms, p, w_out, g2, w_up, w_down, g3, w_gate, w_proj, g4)


def kernel(x, p, positions, in_norm_g, w_in, b_in, mlstm_f_bias, mlstm_conv_w, mlstm_conv_b,
           mlstm_norm_g, hgrn_lb_logits, hgrn_norm_g, swa_q_norm_g, swa_k_norm_g, swa_sinks,
           w_out, mlp_norm_g, w_up, w_down, ple_norm_g, w_ple_gate, w_ple_proj, ple_post_norm_g):
    batch, seq, d_model = x.shape
    depth = w_in.shape[0]
    n = batch * seq
    assert d_model == D_MODEL and seq % max(MIX_ROWS, IN_TM) == 0

    n_raw = GATE_OFF + GATE_COLS
    pad = LANES - GATE_COLS
    w_in_p = jnp.concatenate(
        [w_in[..., :n_raw], jnp.zeros((depth, D_MODEL, pad), w_in.dtype), w_in[..., n_raw:]],
        axis=-1).astype(BF16)
    b_in_p = jnp.concatenate(
        [b_in[..., :n_raw], jnp.zeros((depth, pad), b_in.dtype), b_in[..., n_raw:]],
        axis=-1).astype(F32).reshape(depth, 1, IN_COLS)
    f_bias = jnp.zeros((depth, 1, LANES), F32).at[:, 0, N_HEADS:GATE_COLS].set(mlstm_f_bias.astype(F32))
    row3 = lambda a: a.astype(F32).reshape(depth, 1, a.shape[-1])
    q_gain = row3(jnp.tile(swa_q_norm_g, (1, LANES // HEAD_DIM)))
    k_gain = row3(jnp.tile(swa_k_norm_g, (1, LANES // HEAD_DIM)))
    w_out_b, w_up_b, w_down_b = w_out.astype(BF16), w_up.astype(BF16), w_down.astype(BF16)
    w_gate_b, w_proj_b = w_ple_gate.astype(BF16), w_ple_proj.astype(BF16)

    blk256 = _blk_ones(GROUP_W)
    tril = jnp.asarray(np.tril(np.ones((MLSTM_T, MLSTM_T), np.float32)), dtype=BF16)
    expand = jnp.asarray(
        (np.arange(LANES)[:, None] == (np.arange(GROUP_W) // HEAD_DIM)[None, :]).astype(np.float32),
        dtype=BF16)
    hgrn_tril, hgrn_pmask = _hgrn_constants()

    cos_t, sin_t = _rope_tables(positions)
    la, l1m, oml = [row3(a) for a in _hgrn_lower_bounds(hgrn_lb_logits)]
    worst_log2 = (HGRN_L // 2) * LOG2_E * jnp.min(la, axis=(1, 2))
    hgrn_bounded = worst_log2 > -HGRN_SAFE_LOG2
    sinks = swa_sinks.astype(F32)

    xf = x.reshape(n, D_MODEL)
    for l in range(depth):
        ml, gt, hb, lf, sw, half_chunk_decay = _in_proj(
            xf, row3(in_norm_g), w_in_p, b_in_p, mlstm_conv_w.astype(F32), row3(mlstm_conv_b),
            la, l1m, oml, cos_t, sin_t, q_gain, k_gain, blk256, l, seq)
        mix_consts = (blk256, tril, expand, hgrn_tril, hgrn_pmask)
        mm, mh, ms = lax.cond(
            hgrn_bounded[l] | (jnp.max(half_chunk_decay) <= HGRN_SAFE_LOG2),
            lambda *a: _mixers(*a, l, batch, mix_consts, True),
            lambda *a: _mixers(*a, l, batch, mix_consts, False),
            ml, gt, hb, lf, sw, f_bias, row3(mlstm_norm_g), row3(hgrn_norm_g), sinks)
        xf = _tail(xf, mm, mh, ms, p.reshape(depth, n, PLE_DIM), w_out_b, row3(mlp_norm_g), w_up_b,
                   w_down_b, row3(ple_norm_g), w_gate_b, w_proj_b, row3(ple_post_norm_g), l)
    return xf.reshape(batch, seq, D_MODEL)
```

```python
import functools

import numpy as np
import jax
import jax.numpy as jnp
from jax import lax
from jax.experimental import pallas as pl
from jax.experimental.pallas import tpu as pltpu

F32 = jnp.float32
BF16 = jnp.bfloat16

D_MODEL = 1024
N_HEADS = 4
HEAD_DIM = 64
GROUP_W = N_HEADS * HEAD_DIM
CONV_K = 4
SWA_Q_HEADS = 8
SWA_KV_HEADS = 2
SWA_GROUP = SWA_Q_HEADS // SWA_KV_HEADS
WINDOW = 128
ROPE_THETA = 500000.0
ROT_DIM = HEAD_DIM // 4
ROT_HALF = ROT_DIM // 2
D_FF = 4 * D_MODEL
PLE_DIM = 256
EPS = 1e-6
LOG2_E = 1.4426950408889634

LANES = 128
SUBLANES = 8
VMEM_LIMIT_BYTES = 56 * 1024 * 1024

GATE_COLS = 2 * N_HEADS
M_COLS = 4 * GROUP_W + LANES
H_COLS = 4 * GROUP_W
S_COLS = SWA_Q_HEADS * HEAD_DIM + 2 * SWA_KV_HEADS * HEAD_DIM
IN_COLS = M_COLS + H_COLS + S_COLS
GATE_OFF = 4 * GROUP_W

IN_TM = 1024
TAIL_TM = 512
FF_CHUNK = 1024
GATE_CHUNK = 256
ROPE_TM = 2048
MIX_ROWS = 1024
MLSTM_T = 256
HGRN_L = 64
HGRN_SAFE_LOG2 = 100.0
SWA_T = MIX_ROWS

N_SPLIT = 2
assert HGRN_L == HEAD_DIM


def _params(n_grid_dims):
    return pltpu.CompilerParams(
        dimension_semantics=("arbitrary",) * n_grid_dims,
        vmem_limit_bytes=VMEM_LIMIT_BYTES)


def _dot(a, b):
    return jnp.dot(a, b, preferred_element_type=F32)


def _dot_nt(a, b):
    return lax.dot_general(a, b, (((1,), (1,)), ((), ())), preferred_element_type=F32)


def _dot_tn(a, b):
    return lax.dot_general(a, b, (((0,), (0,)), ((), ())), preferred_element_type=F32)


def _split(x, n=N_SPLIT):
    pieces = []
    r = x
    for _ in range(n):
        p = r.astype(BF16)
        pieces.append(p)
        r = r - p.astype(F32)
    return pieces


def _dot_const_left(m, x):
    acc = None
    for p in _split(x):
        t = _dot(m, p)
        acc = t if acc is None else acc + t
    return acc


def _dot_const_right(x, m):
    acc = None
    for p in _split(x):
        t = _dot(p, m)
        acc = t if acc is None else acc + t
    return acc


def _sigmoid(x):
    return 1.0 / (1.0 + jnp.exp(-x))


def _log_sigmoid(x):
    return jnp.minimum(x, 0.0) - jnp.log1p(jnp.exp(-jnp.abs(x)))


def _rms(x, g):
    return x * lax.rsqrt(jnp.mean(x * x, axis=-1, keepdims=True) + EPS) * g


def _head_rms(x, g, blk_ones):
    ms = _dot(jnp.square(x).astype(BF16), blk_ones) * (1.0 / HEAD_DIM)
    return x * lax.rsqrt(ms + EPS) * g


def _lane_head_id(shape):
    return lax.broadcasted_iota(jnp.int32, shape, len(shape) - 1) // HEAD_DIM


def _blk_ones(width):
    hid = np.arange(width) // HEAD_DIM
    return jnp.asarray((hid[:, None] == hid[None, :]).astype(np.float32), dtype=BF16)


def _hgrn_level_sizes():
    sizes = []
    c = 2
    while c <= HGRN_L:
        sizes.append(c)
        c *= 2
    return sizes


def _hgrn_constants():
    L = HGRN_L
    t = np.arange(L)[:, None]
    u = np.arange(L)[None, :]
    masks = [(t == u)]
    for c in _hgrn_level_sizes():
        mid = (t // c) * c + c // 2 - 1
        same_blk = (t // c) == (u // c)
        u_mid = (u // c) * c + c // 2 - 1
        masks.append(same_blk & (t > mid) & (u <= u_mid))
    masks.append(u <= t)
    tril = (u <= t).astype(np.float32)
    pmask = np.stack([np.tile(m.astype(np.float32), (1, N_HEADS)) for m in masks])
    return jnp.asarray(tril, dtype=BF16), jnp.asarray(pmask, dtype=F32)


ROPE_PER_ROW = LANES // ROT_HALF


def _rope_kernel(pos_ref, freq_ref, sel_ref, one_ref, cos_ref, sin_ref):
    ang = pos_ref[...].astype(F32) * freq_ref[0:1, :]
    cos_ref[...] = _dot_const_right(jnp.cos(ang), sel_ref[0]) + one_ref[0:1, :]
    sin_ref[...] = _dot_const_right(jnp.sin(ang), sel_ref[1])


def _rope_tables(positions):
    n = positions.size
    per_row = ROPE_PER_ROW
    rows = n // per_row
    wide = per_row * LANES
    inv_freq = ROPE_THETA ** (-jnp.arange(0, ROT_DIM, 2, dtype=F32) / ROT_DIM)
    freq = jnp.zeros((SUBLANES, LANES), F32).at[0].set(jnp.tile(inv_freq, per_row))
    pos = jnp.repeat(positions.reshape(rows, per_row), ROT_HALF, axis=1)
    col = np.arange(wide)
    tok, lane = col // LANES, (col % LANES) % HEAD_DIM
    rotated = lane < ROT_DIM
    pick = (np.arange(LANES)[:, None] == (tok * ROT_HALF + lane % ROT_HALF)[None, :]) & rotated[None, :]
    sign = np.where(lane < ROT_HALF, -1.0, 1.0)[None, :]
    sel = jnp.asarray(np.stack([pick.astype(np.float32), pick * sign]), dtype=BF16)
    unrotated = jnp.asarray(np.broadcast_to((~rotated).astype(np.float32)[None, :], (SUBLANES, wide)))
    tm = min(ROPE_TM // per_row, rows)
    out = jax.ShapeDtypeStruct((rows, wide), F32)
    cos_w, sin_w = pl.pallas_call(
        _rope_kernel,
        grid=(rows // tm,),
        in_specs=[pl.BlockSpec((tm, LANES), lambda i: (i, 0)),
                  pl.BlockSpec((SUBLANES, LANES), lambda i: (0, 0)),
                  pl.BlockSpec(sel.shape, lambda i: (0, 0, 0)),
                  pl.BlockSpec((SUBLANES, wide), lambda i: (0, 0))],
        out_specs=[pl.BlockSpec((tm, wide), lambda i: (i, 0))] * 2,
        out_shape=[out, out],
        compiler_params=_params(1),
        name="rope_tables",
    )(pos, freq, sel, unrotated)
    return cos_w.reshape(n, LANES), sin_w.reshape(n, LANES)


def _lb_kernel(lg_ref, la_ref, l1m_ref, oml_ref):
    z = lg_ref[...]
    e = jnp.exp(z - jnp.max(z, axis=0, keepdims=True))
    sm = e / jnp.sum(e, axis=0, keepdims=True)
    depth = z.shape[0]
    run = sm[0:1, :]
    first = run
    for l in range(depth):
        if l > 0:
            run = run + sm[l:l + 1, :]
        lb = run - first
        la_ref[l:l + 1, :] = jnp.log(lb)
        l1m_ref[l:l + 1, :] = jnp.log1p(-lb)
        oml_ref[l:l + 1, :] = 1.0 - lb


def _hgrn_lower_bounds(logits):
    out = jax.ShapeDtypeStruct(logits.shape, F32)
    return pl.pallas_call(
        _lb_kernel, out_shape=[out, out, out], name="hgrn_lower_bounds",
    )(logits.astype(F32))


def _in_proj_kernel(x_ref, g_ref, w_ref, b_ref, cw_ref, cb_ref, la_ref, l1m_ref, oml_ref,
                    cos_ref, sin_ref, qg_ref, kg_ref, ones_ref,
                    ml_ref, gt_ref, hb_ref, lf_ref, sw_ref, hd_ref, xbuf, *, tiles_per_seq):
    TM = x_ref.shape[0]
    W = GROUP_W

    @pl.when(pl.program_id(0) % tiles_per_seq == 0)
    def _():
        xbuf[0:SUBLANES, :] = jnp.zeros((SUBLANES, 2 * W), F32)

    h = _rms(x_ref[...], g_ref[...]).astype(BF16)

    def proj(lo, width):
        return _dot(h, w_ref[:, lo:lo + width]) + b_ref[:, lo:lo + width]

    cos_t = cos_ref[...]
    sin_t = sin_ref[...]
    blk = ones_ref[...]
    lane = lax.broadcasted_iota(jnp.int32, (TM, LANES), 1)
    low_half = lane < HEAD_DIM
    rot_first = (lane % HEAD_DIM) < ROT_HALF
    s0 = M_COLS + H_COLS
    QW = SWA_Q_HEADS * HEAD_DIM

    def head_sumsq(y):
        width = y.shape[-1]
        return _dot(jnp.square(y).astype(BF16), blk[0:width, 0:width])

    def norm_rope(v, sumsq, gain):
        xn = v * lax.rsqrt(sumsq * (1.0 / HEAD_DIM) + EPS) * gain
        partner = jnp.where(rot_first, pltpu.roll(xn, LANES - ROT_HALF, 1), pltpu.roll(xn, ROT_HALF, 1))
        return xn * cos_t + partner * sin_t

    def swa_q(c, y, sumsq):
        for half in range(2):
            sl = slice(half * LANES, (half + 1) * LANES)
            sw_ref[:, (c + half) * LANES:(c + half + 1) * LANES] = (
                norm_rope(y[:, sl], sumsq[:, sl], qg_ref[...]) * (HEAD_DIM ** -0.5 * LOG2_E)).astype(BF16)

    def swa_kv(y, sumsq):
        k = norm_rope(y[:, 0:LANES], sumsq, kg_ref[...])
        v = y[:, LANES:2 * LANES]
        k_sw = pltpu.roll(k, HEAD_DIM, 1)
        v_sw = pltpu.roll(v, HEAD_DIM, 1)
        sw_ref[:, QW:QW + LANES] = jnp.where(low_half, k, k_sw).astype(BF16)
        sw_ref[:, QW + LANES:QW + 2 * LANES] = jnp.where(low_half, k_sw, k).astype(BF16)
        sw_ref[:, QW + 2 * LANES:QW + 3 * LANES] = jnp.where(low_half, v, v_sw).astype(BF16)
        sw_ref[:, QW + 3 * LANES:QW + 4 * LANES] = jnp.where(low_half, v_sw, v).astype(BF16)

    def mlstm_qk(y):
        xbuf[SUBLANES:SUBLANES + TM, :] = y
        acc = cb_ref[...] + cw_ref[CONV_K - 1:CONV_K, :] * xbuf[SUBLANES:SUBLANES + TM, :]
        for j in range(1, CONV_K):
            acc = acc + cw_ref[CONV_K - 1 - j:CONV_K - j, :] * xbuf[SUBLANES - j:SUBLANES - j + TM, :]
        xbuf[0:SUBLANES, :] = xbuf[TM:TM + SUBLANES, :]
        qk = acc * _sigmoid(acc)
        ml_ref[:, 0:W] = qk[:, 0:W].astype(BF16)
        ml_ref[:, W:2 * W] = (qk[:, W:2 * W] * (HEAD_DIM ** -0.5)).astype(BF16)

    def hgrn_qf(y):
        hq = y[:, 0:W]
        hf = y[:, W:2 * W]
        la = la_ref[...]
        u = jnp.exp(-jnp.abs(hf))
        t = l1m_ref[...] + (jnp.minimum(hf, 0.0) - jnp.log1p(u))
        lf = jnp.maximum(la, t) + jnp.log1p(jnp.exp(-jnp.abs(la - t)))
        lf_ref[...] = lf
        half = HGRN_L // 2
        decay = jnp.sum((lf * (-LOG2_E)).reshape(TM // half, half, W), axis=1)
        hd_ref[...] = jnp.broadcast_to(jnp.max(decay, keepdims=True), hd_ref.shape)
        hb_ref[:, 0:W] = (hq * _sigmoid(hq)).astype(BF16)
        hb_ref[:, W:2 * W] = (oml_ref[...] * jnp.where(hf >= 0.0, u, 1.0) / (1.0 + u)).astype(BF16)

    y_h = proj(M_COLS, 2 * W)
    y_m = proj(0, 2 * W)
    hgrn_qf(y_h)
    y_kv = proj(s0 + QW, 2 * LANES)
    mlstm_qk(y_m)
    y_q0 = proj(s0, 2 * LANES)
    swa_kv(y_kv, head_sumsq(y_kv[:, 0:LANES]))
    y_q1 = proj(s0 + 2 * LANES, 2 * LANES)
    swa_q(0, y_q0, head_sumsq(y_q0))
    y_h2 = proj(M_COLS + 2 * W, 2 * W)
    swa_q(2, y_q1, head_sumsq(y_q1))
    y_m2 = proj(2 * W, 2 * W)
    hg = y_h2[:, W:2 * W]
    hb_ref[:, 2 * W:3 * W] = y_h2[:, 0:W].astype(BF16)
    hb_ref[:, 3 * W:4 * W] = (hg * _sigmoid(hg)).astype(BF16)
    gt_ref[...] = proj(GATE_OFF, LANES)
    ml_ref[:, 2 * W:3 * W] = y_m2[:, 0:W].astype(BF16)
    ml_ref[:, 3 * W:4 * W] = _sigmoid(y_m2[:, W:2 * W]).astype(BF16)


def _in_proj(x, g, w, b, conv_w, conv_b, la, l1m, oml, cos_t, sin_t, q_gain, k_gain, blk_ones,
             layer, seq):
    n = x.shape[0]
    tm = min(IN_TM, n)
    row = lambda i: (i, 0)
    lay = lambda i: (layer, 0, 0)
    W4 = 4 * GROUP_W
    return pl.pallas_call(
        functools.partial(_in_proj_kernel, tiles_per_seq=seq // tm),
        grid=(n // tm,),
        in_specs=[pl.BlockSpec((tm, D_MODEL), row),
                  pl.BlockSpec((None, 1, D_MODEL), lay),
                  pl.BlockSpec((None, D_MODEL, IN_COLS), lay),
                  pl.BlockSpec((None, 1, IN_COLS), lay),
                  pl.BlockSpec((None, CONV_K, 2 * GROUP_W), lay),
                  pl.BlockSpec((None, 1, 2 * GROUP_W), lay),
                  pl.BlockSpec((None, 1, GROUP_W), lay),
                  pl.BlockSpec((None, 1, GROUP_W), lay),
                  pl.BlockSpec((None, 1, GROUP_W), lay),
                  pl.BlockSpec((tm, LANES), row),
                  pl.BlockSpec((tm, LANES), row),
                  pl.BlockSpec((None, 1, LANES), lay),
                  pl.BlockSpec((None, 1, LANES), lay),
                  pl.BlockSpec((GROUP_W, GROUP_W), lambda i: (0, 0))],
        out_specs=[pl.BlockSpec((tm, W4), row),
                   pl.BlockSpec((tm, LANES), row),
                   pl.BlockSpec((tm, W4), row),
                   pl.BlockSpec((tm, GROUP_W), row),
                   pl.BlockSpec((tm, W4), row),
                   pl.BlockSpec((SUBLANES, LANES), row)],
        out_shape=[jax.ShapeDtypeStruct((n, W4), BF16),
                   jax.ShapeDtypeStruct((n, LANES), F32),
                   jax.ShapeDtypeStruct((n, W4), BF16),
                   jax.ShapeDtypeStruct((n, GROUP_W), F32),
                   jax.ShapeDtypeStruct((n, W4), BF16),
                   jax.ShapeDtypeStruct((n // tm * SUBLANES, LANES), F32)],
        scratch_shapes=[pltpu.VMEM((tm + SUBLANES, 2 * GROUP_W), F32)],
        compiler_params=_params(1),
        name="in_proj",
    )(x, g, w, b, conv_w, conv_b, la, l1m, oml, cos_t, sin_t, q_gain, k_gain, blk_ones)


def _mlstm_chunk(r0, y_ref, gt_ref, fb_ref, g_ref, ones_ref, onesf_ref, tril_ref, exp_ref, o_ref,
                 c_st, n_st, m_st):
    T = MLSTM_T
    W = GROUP_W
    qb = y_ref[r0:r0 + T, 0:W]
    kb = y_ref[r0:r0 + T, W:2 * W]
    vb = y_ref[r0:r0 + T, 2 * W:3 * W]
    out_gate = y_ref[r0:r0 + T, 3 * W:4 * W]

    gates = gt_ref[r0:r0 + T, :]
    i_pre = gates
    lf = pltpu.roll(_log_sigmoid(gates + fb_ref[...]), LANES - N_HEADS, 1)
    lf_hi = lf.astype(BF16)
    lf_lo = (lf - lf_hi.astype(F32)).astype(BF16)
    bb = _dot(tril_ref[...], jnp.concatenate([lf_hi, lf_lo], axis=1))
    b = bb[:, 0:LANES] + bb[:, LANES:2 * LANES]
    a = i_pre - b
    row = lax.broadcasted_iota(jnp.int32, (T, LANES), 0)
    cm = a
    sh = 1
    while sh < T:
        cm = jnp.maximum(cm, jnp.where(row >= sh, pltpu.roll(cm, sh, 0), -jnp.inf))
        sh *= 2
    m_prev = m_st[0:1, :]
    m_j = b + jnp.maximum(m_prev, cm)
    m_new = m_j[T - 1:T, :]
    b_last = b[T - 1:T, :]
    w_inter = jnp.exp(b + m_prev - m_j)
    e_den = jnp.exp(-m_j)
    w_s = jnp.exp(b_last + a - m_new)
    decay = jnp.exp(b_last + m_prev - m_new)
    cexp = (b - m_j) * LOG2_E
    a_t = (a * LOG2_E).T

    stacked = jnp.concatenate([w_inter, e_den, w_s], axis=0)
    head_lane = lax.broadcasted_iota(jnp.int32, stacked.shape, 1) < N_HEADS
    ex = _dot(jnp.where(head_lane, stacked, 0.0).astype(BF16), exp_ref[...])
    w_inter_x = ex[0:T]
    e_den_x = ex[T:2 * T]
    w_s_x = ex[2 * T:3 * T]
    decay8 = jnp.where(head_lane[0:SUBLANES], jnp.broadcast_to(decay, (SUBLANES, LANES)), 0.0)
    decay_x = _dot_const_right(decay8, exp_ref[...])[0:1]

    blk = ones_ref[...]
    hid = _lane_head_id((T, W))
    causal = (lax.broadcasted_iota(jnp.int32, (T, T), 1)
              <= lax.broadcasted_iota(jnp.int32, (T, T), 0))
    head_rows = [blk[h * HEAD_DIM:h * HEAD_DIM + 1, :] for h in range(N_HEADS)]
    s_all = _dot_nt(jnp.concatenate([qb * hr for hr in head_rows], axis=0), kb)
    num = None
    den = None
    for h in range(N_HEADS):
        dm = cexp[:, h:h + 1] + a_t[h:h + 1, :]
        s = s_all[h * T:(h + 1) * T] * jnp.exp2(jnp.where(causal, dm, -jnp.inf))
        dn = _dot(s.astype(BF16), vb * head_rows[h])
        dd = jnp.where(hid == h, jnp.sum(s, axis=-1, keepdims=True), 0.0)
        num = dn if num is None else num + dn
        den = dd if den is None else den + dd
    n_row = n_st[0:1, :].astype(BF16)
    num = num + w_inter_x * _dot(qb, c_st[...].astype(BF16))
    den = den + w_inter_x * _dot(qb * n_row, blk)
    hh = num / jnp.maximum(jnp.abs(den), e_den_x)

    o_ref[r0:r0 + T, :] = _head_rms(hh, g_ref[...], blk) * out_gate.astype(F32)

    kw = kb * w_s_x.astype(BF16)
    c_st[...] = decay_x * c_st[...] + onesf_ref[...] * _dot_tn(kw, vb)
    n_st[...] = decay_x * n_st[...] + _dot(jnp.ones((SUBLANES, T), BF16), kw)
    m_st[...] = jnp.broadcast_to(m_new, m_st.shape)


def _hgrn_chunk(r0, y_ref, lf_ref, g_ref, ones_ref, onesf_ref, mst_ref, pm_ref, o_ref, s_st, bounded):
    L = HGRN_L
    W = GROUP_W
    gain = g_ref[...]
    blk = ones_ref[...]
    blk_f = onesf_ref[...]
    tril = mst_ref[...]
    row = lax.broadcasted_iota(jnp.int32, (L, W), 0)

    def level_gap(g, lf2, c):
        if c == 2:
            return jnp.where(row % 2 == 1, lf2, 0.0)
        if c >= SUBLANES:
            mids = [b * c + c // 2 - 1 for b in range(L // c)]
            return g - jnp.concatenate(
                [jnp.broadcast_to(g[m:m + 1, :], (c, W)) for m in mids], axis=0)
        lo = jnp.concatenate([jnp.broadcast_to(g[m:m + 1, :], (SUBLANES, W))
                              for m in range(1, L, SUBLANES)], axis=0)
        hi = jnp.concatenate([jnp.broadcast_to(g[m:m + 1, :], (SUBLANES, W))
                              for m in range(5, L, SUBLANES)], axis=0)
        return g - jnp.where(row % SUBLANES < 4, lo, hi)

    def head_stack(xb):
        return jnp.concatenate([xb] * N_HEADS, axis=0) * blk

    qf = y_ref[r0:r0 + L, 0:W]
    key = y_ref[r0:r0 + L, W:2 * W]
    vb = y_ref[r0:r0 + L, 2 * W:3 * W]
    out_gate = y_ref[r0:r0 + L, 3 * W:4 * W]

    lf2 = lf_ref[r0:r0 + L, :] * LOG2_E
    g = _dot_const_left(tril, lf2)
    g_last = g[L - 1:L, :]
    e_last = jnp.exp2(g_last)

    if bounded:
        g_mid = g[L // 2 - 1:L // 2, :]
        xq = qf * jnp.exp2(g - g_mid).astype(BF16)
        xk = key * jnp.exp2(g_mid - g).astype(BF16)
        a = pm_ref[pm_ref.shape[0] - 1] * _dot_nt(xq, head_stack(xk))
        q_hat = xq * jnp.exp2(g_mid).astype(BF16)
        k_hat = xk * jnp.exp2(g_last - g_mid).astype(BF16)
    else:
        q_hat = qf * jnp.exp2(g).astype(BF16)
        k_hat = key * jnp.exp2(g_last - g).astype(BF16)
        a = pm_ref[0] * _dot_nt(qf, head_stack(key))
        for i, c in enumerate(_hgrn_level_sizes()):
            e_i = jnp.exp2(-jnp.abs(level_gap(g, lf2, c))).astype(BF16)
            a = a + pm_ref[1 + i] * _dot_nt(qf * e_i, head_stack(key * e_i))
    o = _dot(a.astype(BF16), head_stack(vb)) + _dot_nt(q_hat, s_st[...].astype(BF16))

    o_ref[r0:r0 + L, :] = _head_rms(o, gain, blk) * out_gate.astype(F32)
    s_st[...] = s_st[...] * e_last + blk_f * _dot_tn(vb, k_hat)


def _swa_blocks(sink_ref, y_ref, o_ref, k_prev, v_prev, layer):
    T = SWA_T
    Wn = WINDOW
    QW = SWA_Q_HEADS * HEAD_DIM
    step = pl.program_id(1)

    k2 = [jnp.concatenate([k_prev[g], y_ref[:, QW + g * LANES:QW + (g + 1) * LANES]], axis=0)
          for g in range(SWA_KV_HEADS)]
    v2 = [jnp.concatenate([v_prev[g], y_ref[:, QW + (2 + g) * LANES:QW + (3 + g) * LANES]], axis=0)
          for g in range(SWA_KV_HEADS)]
    for g in range(SWA_KV_HEADS):
        k_prev[g] = k2[g][T:T + Wn]
        v_prev[g] = v2[g][T:T + Wn]

    qi = lax.broadcasted_iota(jnp.int32, (Wn, 2 * Wn), 0)
    ki = lax.broadcasted_iota(jnp.int32, (Wn, 2 * Wn), 1)
    band = (ki > qi) & (ki <= qi + Wn)
    band_first = band & (ki >= jnp.where(step > 0, 0, Wn))
    lane_w = lax.broadcasted_iota(jnp.int32, (Wn, LANES), 1) < HEAD_DIM
    lane_row = lax.broadcasted_iota(jnp.int32, (1, LANES), 1) < HEAD_DIM
    low_row = jnp.where(lane_row, 1.0, 0.0).astype(BF16)
    high_row = jnp.where(lane_row, 0.0, 1.0).astype(BF16)

    def scores(j, g):
        r0 = j * Wn
        kc = k2[g][r0:r0 + 2 * Wn]
        parts = []
        for c in (2 * g, 2 * g + 1):
            qc = y_ref[r0:r0 + Wn, c * LANES:(c + 1) * LANES]
            parts.append(qc * low_row)
            parts.append(qc * high_row)
        return _dot_nt(jnp.concatenate(parts, axis=0), kc)

    def finish(j, g, s):
        r0 = j * Wn
        mask = band_first if j == 0 else band
        vc = v2[g][r0:r0 + 2 * Wn]
        outs = []
        for r in range(SWA_GROUP):
            sink = sink_ref[layer, SWA_GROUP * g + r] * LOG2_E
            sr = jnp.where(mask, s[r * Wn:(r + 1) * Wn], -jnp.inf)
            m = jnp.maximum(jnp.max(sr, axis=-1, keepdims=True), sink)
            pexp = jnp.exp2(sr - m)
            den = jnp.sum(pexp, axis=-1, keepdims=True) + jnp.exp2(sink - m)
            outs.append(_dot(pexp.astype(BF16), vc) / den)
        c0 = 2 * g * LANES
        o_ref[r0:r0 + Wn, c0:c0 + LANES] = jnp.where(lane_w, outs[0], outs[1])
        o_ref[r0:r0 + Wn, c0 + LANES:c0 + 2 * LANES] = jnp.where(lane_w, outs[2], outs[3])

    return [(functools.partial(scores, j, g), functools.partial(finish, j, g))
            for j in range(T // Wn) for g in range(SWA_KV_HEADS)]


def _mixers_kernel(sink_ref, ml_ref, gt_ref, fb_ref, gm_ref, hb_ref, lf_ref, gh_ref, sw_ref,
                   ones_ref, onesf_ref, tril_ref, exp_ref, tril64_ref, pm_ref,
                   om_ref, oh_ref, os_ref,
                   c_st, n_st, m_st, s_st, k_prev, v_prev, *, layer, bounded):
    @pl.when(pl.program_id(1) == 0)
    def _():
        for ref in (c_st, n_st, m_st, s_st, k_prev, v_prev):
            ref[...] = jnp.zeros_like(ref)

    swa_blocks = _swa_blocks(sink_ref, sw_ref, os_ref, k_prev, v_prev, layer)
    n_sub = MIX_ROWS // HGRN_L
    assert len(swa_blocks) == n_sub and n_sub % (MIX_ROWS // MLSTM_T) == 0
    per_mlstm = n_sub // (MIX_ROWS // MLSTM_T)
    for sub in range(n_sub):
        swa_scores, swa_finish = swa_blocks[sub]
        s = swa_scores()
        _hgrn_chunk(sub * HGRN_L, hb_ref, lf_ref, gh_ref, ones_ref, onesf_ref, tril64_ref, pm_ref, oh_ref,
                    s_st, bounded)
        swa_finish(s)
        if sub % per_mlstm == 0:
            _mlstm_chunk((sub // per_mlstm) * MLSTM_T, ml_ref, gt_ref, fb_ref, gm_ref, ones_ref, onesf_ref,
                         tril_ref, exp_ref, om_ref, c_st, n_st, m_st)


def _mixers(ml, gt, hb, lf, sw, f_bias, mlstm_g, hgrn_g, sinks, layer, batch, consts, bounded):
    n = ml.shape[0]
    seq = n // batch
    T = MIX_ROWS
    steps = seq // T
    W4 = 4 * GROUP_W
    QW = SWA_Q_HEADS * HEAD_DIM
    row = lambda bi, ci: (bi * steps + ci, 0)
    lay = lambda bi, ci: (layer, 0, 0)
    cst = lambda bi, ci: (0, 0)
    blk_ones, tril, expand, tril64, pmask = consts
    return pl.pallas_call(
        functools.partial(_mixers_kernel, layer=layer, bounded=bounded),
        grid=(batch, steps),
        in_specs=[pl.BlockSpec(memory_space=pltpu.SMEM),
                  pl.BlockSpec((T, W4), row),
                  pl.BlockSpec((T, LANES), row),
                  pl.BlockSpec((None, 1, LANES), lay),
                  pl.BlockSpec((None, 1, GROUP_W), lay),
                  pl.BlockSpec((T, W4), row),
                  pl.BlockSpec((T, GROUP_W), row),
                  pl.BlockSpec((None, 1, GROUP_W), lay),
                  pl.BlockSpec((T, W4), row),
                  pl.BlockSpec((GROUP_W, GROUP_W), cst),
                  pl.BlockSpec((GROUP_W, GROUP_W), cst),
                  pl.BlockSpec((MLSTM_T, MLSTM_T), cst),
                  pl.BlockSpec((LANES, GROUP_W), cst),
                  pl.BlockSpec((HGRN_L, HGRN_L), cst),
                  pl.BlockSpec(pmask.shape, lambda bi, ci: (0, 0, 0))],
        out_specs=[pl.BlockSpec((T, GROUP_W), row),
                   pl.BlockSpec((T, GROUP_W), row),
                   pl.BlockSpec((T, QW), row)],
        out_shape=[jax.ShapeDtypeStruct((n, GROUP_W), F32),
                   jax.ShapeDtypeStruct((n, GROUP_W), F32),
                   jax.ShapeDtypeStruct((n, QW), F32)],
        scratch_shapes=[pltpu.VMEM((GROUP_W, GROUP_W), F32),
                        pltpu.VMEM((SUBLANES, GROUP_W), F32),
                        pltpu.VMEM((SUBLANES, LANES), F32),
                        pltpu.VMEM((GROUP_W, GROUP_W), F32),
                        pltpu.VMEM((SWA_KV_HEADS, WINDOW, LANES), BF16),
                        pltpu.VMEM((SWA_KV_HEADS, WINDOW, LANES), BF16)],
        compiler_params=_params(2),
        name="mixers",
    )(sinks, ml, gt, f_bias, mlstm_g, hb, lf, hgrn_g, sw, blk_ones, blk_ones.astype(F32), tril, expand,
      tril64, pmask)


def _tail_kernel(x_ref, mm_ref, mh_ref, ms_ref, p_ref, wo_ref, g2_ref, wu_ref, wd_ref, g3_ref,
                 wg_ref, wp_ref, g4_ref, o_ref):
    W = GROUP_W
    mix = _dot(mm_ref[...].astype(BF16), wo_ref[0:W, :])
    mix = mix + _dot(mh_ref[...].astype(BF16), wo_ref[W:2 * W, :])
    mix = mix + _dot(ms_ref[...].astype(BF16), wo_ref[2 * W:4 * W, :])
    x = x_ref[...] + mix
    emb = _rms(_dot(p_ref[...].astype(BF16), wp_ref[...]), g4_ref[...])

    h2 = _rms(x, g2_ref[...]).astype(BF16)
    mlp = None
    for c in range(D_FF // FF_CHUNK):
        u = _dot(h2, wu_ref[:, c * FF_CHUNK:(c + 1) * FF_CHUNK])
        act = jnp.square(jnp.maximum(u, 0.0)).astype(BF16)
        d = _dot(act, wd_ref[c * FF_CHUNK:(c + 1) * FF_CHUNK, :])
        mlp = d if mlp is None else mlp + d
    x = x + mlp

    h3 = _rms(x, g3_ref[...]).astype(BF16)
    for c in range(D_MODEL // GATE_CHUNK):
        cols = slice(c * GATE_CHUNK, (c + 1) * GATE_CHUNK)
        gate = _sigmoid(_dot(h3, wg_ref[:, cols]))
        o_ref[:, cols] = x[:, cols] + gate * emb[:, cols]


def _tail(x, mm, mh, ms, p, w_out, g2, w_up, w_down, g3, w_gate, w_proj, g4, layer):
    n = x.shape[0]
    tm = min(TAIL_TM, n)
    tiles = n // tm
    row = lambda i: (i, 0)
    lay = lambda i: (layer, 0, 0)
    once = pl.Buffered(1)

    def wspec(shape):
        return pl.BlockSpec((None,) + shape, lay, pipeline_mode=once)

    return pl.pallas_call(
        _tail_kernel,
        grid=(tiles,),
        in_specs=[pl.BlockSpec((tm, D_MODEL), row),
                  pl.BlockSpec((tm, GROUP_W), row),
                  pl.BlockSpec((tm, GROUP_W), row),
                  pl.BlockSpec((tm, 2 * GROUP_W), row),
                  pl.BlockSpec((None, tm, PLE_DIM), lambda i: (layer, i, 0)),
                  wspec((D_MODEL, D_MODEL)),
                  wspec((1, D_MODEL)),
                  wspec((D_MODEL, D_FF)),
                  wspec((D_FF, D_MODEL)),
                  wspec((1, D_MODEL)),
                  wspec((D_MODEL, D_MODEL)),
                  wspec((PLE_DIM, D_MODEL)),
                  wspec((1, D_MODEL))],
        out_specs=pl.BlockSpec((tm, D_MODEL), row),
        out_shape=jax.ShapeDtypeStruct((n, D_MODEL), F32),
        compiler_params=_params(1),
        name="tail",
    )(x, mm, mh, ms, p, w_out, g2, w_up, w_down, g3, w_gate, w_proj, g4)


def _tail_in_proj_kernel(x_ref, mm_ref, mh_ref, ms_ref, p_ref, wo_ref, g2_ref, wu_ref, wd_ref, g3_ref,
                         wg_ref, wp_ref, g4_ref,
                         g_ref, w_ref, b_ref, cw_ref, cb_ref, la_ref, l1m_ref, oml_ref,
                         cos_ref, sin_ref, qg_ref, kg_ref, ones_ref,
                         o_ref, ml_ref, gt_ref, hb_ref, lf_ref, sw_ref, hd_ref, xbuf, *, tiles_per_seq):
    _tail_kernel(x_ref, mm_ref, mh_ref, ms_ref, p_ref, wo_ref, g2_ref, wu_ref, wd_ref, g3_ref,
                 wg_ref, wp_ref, g4_ref, o_ref)
    _in_proj_kernel(o_ref, g_ref, w_ref, b_ref, cw_ref, cb_ref, la_ref, l1m_ref, oml_ref,
                    cos_ref, sin_ref, qg_ref, kg_ref, ones_ref,
                    ml_ref, gt_ref, hb_ref, lf_ref, sw_ref, hd_ref, xbuf, tiles_per_seq=tiles_per_seq)


def _tail_in_proj(x, mm, mh, ms, p, w_out, g2, w_up, w_down, g3, w_gate, w_proj, g4, layer,
                  g, w, b, conv_w, conv_b, la, l1m, oml, cos_t, sin_t, q_gain, k_gain, blk_ones, seq):
    n = x.shape[0]
    tm = min(TAIL_TM, n)
    tiles = n // tm
    nxt = layer + 1
    row = lambda i: (i, 0)
    lay = lambda i: (layer, 0, 0)
    lay1 = lambda i: (nxt, 0, 0)
    once = pl.Buffered(1)
    W4 = 4 * GROUP_W

    def wspec(shape):
        return pl.BlockSpec((None,) + shape, lay, pipeline_mode=once)

    def wspec1(shape):
        return pl.BlockSpec((None,) + shape, lay1, pipeline_mode=once)

    return pl.pallas_call(
        functools.partial(_tail_in_proj_kernel, tiles_per_seq=seq // tm),
        grid=(tiles,),
        in_specs=[pl.BlockSpec((tm, D_MODEL), row),
                  pl.BlockSpec((tm, GROUP_W), row),
                  pl.BlockSpec((tm, GROUP_W), row),
                  pl.BlockSpec((tm, 2 * GROUP_W), row),
                  pl.BlockSpec((None, tm, PLE_DIM), lambda i: (layer, i, 0)),
                  wspec((D_MODEL, D_MODEL)),
                  wspec((1, D_MODEL)),
                  wspec((D_MODEL, D_FF)),
                  wspec((D_FF, D_MODEL)),
                  wspec((1, D_MODEL)),
                  wspec((D_MODEL, D_MODEL)),
                  wspec((PLE_DIM, D_MODEL)),
                  wspec((1, D_MODEL)),
                  wspec1((1, D_MODEL)),
                  wspec1((D_MODEL, IN_COLS)),
                  wspec1((1, IN_COLS)),
                  wspec1((CONV_K, 2 * GROUP_W)),
                  wspec1((1, 2 * GROUP_W)),
                  wspec1((1, GROUP_W)),
                  wspec1((1, GROUP_W)),
                  wspec1((1, GROUP_W)),
                  pl.BlockSpec((tm, LANES), row),
                  pl.BlockSpec((tm, LANES), row),
                  wspec1((1, LANES)),
                  wspec1((1, LANES)),
                  pl.BlockSpec((GROUP_W, GROUP_W), lambda i: (0, 0), pipeline_mode=once)],
        out_specs=[pl.BlockSpec((tm, D_MODEL), row),
                   pl.BlockSpec((tm, W4), row),
                   pl.BlockSpec((tm, LANES), row),
                   pl.BlockSpec((tm, W4), row),
                   pl.BlockSpec((tm, GROUP_W), row),
                   pl.BlockSpec((tm, W4), row),
                   pl.BlockSpec((SUBLANES, LANES), row)],
        out_shape=[jax.ShapeDtypeStruct((n, D_MODEL), F32),
                   jax.ShapeDtypeStruct((n, W4), BF16),
                   jax.ShapeDtypeStruct((n, LANES), F32),
                   jax.ShapeDtypeStruct((n, W4), BF16),
                   jax.ShapeDtypeStruct((n, GROUP_W), F32),
                   jax.ShapeDtypeStruct((n, W4), BF16),
                   jax.ShapeDtypeStruct((tiles * SUBLANES, LANES), F32)],
        scratch_shapes=[pltpu.VMEM((tm + SUBLANES, 2 * GROUP_W), F32)],
        compiler_params=pltpu.CompilerParams(
            dimension_semantics=("arbitrary",), vmem_limit_bytes=VMEM_LIMIT_BYTES + 4 * 1024 * 1024),
        name="tail_in_proj",
    )(x, mm, mh, ms, p, w_out, g2, w_up, w_down, g3, w_gate, w_proj, g4,
      g, w, b, conv_w, conv_b, la, l1m, oml, cos_t, sin_t, q_gain, k_gain, blk_ones)


def kernel(x, p, positions, in_norm_g, w_in, b_in, mlstm_f_bias, mlstm_conv_w, mlstm_conv_b,
           mlstm_norm_g, hgrn_lb_logits, hgrn_norm_g, swa_q_norm_g, swa_k_norm_g, swa_sinks,
           w_out, mlp_norm_g, w_up, w_down, ple_norm_g, w_ple_gate, w_ple_proj, ple_post_norm_g):
    batch, seq, d_model = x.shape
    depth = w_in.shape[0]
    n = batch * seq
    assert d_model == D_MODEL and seq % max(MIX_ROWS, IN_TM) == 0

    n_raw = GATE_OFF + GATE_COLS
    pad = LANES - GATE_COLS
    w_in_p = jnp.concatenate(
        [w_in[..., :n_raw], jnp.zeros((depth, D_MODEL, pad), w_in.dtype), w_in[..., n_raw:]],
        axis=-1).astype(BF16)
    b_in_p = jnp.concatenate(
        [b_in[..., :n_raw], jnp.zeros((depth, pad), b_in.dtype), b_in[..., n_raw:]],
        axis=-1).astype(F32).reshape(depth, 1, IN_COLS)
    f_bias = jnp.zeros((depth, 1, LANES), F32).at[:, 0, N_HEADS:GATE_COLS].set(mlstm_f_bias.astype(F32))
    row3 = lambda a: a.astype(F32).reshape(depth, 1, a.shape[-1])
    q_gain = row3(jnp.tile(swa_q_norm_g, (1, LANES // HEAD_DIM)))
    k_gain = row3(jnp.tile(swa_k_norm_g, (1, LANES // HEAD_DIM)))
    w_out_b, w_up_b, w_down_b = w_out.astype(BF16), w_up.astype(BF16), w_down.astype(BF16)
    w_gate_b, w_proj_b = w_ple_gate.astype(BF16), w_ple_proj.astype(BF16)

    blk256 = _blk_ones(GROUP_W)
    tril = jnp.asarray(np.tril(np.ones((MLSTM_T, MLSTM_T), np.float32)), dtype=BF16)
    expand = jnp.asarray(
        (np.arange(LANES)[:, None] == (np.arange(GROUP_W) // HEAD_DIM)[None, :]).astype(np.float32),
        dtype=BF16)
    hgrn_tril, hgrn_pmask = _hgrn_constants()

    cos_t, sin_t = _rope_tables(positions)
    la, l1m, oml = [row3(a) for a in _hgrn_lower_bounds(hgrn_lb_logits)]
    worst_log2 = (HGRN_L // 2) * LOG2_E * jnp.min(la, axis=(1, 2))
    hgrn_bounded = worst_log2 > -HGRN_SAFE_LOG2
    sinks = swa_sinks.astype(F32)

    xf = x.reshape(n, D_MODEL)
    in_args = (row3(in_norm_g), w_in_p, b_in_p, mlstm_conv_w.astype(F32), row3(mlstm_conv_b),
               la, l1m, oml, cos_t, sin_t, q_gain, k_gain, blk256)
    ml, gt, hb, lf, sw, half_chunk_decay = _in_proj(xf, *in_args, 0, seq)
    for l in range(depth):
        mix_consts = (blk256, tril, expand, hgrn_tril, hgrn_pmask)
        mm, mh, ms = lax.cond(
            hgrn_bounded[l] | (jnp.max(half_chunk_decay) <= HGRN_SAFE_LOG2),
            lambda *a: _mixers(*a, l, batch, mix_consts, True),
            lambda *a: _mixers(*a, l, batch, mix_consts, False),
            ml, gt, hb, lf, sw, f_bias, row3(mlstm_norm_g), row3(hgrn_norm_g), sinks)
        tail_args = (xf, mm, mh, ms, p.reshape(depth, n, PLE_DIM), w_out_b, row3(mlp_norm_g), w_up_b,
                     w_down_b, row3(ple_norm_g), w_gate_b, w_proj_b, row3(ple_post_norm_g), l)
        if l + 1 < depth:
            xf, ml, gt, hb, lf, sw, half_chunk_decay = _tail_in_proj(*tail_args, *in_args, seq)
        else:
            xf = _tail(*tail_args)
    return xf.reshape(batch, seq, D_MODEL)
```
